```python
import math
import jax
import jax.numpy as jnp
from jax import lax
import numpy as np


D_MODEL = 1024
BATCH = 8
SEQ = 4096
DEPTH = 2

EPS = 1e-6
NEG_INF = -1e30
FORCED_SCORE = 1e9
NSA_HEADS = 8
NSA_KV_GROUPS = 2
NSA_HEAD_DIM = 64
CMP_BLOCK = 32
CMP_STRIDE = 16
CMP_HIDDEN = 256
SEL_BLOCK = 64
SEL_TOP_N = 16
WINDOW = 512
NSA_Q_BLOCK = 64
MLA_HEADS = 8
MLA_Q_RANK = 256
MLA_KV_RANK = 128
MLA_NOPE_DIM = 64
MLA_ROPE_DIM = 32
MLA_V_DIM = 64
ROPE_THETA = 10000.0
ATTN_Q_BLOCK = 128
REL_BUCKETS = 32
REL_MAX_DIST = 128
CONV_WIDTH = 3
D_FF = 2816
N_EXPERTS = 8
TOP_K = 2
D_FF_EXPERT = 1408
MOE_ROW_BLOCK = 256
NSA_Q_W = NSA_HEADS * NSA_HEAD_DIM
NSA_KV_W = NSA_KV_GROUPS * NSA_HEAD_DIM
NSA_GATE_W = NSA_HEADS * 3
EVEN_IN_SIZES = (NSA_Q_W,) + (NSA_KV_W,) * 6 + (NSA_GATE_W, MLA_Q_RANK, MLA_KV_RANK, MLA_ROPE_DIM)
EVEN_IN_W = sum(EVEN_IN_SIZES)
MIX_OUT_W = NSA_HEADS * NSA_HEAD_DIM + MLA_HEADS * MLA_V_DIM

kernel_name = 'hybrid_nsa_mla_shortconv_moe'


def rms_norm(x, g):
    xf = x.astype(jnp.float32)
    y = xf * lax.rsqrt(jnp.mean(jnp.square(xf), axis=-1, keepdims=True) + EPS)
    return (y * g.astype(jnp.float32)).astype(x.dtype)


def masked_softmax(s, mask):
    p = jax.nn.softmax(jnp.where(mask, s, NEG_INF), axis=-1)
    return jnp.where(mask, p, 0.0)


def t5_bucket(dist):
    n = jnp.maximum(dist, 0)
    max_exact = REL_BUCKETS // 2
    nf = jnp.maximum(n, max_exact).astype(jnp.float32)
    large = max_exact + (jnp.log(nf / max_exact) / math.log(REL_MAX_DIST / max_exact)
                         * (REL_BUCKETS - max_exact)).astype(jnp.int32)
    return jnp.where(n < max_exact, n, jnp.minimum(large, REL_BUCKETS - 1))


def rope(x, pos):
    half = x.shape[-1] // 2
    inv_freq = ROPE_THETA ** (-jnp.arange(half, dtype=jnp.float32) / half)
    ang = pos.astype(jnp.float32)[:, None] * inv_freq[None, :]
    cos = jnp.cos(ang)[:, None, :]
    sin = jnp.sin(ang)[:, None, :]
    xf = x.astype(jnp.float32)
    x1, x2 = xf[..., :half], xf[..., half:]
    return jnp.concatenate([x1 * cos - x2 * sin, x1 * sin + x2 * cos], axis=-1).astype(x.dtype)


def compress(kv, pos_emb, w1, w2):
    B, S, G, dk = kv.shape
    nc = (S - CMP_BLOCK) // CMP_STRIDE + 1
    idx = jnp.arange(nc)[:, None] * CMP_STRIDE + jnp.arange(CMP_BLOCK)[None, :]
    blk = kv[:, idx] + pos_emb[None, None, :, None, :]
    blk = blk.transpose(0, 1, 3, 2, 4).reshape(B, nc, G, CMP_BLOCK * dk)
    return jax.nn.gelu(blk @ w1) @ w2


def nsa_mixer(q, k_cmp, v_cmp, k_sel, v_sel, k_win, v_win, gates, rel_bias,
              q_norm, k_norm, cmp_pos, cmp_w1, cmp_w2):
    B, S = q.shape[0], q.shape[1]
    G, R, dk = NSA_KV_GROUPS, NSA_HEADS // NSA_KV_GROUPS, NSA_HEAD_DIM
    QB, L = NSA_Q_BLOCK, WINDOW + NSA_Q_BLOCK
    scale = dk ** -0.5
    qn = rms_norm(q, q_norm).reshape(B, S, G, R, dk)
    kc = rms_norm(compress(k_cmp, cmp_pos[0], cmp_w1[0], cmp_w2[0]), k_norm[0])
    vc = compress(v_cmp, cmp_pos[1], cmp_w1[1], cmp_w2[1])
    nc = kc.shape[1]
    ns = S // SEL_BLOCK
    n_top = min(SEL_TOP_N, ns)

    def to_blocks(a):
        return a.reshape(B, ns, SEL_BLOCK, G, dk).transpose(0, 3, 1, 2, 4).reshape(B, G, ns, SEL_BLOCK * dk)

    ks_blk = to_blocks(rms_norm(k_sel, k_norm[1]))
    vs_blk = to_blocks(v_sel)
    pad = ((0, 0), (WINDOW, 0), (0, 0), (0, 0))
    kw_pad = jnp.pad(rms_norm(k_win, k_norm[2]), pad)
    vw_pad = jnp.pad(v_win, pad)
    gate_all = gates.reshape(B, S, G, R, 3)

    cmp_start = jnp.arange(nc) * CMP_STRIDE
    cmp_end = cmp_start + CMP_BLOCK - 1
    sel_start = jnp.arange(ns) * SEL_BLOCK
    overlap = ((cmp_start[:, None] < sel_start[None, :] + SEL_BLOCK)
               & (cmp_end[:, None] >= sel_start[None, :])).astype(jnp.float32)
    bias_gr = rel_bias.reshape(REL_BUCKETS, G, R)
    g_idx = jnp.arange(G)[None, :, None, None, None]
    blk_ids = jnp.arange(ns)
    offs_sel = jnp.arange(SEL_BLOCK)
    flat = n_top * SEL_BLOCK

    def one_block(i):
        q0 = i * QB
        t = q0 + jnp.arange(QB)
        qb = lax.dynamic_slice_in_dim(qn, q0, QB, axis=1)
        d_c = t[:, None] - cmp_end[None, :]
        s_c = jnp.einsum('bqgrd,bcgd->bgrqc', qb, kc).astype(jnp.float32) * scale
        s_c = s_c + bias_gr[t5_bucket(d_c)].transpose(2, 3, 0, 1).astype(jnp.float32)
        p_c = masked_softmax(s_c, d_c >= 0)
        o_c = jnp.einsum('bgrqc,bcgd->bqgrd', p_c.astype(vc.dtype), vc)
        imp = jnp.einsum('bgrqc,cj->bgqj', p_c, overlap)
        forced = (blk_ids[None, :] == (t // SEL_BLOCK)[:, None]) | (blk_ids[None, :] == 0)
        imp = jnp.where(forced, FORCED_SCORE, imp)
        imp = jnp.where(sel_start[None, :] <= t[:, None], imp, -1.0)
        _, sel = lax.top_k(imp, n_top)
        sel_flat = sel.reshape(B, G, QB * n_top, 1)
        k_g = jnp.take_along_axis(ks_blk, sel_flat, axis=2).reshape(B, G, QB, n_top, SEL_BLOCK, dk)
        v_g = jnp.take_along_axis(vs_blk, sel_flat, axis=2).reshape(B, G, QB, flat, dk)
        s_pos = sel[..., None] * SEL_BLOCK + offs_sel
        d_s = t[None, None, :, None, None] - s_pos
        s_s = jnp.einsum('bqgrd,bgqnkd->bgrqnk', qb, k_g).astype(jnp.float32) * scale
        s_s = s_s + jnp.moveaxis(bias_gr[t5_bucket(d_s), g_idx], -1, 2).astype(jnp.float32)
        p_s = masked_softmax(s_s.reshape(B, G, R, QB, flat), (d_s >= 0).reshape(B, G, 1, QB, flat))
        o_s = jnp.einsum('bgrqk,bgqkd->bqgrd', p_s.astype(v_g.dtype), v_g)
        k_w = lax.dynamic_slice_in_dim(kw_pad, q0, L, axis=1)
        v_w = lax.dynamic_slice_in_dim(vw_pad, q0, L, axis=1)
        w_pos = q0 - WINDOW + jnp.arange(L)
        d_w = t[:, None] - w_pos[None, :]
        s_w = jnp.einsum('bqgrd,bkgd->bgrqk', qb, k_w).astype(jnp.float32) * scale
        s_w = s_w + bias_gr[t5_bucket(d_w)].transpose(2, 3, 0, 1).astype(jnp.float32)
        p_w = masked_softmax(s_w, (d_w >= 0) & (d_w < WINDOW) & (w_pos[None, :] >= 0))
        o_w = jnp.einsum('bgrqk,bkgd->bqgrd', p_w.astype(v_w.dtype), v_w)
        g = jax.nn.sigmoid(lax.dynamic_slice_in_dim(gate_all, q0, QB, axis=1).astype(jnp.float32))
        o = g[..., 0:1] * o_c + g[..., 1:2] * o_s + g[..., 2:3] * o_w
        return o.astype(q.dtype)

    out = lax.map(one_block, jnp.arange(S // QB))
    return out.transpose(1, 0, 2, 3, 4, 5).reshape(B, S, NSA_HEADS * dk)


def causal_block_attention(q, k, v):
    B, S, H, dh = q.shape
    scale = dh ** -0.5
    key_pos = jnp.arange(S)

    def one_block(i):
        q0 = i * ATTN_Q_BLOCK
        qb = lax.dynamic_slice_in_dim(q, q0, ATTN_Q_BLOCK, axis=1)
        s = jnp.einsum('bqhd,bkhd->bhqk', qb, k).astype(jnp.float32) * scale
        mask = (q0 + jnp.arange(ATTN_Q_BLOCK))[:, None] >= key_pos[None, :]
        p = masked_softmax(s, mask)
        return jnp.einsum('bhqk,bkhd->bqhd', p.astype(v.dtype), v)

    out = lax.map(one_block, jnp.arange(S // ATTN_Q_BLOCK))
    return out.transpose(1, 0, 2, 3, 4).reshape(B, S, H, v.shape[-1])


def mla_mixer(c_q, c_kv, k_rope, cq_norm, ckv_norm, w_uq, w_ukv, q_norm, k_norm):
    B, S = c_q.shape[0], c_q.shape[1]
    H, dn, dr, dv = MLA_HEADS, MLA_NOPE_DIM, MLA_ROPE_DIM, MLA_V_DIM
    pos = jnp.arange(S)
    q = (rms_norm(c_q, cq_norm) @ w_uq).reshape(B, S, H, dn + dr)
    kv = (rms_norm(c_kv, ckv_norm) @ w_ukv).reshape(B, S, H, dn + dv)
    k = jnp.concatenate([kv[..., :dn], jnp.broadcast_to(k_rope[:, :, None, :], (B, S, H, dr))], axis=-1)
    v = kv[..., dn:]
    q = rms_norm(q, q_norm)
    k = rms_norm(k, k_norm)
    q = jnp.concatenate([q[..., :dn], rope(q[..., dn:], pos)], axis=-1)
    k = jnp.concatenate([k[..., :dn], rope(k[..., dn:], pos)], axis=-1)
    return causal_block_attention(q, k, v).reshape(B, S, H * dv)


def short_conv_mixer(u, b_gate, c_gate, conv_w):
    y = lax.conv_general_dilated(c_gate * u, conv_w[:, None, :], window_strides=(1,),
                                 padding=[(CONV_WIDTH - 1, 0)],
                                 dimension_numbers=('NWC', 'WIO', 'NWC'),
                                 feature_group_count=u.shape[-1])
    return b_gate * y


def swiglu(x, w_gate, w_up, w_down):
    return (jax.nn.silu(x @ w_gate) * (x @ w_up)) @ w_down


def moe_swiglu(x, w_router, b_router, w_gate, w_up, w_down):
    B, S, D = x.shape
    T = B * S
    RB = MOE_ROW_BLOCK
    xt = x.reshape(T, D)
    logits = (xt @ w_router).astype(jnp.float32) + b_router.astype(jnp.float32)
    top_logit, top_e = lax.top_k(logits, TOP_K)
    top_w = jax.nn.softmax(top_logit, axis=-1)
    P = T * TOP_K
    e_flat = top_e.reshape(P)
    tok_flat = jnp.arange(P) // TOP_K
    w_flat = top_w.reshape(P)
    order = jnp.argsort(e_flat)
    e_s, tok_s, w_s = e_flat[order], tok_flat[order], w_flat[order]
    counts = jnp.bincount(e_flat, length=N_EXPERTS)
    padded = (counts + RB - 1) // RB * RB
    start = jnp.cumsum(counts) - counts
    pad_end = jnp.cumsum(padded)
    pad_start = pad_end - padded
    dest = pad_start[e_s] + jnp.arange(P) - start[e_s]
    n_blk = (P + RB - 1) // RB + N_EXPERTS
    n_rows = n_blk * RB
    tok_buf = jnp.zeros((n_rows,), jnp.int32).at[dest].set(tok_s)
    w_buf = jnp.zeros((n_rows,), jnp.float32).at[dest].set(w_s)
    blk_expert = jnp.minimum(jnp.sum(pad_end[None, :] <= (jnp.arange(n_blk) * RB)[:, None], axis=1),
                             N_EXPERTS - 1)

    def one_block(args):
        e, toks, w = args
        xb = xt[toks]
        h = jax.nn.silu(xb @ w_gate[e]) * (xb @ w_up[e])
        return (h @ w_down[e]) * w[:, None].astype(h.dtype)

    out = lax.map(one_block, (blk_expert, tok_buf.reshape(n_blk, RB), w_buf.reshape(n_blk, RB)))
    y = jnp.zeros((T, D), x.dtype).at[tok_buf].add(out.reshape(n_rows, D).astype(x.dtype))
    return y.reshape(B, S, D)


def setup_inputs(seed: int = 0) -> dict:
    key = jax.random.key(seed)
    keys = iter(jax.random.split(key, 32))
    ne, no = (DEPTH + 1) // 2, DEPTH // 2

    def dense(shape, fan_in):
        return jax.random.normal(next(keys), shape, jnp.float32) * (fan_in ** -0.5)

    def gain(shape):
        return 1.0 + 0.02 * jax.random.normal(next(keys), shape, jnp.float32)

    def small(shape, scale):
        return scale * jax.random.normal(next(keys), shape, jnp.float32)

    cmp_in = CMP_BLOCK * NSA_HEAD_DIM
    mla_qk = MLA_NOPE_DIM + MLA_ROPE_DIM
    return {
        'x': jax.random.normal(next(keys), (BATCH, SEQ, D_MODEL), jnp.float32),
        'rel_bias': small((REL_BUCKETS, NSA_HEADS), 0.1),
        'ev_mix_norm': gain((ne, D_MODEL)),
        'ev_w_in': dense((ne, D_MODEL, EVEN_IN_W), D_MODEL),
        'nsa_q_norm': gain((ne, NSA_HEAD_DIM)),
        'nsa_k_norm': gain((ne, 3, NSA_HEAD_DIM)),
        'nsa_cmp_pos': small((ne, 2, CMP_BLOCK, NSA_HEAD_DIM), 0.02),
        'nsa_cmp_w1': dense((ne, 2, cmp_in, CMP_HIDDEN), cmp_in),
        'nsa_cmp_w2': dense((ne, 2, CMP_HIDDEN, NSA_HEAD_DIM), CMP_HIDDEN),
        'mla_cq_norm': gain((ne, MLA_Q_RANK)),
        'mla_ckv_norm': gain((ne, MLA_KV_RANK)),
        'mla_w_uq': dense((ne, MLA_Q_RANK, MLA_HEADS * mla_qk), MLA_Q_RANK),
        'mla_w_ukv': dense((ne, MLA_KV_RANK, MLA_HEADS * (MLA_NOPE_DIM + MLA_V_DIM)), MLA_KV_RANK),
        'mla_q_norm': gain((ne, mla_qk)),
        'mla_k_norm': gain((ne, mla_qk)),
        'ev_w_out': dense((ne, MIX_OUT_W, D_MODEL), MIX_OUT_W),
        'ev_ffn_norm': gain((ne, D_MODEL)),
        'ffn_w_gate': dense((ne, D_MODEL, D_FF), D_MODEL),
        'ffn_w_up': dense((ne, D_MODEL, D_FF), D_MODEL),
        'ffn_w_down': dense((ne, D_FF, D_MODEL), D_FF),
        'od_mix_norm': gain((no, D_MODEL)),
        'od_w_in': dense((no, D_MODEL, 3 * D_MODEL), D_MODEL),
        'conv_w': dense((no, CONV_WIDTH, D_MODEL), CONV_WIDTH),
        'od_w_out': dense((no, D_MODEL, D_MODEL), D_MODEL),
        'od_ffn_norm': gain((no, D_MODEL)),
        'moe_w_router': dense((no, D_MODEL, N_EXPERTS), D_MODEL),
        'moe_b_router': small((no, N_EXPERTS), 0.01),
        'moe_w_gate': dense((no, N_EXPERTS, D_MODEL, D_FF_EXPERT), D_MODEL),
        'moe_w_up': dense((no, N_EXPERTS, D_MODEL, D_FF_EXPERT), D_MODEL),
        'moe_w_down': dense((no, N_EXPERTS, D_FF_EXPERT, D_MODEL), D_FF_EXPERT),
    }


def reference(x, rel_bias, ev_mix_norm, ev_w_in, nsa_q_norm, nsa_k_norm, nsa_cmp_pos, nsa_cmp_w1,
              nsa_cmp_w2, mla_cq_norm, mla_ckv_norm, mla_w_uq, mla_w_ukv, mla_q_norm, mla_k_norm,
              ev_w_out, ev_ffn_norm, ffn_w_gate, ffn_w_up, ffn_w_down, od_mix_norm, od_w_in, conv_w,
              od_w_out, od_ffn_norm, moe_w_router, moe_b_router, moe_w_gate, moe_w_up, moe_w_down):
    B, S, _ = x.shape
    split_at = np.cumsum(EVEN_IN_SIZES)[:-1].tolist()
    kv_shape = (B, S, NSA_KV_GROUPS, NSA_HEAD_DIM)
    for layer in range(DEPTH):
        i = layer // 2
        if layer % 2 == 0:
            h = rms_norm(x, ev_mix_norm[i]) @ ev_w_in[i]
            (q, k_c, v_c, k_s, v_s, k_w, v_w, gates, c_q, c_kv, k_rope) = jnp.split(h, split_at, axis=-1)
            o_nsa = nsa_mixer(q.reshape(B, S, NSA_HEADS, NSA_HEAD_DIM),
                              k_c.reshape(kv_shape), v_c.reshape(kv_shape),
                              k_s.reshape(kv_shape), v_s.reshape(kv_shape),
                              k_w.reshape(kv_shape), v_w.reshape(kv_shape),
                              gates, rel_bias, nsa_q_norm[i], nsa_k_norm[i],
                              nsa_cmp_pos[i], nsa_cmp_w1[i], nsa_cmp_w2[i])
            o_mla = mla_mixer(c_q, c_kv, k_rope, mla_cq_norm[i], mla_ckv_norm[i],
                              mla_w_uq[i], mla_w_ukv[i], mla_q_norm[i], mla_k_norm[i])
            x = x + jnp.concatenate([o_nsa, o_mla], axis=-1) @ ev_w_out[i]
            x = x + swiglu(rms_norm(x, ev_ffn_norm[i]), ffn_w_gate[i], ffn_w_up[i], ffn_w_down[i])
        else:
            b_g, c_g, u = jnp.split(rms_norm(x, od_mix_norm[i]) @ od_w_in[i], 3, axis=-1)
            x = x + short_conv_mixer(u, b_g, c_g, conv_w[i]) @ od_w_out[i]
            x = x + moe_swiglu(rms_norm(x, od_ffn_norm[i]), moe_w_router[i], moe_b_router[i],
                               moe_w_gate[i], moe_w_up[i], moe_w_down[i])
    return x
```

```python
import functools
import math

import jax
import jax.numpy as jnp
import numpy as np
from jax import lax
from jax.experimental import pallas as pl
from jax.experimental.pallas import tpu as pltpu

F32 = jnp.float32
BF16 = jnp.bfloat16

EPS = 1e-6
NEG = -1e30
FORCED_SCORE = 1e9
NSA_HEADS = 8
NSA_GROUPS = 2
NSA_REP = NSA_HEADS // NSA_GROUPS
NSA_DK = 64
CMP_BLOCK = 32
CMP_STRIDE = 16
CMP_HIDDEN = 256
SEL_BLOCK = 64
SEL_TOP_N = 16
WINDOW = 512
MLA_HEADS = 8
MLA_Q_RANK = 256
MLA_KV_RANK = 128
MLA_NOPE = 64
MLA_ROPE = 32
MLA_V = 64
ROPE_THETA = 10000.0
REL_BUCKETS = 32
REL_MAX_DIST = 128
CONV_WIDTH = 3
N_EXPERTS = 8
TOP_K = 2
EVEN_IN_SIZES = (512,) + (128,) * 6 + (24, 256, 128, 32)

LANE = 128
VMEM_LIMIT = 56 * 1024 * 1024
SEL_MASK = -30000.0
FAR_DIST = 128

_Q0, _KV0, _KC0, _VC0, _GT0, _CQ0, _CKV0, _KR0, _EVEN_W = 0, 1024, 2048, 2176, 2304, 2560, 2816, 2944, 3072


def _dot(a, b):
    return jnp.dot(a, b, preferred_element_type=F32)


def _dot_nt(a, b):
    return lax.dot_general(a, b, (((1,), (1,)), ((), ())), preferred_element_type=F32)


def _rms(x, gain, n):
    ss = jnp.sum(x * x, axis=-1, keepdims=True) * (1.0 / n)
    return x * lax.rsqrt(ss + EPS) * gain


def _params(*sem):
    return pltpu.CompilerParams(dimension_semantics=sem, vmem_limit_bytes=VMEM_LIMIT)


def _const_spec(shape):
    nd = len(shape)
    return pl.BlockSpec(shape, lambda *_: (0,) * nd, pipeline_mode=pl.Buffered(1))


def _rope(x, cos, sa, sb):
    return x * cos + pltpu.roll(x, 16, 1) * sa + pltpu.roll(x, LANE - 16, 1) * sb


def _even_in_kernel(x_ref, gmix_ref, w_ref, gq_ref, gks_ref, gkw_ref, gcq_ref, gckv_ref,
                    wuq_ref, wuk_ref, wuv_ref, gmq_ref, gmk_ref, cos_ref, sa_ref, sb_ref,
                    qn_ref, kv_ref, kc_ref, vc_ref, gate_ref, qm_ref, km_ref, vm_ref, *, tm):
    j = pl.program_id(1)

    @pl.when(j == 0)
    def _():
        kv_ref[...] = jnp.zeros_like(kv_ref)
        qn_ref[...] = jnp.zeros_like(qn_ref)
        kc_ref[...] = jnp.zeros_like(kc_ref)
        vc_ref[...] = jnp.zeros_like(vc_ref)
        gate_ref[...] = jnp.zeros_like(gate_ref)
        qm_ref[...] = jnp.zeros_like(qm_ref)
        km_ref[...] = jnp.zeros_like(km_ref)
        vm_ref[...] = jnp.zeros_like(vm_ref)

    @pl.when(j > 0)
    def _():
        xn = _rms(x_ref[...], gmix_ref[...], x_ref.shape[-1]).astype(BF16)
        hq = _dot(xn, w_ref[:, _Q0:_Q0 + 1024])
        gq = gq_ref[...]
        for h in range(NSA_HEADS):
            seg = hq[:, LANE * h:LANE * (h + 1)]
            qn_ref[:, LANE * h:LANE * (h + 1)] = (_rms(seg, gq, NSA_DK) * NSA_DK ** -0.5).astype(BF16)
        hkv = _dot(xn, w_ref[:, _KV0:_KV0 + 1024])
        pos = (j - 1) * tm + lax.broadcasted_iota(jnp.int32, (tm, LANE), 0)
        lane = lax.broadcasted_iota(jnp.int32, (tm, LANE), 1)
        onehot = jnp.where(lane - NSA_DK == pos // SEL_BLOCK, 1.0, 0.0)
        for g in range(NSA_GROUPS):
            o = 4 * LANE * g
            kv_ref[:, o:o + LANE] = (_rms(hkv[:, o:o + LANE], gks_ref[...], NSA_DK) + onehot).astype(BF16)
            kv_ref[:, o + LANE:o + 2 * LANE] = hkv[:, o + LANE:o + 2 * LANE].astype(BF16)
            kv_ref[:, o + 2 * LANE:o + 3 * LANE] = _rms(hkv[:, o + 2 * LANE:o + 3 * LANE], gkw_ref[...],
                                                        NSA_DK).astype(BF16)
            kv_ref[:, o + 3 * LANE:o + 4 * LANE] = hkv[:, o + 3 * LANE:o + 4 * LANE].astype(BF16)
        hc = _dot(xn, w_ref[:, _KC0:_KC0 + 256])
        kc_ref[...] = hc[:, :LANE].astype(BF16)
        vc_ref[...] = hc[:, LANE:].astype(BF16)
        gate_ref[...] = jax.nn.sigmoid(_dot(xn, w_ref[:, _GT0:_GT0 + 256]))
        hm = _dot(xn, w_ref[:, _CQ0:_EVEN_W])
        cq = _rms(hm[:, :MLA_Q_RANK], gcq_ref[...], MLA_Q_RANK).astype(BF16)
        ckv = _rms(hm[:, MLA_Q_RANK:MLA_Q_RANK + MLA_KV_RANK], gckv_ref[...], MLA_KV_RANK).astype(BF16)
        k_rope = hm[:, MLA_Q_RANK + MLA_KV_RANK:]
        qm = _dot(cq, wuq_ref[...])
        kn = _dot(ckv, wuk_ref[...])
        vm_ref[...] = _dot(ckv, wuv_ref[...]).astype(BF16)
        cos, sa, sb = cos_ref[...], sa_ref[...], sb_ref[...]
        dqk = MLA_NOPE + MLA_ROPE
        for h in range(MLA_HEADS):
            sl = slice(LANE * h, LANE * (h + 1))
            qh = _rope(_rms(qm[:, sl], gmq_ref[...], dqk), cos, sa, sb) * dqk ** -0.5
            qm_ref[:, sl] = qh.astype(BF16)
            kh = _rope(_rms(kn[:, sl] + k_rope, gmk_ref[...], dqk), cos, sa, sb)
            km_ref[:, sl] = kh.astype(BF16)


def _even_in_weights(w_in):
    offs = np.concatenate([[0], np.cumsum(EVEN_IN_SIZES)])
    part = [w_in[:, offs[n]:offs[n + 1]] for n in range(len(EVEN_IN_SIZES))]
    q, k_c, v_c, k_s, v_s, k_w, v_w, gates, c_q, c_kv, k_rope = part
    d = w_in.shape[0]
    z = lambda n: jnp.zeros((d, n), w_in.dtype)
    cols = []
    for h in range(NSA_HEADS):
        cols += [q[:, 64 * h:64 * h + 64], z(64)]
    for g in range(NSA_GROUPS):
        s = slice(64 * g, 64 * g + 64)
        cols += [k_s[:, s], z(64), v_s[:, s], v_s[:, s], k_w[:, s], z(64), v_w[:, s], v_w[:, s]]
    cols += [k_c, v_c]
    for g in range(NSA_GROUPS):
        cols += [gates[:, 12 * g:12 * g + 12], z(LANE - 12)]
    cols += [c_q, c_kv, z(64), k_rope, z(32)]
    w = jnp.concatenate(cols, axis=1)
    assert w.shape[1] == _EVEN_W
    return w.astype(BF16)


def _pad_gain(g, width):
    return jnp.pad(g.astype(F32), (0, width - g.shape[0]))[None, :]


def _rope_tables(seq):
    half = MLA_ROPE // 2
    inv_freq = ROPE_THETA ** (-jnp.arange(half, dtype=F32) / half)
    ang = jnp.arange(seq).astype(F32)[:, None] * inv_freq[None, :]
    cos, sin = jnp.cos(ang), jnp.sin(ang)
    one = jnp.ones((seq, MLA_NOPE), F32)
    zn = jnp.zeros((seq, MLA_NOPE), F32)
    zt = jnp.zeros((seq, LANE - MLA_NOPE - MLA_ROPE), F32)
    zh = jnp.zeros((seq, half), F32)
    cos_t = jnp.concatenate([one, cos, cos, zt + 1.0], axis=1)
    sa = jnp.concatenate([zn, zh, sin, zt], axis=1)
    sb = jnp.concatenate([zn, -sin, zh, zt], axis=1)
    return cos_t, sa, sb


def _even_in_proj(x, gmix, w_in, q_norm, k_norm, cq_norm, ckv_norm, w_uq, w_ukv, mq_norm, mk_norm, tm=512):
    bsz, seq, d = x.shape
    nt = seq // tm
    w = _even_in_weights(w_in)
    dqk = MLA_NOPE + MLA_ROPE
    wuq = jnp.concatenate(
        [jnp.pad(w_uq[:, dqk * h:dqk * (h + 1)], ((0, 0), (0, LANE - dqk))) for h in range(MLA_HEADS)],
        axis=1).astype(BF16)
    kvw = MLA_NOPE + MLA_V
    wuk = jnp.concatenate(
        [jnp.pad(w_ukv[:, kvw * h:kvw * h + MLA_NOPE], ((0, 0), (0, LANE - MLA_NOPE))) for h in range(MLA_HEADS)],
        axis=1).astype(BF16)
    wuv = jnp.concatenate([w_ukv[:, kvw * h + MLA_NOPE:kvw * (h + 1)] for h in range(MLA_HEADS)], axis=1).astype(BF16)
    cos_t, sa, sb = _rope_tables(seq)
    tok = lambda width: pl.BlockSpec((None, tm, width), lambda b, j: (b, jnp.maximum(j - 1, 0), 0))
    postab = pl.BlockSpec((tm, LANE), lambda b, j: (jnp.maximum(j - 1, 0), 0))
    in_specs = [tok(d), _const_spec((1, d)), _const_spec((d, _EVEN_W)),
                _const_spec((1, LANE)), _const_spec((1, LANE)), _const_spec((1, LANE)),
                _const_spec((1, MLA_Q_RANK)), _const_spec((1, MLA_KV_RANK)),
                _const_spec(wuq.shape), _const_spec(wuk.shape), _const_spec(wuv.shape),
                _const_spec((1, LANE)), _const_spec((1, LANE)), postab, postab, postab]
    out_shape = [jax.ShapeDtypeStruct((bsz, seq, 1024), BF16),
                 jax.ShapeDtypeStruct((bsz, seq + tm, 1024), BF16),
                 jax.ShapeDtypeStruct((bsz, seq, LANE), BF16),
                 jax.ShapeDtypeStruct((bsz, seq, LANE), BF16),
                 jax.ShapeDtypeStruct((bsz, seq, 2 * LANE), F32),
                 jax.ShapeDtypeStruct((bsz, seq, 1024), BF16),
                 jax.ShapeDtypeStruct((bsz, seq, 1024), BF16),
                 jax.ShapeDtypeStruct((bsz, seq, 512), BF16)]
    out_specs = [tok(1024), pl.BlockSpec((None, tm, 1024), lambda b, j: (b, j, 0)), tok(LANE), tok(LANE),
                 tok(2 * LANE), tok(1024), tok(1024), tok(512)]
    return pl.pallas_call(
        functools.partial(_even_in_kernel, tm=tm), grid=(bsz, nt + 1), in_specs=in_specs, out_specs=out_specs,
        out_shape=out_shape, compiler_params=_params("parallel", "arbitrary"), name="even_in_proj",
    )(x, gmix[None, :], w, _pad_gain(q_norm, LANE), _pad_gain(k_norm[1], LANE), _pad_gain(k_norm[2], LANE),
      cq_norm[None, :], ckv_norm[None, :], wuq, wuk, wuv, _pad_gain(mq_norm, LANE), _pad_gain(mk_norm, LANE),
      cos_t, sa, sb)


def _cmp_kernel(x_ref, wa_ref, wb_ref, pa_ref, pb_ref, w2_ref, gain_ref, o_ref, *, normalize):
    x = x_ref[...]
    ua = _dot(x, wa_ref[...])
    ub = _dot(x, wb_ref[...])
    pt = _dot(pa_ref[...], wa_ref[...]) + _dot(pb_ref[...], wb_ref[...])
    n = ub.shape[0]
    pre = ua + pltpu.roll(ub, n - 1, 0) + pt[0:1]
    hid = jax.nn.gelu(pre).astype(BF16)
    out = _dot(hid, w2_ref[...])
    if normalize:
        gain = gain_ref[...]
        for g in range(NSA_GROUPS):
            sl = slice(LANE * g, LANE * (g + 1))
            o_ref[:, sl] = _rms(out[:, sl], gain, NSA_DK).astype(BF16)
    else:
        o_ref[...] = out.astype(BF16)


def _compress(kv, pos_emb, w1, w2, gain, normalize):
    bsz, seq, _ = kv.shape
    half = CMP_BLOCK // 2
    assert CMP_STRIDE == half
    nrow = seq // CMP_STRIDE
    x = kv.reshape(bsz, nrow, CMP_STRIDE * LANE)
    w1r = w1.reshape(CMP_BLOCK, NSA_DK, CMP_HIDDEN)
    eye = jnp.eye(NSA_GROUPS, dtype=w1.dtype)
    widen = lambda w: jnp.einsum("ldn,gh->lgdhn", w, eye).reshape(half * LANE, NSA_GROUPS * CMP_HIDDEN).astype(BF16)
    wa, wb = widen(w1r[:half]), widen(w1r[half:])
    prow = lambda p: jnp.pad(jnp.broadcast_to(p[:, None, :], (half, NSA_GROUPS, NSA_DK)).reshape(1, half * LANE),
                             ((0, 15), (0, 0))).astype(BF16)
    pa, pb = prow(pos_emb[:half]), prow(pos_emb[half:])
    second = jnp.zeros_like(w2) if normalize else w2
    w2w = jnp.einsum("nd,gh->gnhd", jnp.concatenate([w2, second], axis=1), eye)
    w2w = w2w.reshape(NSA_GROUPS * CMP_HIDDEN, NSA_GROUPS * LANE).astype(BF16)
    return pl.pallas_call(
        functools.partial(_cmp_kernel, normalize=normalize), grid=(bsz,),
        in_specs=[pl.BlockSpec((None, nrow, CMP_STRIDE * LANE), lambda b: (b, 0, 0)),
                  _const_spec(wa.shape), _const_spec(wb.shape), _const_spec(pa.shape), _const_spec(pb.shape),
                  _const_spec(w2w.shape), _const_spec((1, LANE))],
        out_specs=pl.BlockSpec((None, nrow, NSA_GROUPS * LANE), lambda b: (b, 0, 0)),
        out_shape=jax.ShapeDtypeStruct((bsz, nrow, NSA_GROUPS * LANE), BF16),
        compiler_params=_params("parallel"), name="nsa_compress",
    )(x, wa, wb, pa, pb, w2w, _pad_gain(gain, LANE))


def _bucket_table():
    dist = np.arange(FAR_DIST + 1)
    max_exact = REL_BUCKETS // 2
    nf = np.maximum(dist, max_exact).astype(np.float32)
    large = max_exact + (np.log(nf / max_exact) / math.log(REL_MAX_DIST / max_exact)
                         * (REL_BUCKETS - max_exact)).astype(np.int32)
    return np.where(dist < max_exact, dist, np.minimum(large, REL_BUCKETS - 1))


def _bias_tables(rel_bias, seq):
    tbl = rel_bias.astype(F32)[_bucket_table()].T
    far = tbl[:, FAR_DIST][:, None, None]

    def lookup(dist, valid):
        vals = jnp.take(tbl, np.clip(dist, 0, FAR_DIST), axis=1)
        return vals, jnp.asarray(valid)[None]

    a = np.arange(SEL_BLOCK)[:, None]
    t = np.arange(seq)[:, None]
    ncp = seq // CMP_STRIDE
    d_c = t - (np.arange(ncp)[None, :] * CMP_STRIDE + CMP_BLOCK - 1)
    v, ok = lookup(d_c, d_c >= 0)
    bias_cmp = jnp.where(ok, v, NEG)
    b = np.arange(4 * SEL_BLOCK)[None, :]
    d_n = 3 * SEL_BLOCK + a - b
    v, ok = lookup(d_n, d_n >= 0)
    bias_near = jnp.where(ok, v - far, NEG)
    b = np.arange(WINDOW + SEL_BLOCK)[None, :]
    d_w = WINDOW + a - b
    v, ok = lookup(d_w, (d_w >= 0) & (d_w < WINDOW))
    bias_win = jnp.where(ok, v, NEG)
    return bias_cmp, bias_near, bias_win


def _overlap_t(seq):
    ncp = seq // CMP_STRIDE
    cs = np.arange(ncp)[None, :] * CMP_STRIDE
    ss = np.arange(SEL_BLOCK)[:, None] * SEL_BLOCK
    ov = (cs < ss + SEL_BLOCK) & (cs + CMP_BLOCK - 1 >= ss) & (np.arange(ncp)[None, :] < (seq - CMP_BLOCK) // CMP_STRIDE + 1)
    return jnp.asarray(ov, BF16)


def _softmax_step(s, v, state):
    m, l, acc = state
    mn = jnp.maximum(m, jnp.max(s, axis=-1, keepdims=True))
    alpha = jnp.exp(m - mn)
    p = jnp.exp(s - mn)
    return mn, alpha * l + jnp.sum(p, axis=-1, keepdims=True), alpha * acc + _dot(p.astype(BF16), v)


def _nsa_kernel(q_ref, kv_ref, kc_ref, vc_ref, g_ref, bc_ref, bn_ref, bw_ref, ovt_ref, o_ref, *, pad):
    i = pl.program_id(2)
    qb, rep = SEL_BLOCK, NSA_REP
    rows = qb * rep
    q = q_ref[...]
    qq = jnp.concatenate([q[:, LANE * r:LANE * (r + 1)] for r in range(rep)], axis=0)

    ncp = kc_ref.shape[0]
    s = _dot_nt(qq, kc_ref[...]) + bc_ref[...].reshape(rows, ncp)
    m = jnp.maximum(jnp.max(s, axis=-1, keepdims=True), -1e20)
    e = jnp.exp(s - m)
    inv = 1.0 / jnp.maximum(jnp.sum(e, axis=-1, keepdims=True), 1e-30)
    o_c = _dot(e.astype(BF16), vc_ref[...]) * inv
    p = e * inv
    psum = p[0:qb] + p[qb:2 * qb] + p[2 * qb:3 * qb] + p[3 * qb:4 * qb]

    hi = psum.astype(BF16)
    r1 = psum - hi.astype(F32)
    mid = r1.astype(BF16)
    lo = (r1 - mid.astype(F32)).astype(BF16)
    ovt = ovt_ref[...]
    imp = _dot_nt(ovt, hi) + _dot_nt(ovt, mid) + _dot_nt(ovt, lo)
    jj = lax.broadcasted_iota(jnp.int32, (qb, qb), 0)
    imp = jnp.where((jj == i) | (jj == 0), FORCED_SCORE, imp)
    imp = jnp.where(jj > i, -1.0, imp)
    grp = [imp[8 * a:8 * a + 8] for a in range(8)]
    cnt = [jnp.zeros((8, qb), F32) for _ in range(8)]
    sub = lax.broadcasted_iota(jnp.int32, (8, qb), 0)
    for k in range(qb):
        rk = imp[k:k + 1, :]
        for a in range(8):
            gt = jnp.where(rk > grp[a], 1.0, 0.0)
            ge = jnp.where(rk >= grp[a], 1.0, 0.0)
            if 8 * a + 7 <= k:
                cnt[a] = cnt[a] + gt
            elif 8 * a > k:
                cnt[a] = cnt[a] + ge
            else:
                cnt[a] = cnt[a] + jnp.where(sub + 8 * a > k, ge, gt)
    neg = jnp.where(jnp.concatenate(cnt, axis=0) < SEL_TOP_N, 0.0, SEL_MASK)
    neg_far = jnp.where(jj >= i - 3, SEL_MASK, neg)
    eye = jnp.where(jj == lax.broadcasted_iota(jnp.int32, (qb, qb), 1), 1.0, 0.0).astype(BF16)
    zero = jnp.zeros((qb, qb), BF16)

    def aug(nt):
        a = _dot_nt(eye, jnp.concatenate([zero, nt.astype(BF16)], axis=0)).astype(BF16)
        return qq + jnp.concatenate([a] * rep, axis=0)

    q_near, q_far = aug(neg), aug(neg_far)

    def far_body(c, state):
        st = pl.multiple_of(pad + c * 256, 256)
        return _softmax_step(_dot_nt(q_far, kv_ref[pl.ds(st, 256), 0:LANE]), kv_ref[pl.ds(st, 256), LANE:2 * LANE], state)

    init = (jnp.full((rows, 1), NEG, F32), jnp.zeros((rows, 1), F32), jnp.zeros((rows, LANE), F32))
    state = lax.fori_loop(0, (jnp.maximum(i - 3, 0) + 3) // 4, far_body, init)
    st = pl.multiple_of(pad + (i - 3) * qb, qb)
    s = _dot_nt(q_near, kv_ref[pl.ds(st, 4 * qb), 0:LANE]) + bn_ref[...].reshape(rows, 4 * qb)
    lane = lax.broadcasted_iota(jnp.int32, (rows, 4 * qb), 1)
    s = jnp.where(lane + (i - 3) * qb >= 0, s, NEG)
    _, l, acc = _softmax_step(s, kv_ref[pl.ds(st, 4 * qb), LANE:2 * LANE], state)
    o_s = acc / l

    wk = WINDOW + qb
    st = pl.multiple_of(pad - WINDOW + i * qb, qb)
    s = _dot_nt(qq, kv_ref[pl.ds(st, wk), 2 * LANE:3 * LANE]) + bw_ref[...].reshape(rows, wk)
    lane = lax.broadcasted_iota(jnp.int32, (rows, wk), 1)
    s = jnp.where(lane + i * qb - WINDOW >= 0, s, NEG)
    m = jnp.max(s, axis=-1, keepdims=True)
    e = jnp.exp(s - m)
    o_w = _dot(e.astype(BF16), kv_ref[pl.ds(st, wk), 3 * LANE:4 * LANE]) / jnp.sum(e, axis=-1, keepdims=True)

    g = g_ref[...]
    outs = []
    for r in range(rep):
        sl = slice(qb * r, qb * (r + 1))
        outs.append(g[:, 3 * r:3 * r + 1] * o_c[sl] + g[:, 3 * r + 1:3 * r + 2] * o_s[sl]
                    + g[:, 3 * r + 2:3 * r + 3] * o_w[sl])
    low = lax.broadcasted_iota(jnp.int32, (qb, LANE), 1) < NSA_DK
    for pr in range(rep // 2):
        o_ref[:, LANE * pr:LANE * (pr + 1)] = jnp.where(low, outs[2 * pr], outs[2 * pr + 1]).astype(o_ref.dtype)


def _nsa_attention(qn, kv, kc, vc, gates, rel_bias, pad):
    bsz, seq, _ = qn.shape
    qb = SEL_BLOCK
    ncp = seq // CMP_STRIDE
    assert seq // qb <= qb and pad >= WINDOW
    bias_cmp, bias_near, bias_win = _bias_tables(rel_bias, seq)
    ovt = _overlap_t(seq)
    wk = WINDOW + qb
    return pl.pallas_call(
        functools.partial(_nsa_kernel, pad=pad), grid=(bsz, NSA_GROUPS, seq // qb),
        in_specs=[pl.BlockSpec((None, qb, 4 * LANE), lambda b, g, i: (b, i, g)),
                  pl.BlockSpec((None, seq + pad, 4 * LANE), lambda b, g, i: (b, 0, g)),
                  pl.BlockSpec((None, ncp, LANE), lambda b, g, i: (b, 0, g)),
                  pl.BlockSpec((None, ncp, LANE), lambda b, g, i: (b, 0, g)),
                  pl.BlockSpec((None, qb, LANE), lambda b, g, i: (b, i, g)),
                  pl.BlockSpec((NSA_REP, qb, ncp), lambda b, g, i: (g, i, 0)),
                  pl.BlockSpec((NSA_REP, qb, 4 * qb), lambda b, g, i: (g, 0, 0)),
                  pl.BlockSpec((NSA_REP, qb, wk), lambda b, g, i: (g, 0, 0)),
                  pl.BlockSpec((qb, ncp), lambda b, g, i: (0, 0))],
        out_specs=pl.BlockSpec((None, qb, 2 * LANE), lambda b, g, i: (b, i, g)),
        out_shape=jax.ShapeDtypeStruct((bsz, seq, NSA_HEADS * NSA_DK), BF16),
        compiler_params=_params("parallel", "parallel", "arbitrary"), name="nsa_attention",
    )(qn, kv, kc, vc, gates, bias_cmp, bias_near, bias_win, ovt)


def _mla_attn_kernel(q_ref, k_ref, v_ref, o_ref, *, tq):
    i = pl.program_id(2)
    outs = []
    for hh in range(2):
        sl = slice(LANE * hh, LANE * (hh + 1))
        q = q_ref[:, sl]

        def body(c, state, sl=sl, q=q):
            st = pl.multiple_of(c * tq, tq)
            return _softmax_step(_dot_nt(q, k_ref[pl.ds(st, tq), sl]), v_ref[pl.ds(st, tq), :], state)

        init = (jnp.full((tq, 1), NEG, F32), jnp.zeros((tq, 1), F32), jnp.zeros((tq, LANE), F32))
        state = lax.fori_loop(0, i, body, init)
        st = pl.multiple_of(i * tq, tq)
        s = _dot_nt(q, k_ref[pl.ds(st, tq), sl])
        causal = lax.broadcasted_iota(jnp.int32, (tq, tq), 0) >= lax.broadcasted_iota(jnp.int32, (tq, tq), 1)
        _, l, acc = _softmax_step(jnp.where(causal, s, NEG), v_ref[pl.ds(st, tq), :], state)
        outs.append(acc / l)
    low = lax.broadcasted_iota(jnp.int32, (tq, LANE), 1) < MLA_V
    o_ref[...] = jnp.where(low, outs[0], outs[1]).astype(o_ref.dtype)


def _mla_attention(qm, km, vm, tq=512):
    bsz, seq, _ = qm.shape
    tq = min(tq, seq)
    return pl.pallas_call(
        functools.partial(_mla_attn_kernel, tq=tq), grid=(bsz, MLA_HEADS // 2, seq // tq),
        in_specs=[pl.BlockSpec((None, tq, 2 * LANE), lambda b, h, i: (b, i, h)),
                  pl.BlockSpec((None, seq, 2 * LANE), lambda b, h, i: (b, 0, h)),
                  pl.BlockSpec((None, seq, LANE), lambda b, h, i: (b, 0, h))],
        out_specs=pl.BlockSpec((None, tq, LANE), lambda b, h, i: (b, i, h)),
        out_shape=jax.ShapeDtypeStruct((bsz, seq, MLA_HEADS * MLA_V), BF16),
        compiler_params=_params("parallel", "parallel", "arbitrary"), name="mla_attention",
    )(qm, km, vm)


def _even_out_kernel(x_ref, on_ref, om_ref, wn_ref, wm_ref, g_ref, wg_ref, wu_ref, wd_ref, o_ref, *, chunk):
    x1 = x_ref[...] + _dot(on_ref[...], wn_ref[...]) + _dot(om_ref[...], wm_ref[...])
    n = _rms(x1, g_ref[...], x1.shape[-1]).astype(BF16)
    ffn = None
    for f0 in range(0, wg_ref.shape[1], chunk):
        gate = _dot(n, wg_ref[:, f0:f0 + chunk])
        act = (gate * jax.nn.sigmoid(gate) * _dot(n, wu_ref[:, f0:f0 + chunk])).astype(BF16)
        part = _dot(act, wd_ref[f0:f0 + chunk, :])
        ffn = part if ffn is None else ffn + part
    o_ref[...] = x1 + ffn


def _even_out_ffn(x, o_nsa, o_mla, w_out, gain, w_gate, w_up, w_down, tm=512):
    t, d = x.shape
    dff = w_gate.shape[1]
    wn, wm = w_out[:o_nsa.shape[1]].astype(BF16), w_out[o_nsa.shape[1]:].astype(BF16)
    row = lambda width: pl.BlockSpec((tm, width), lambda i: (i, 0))
    return pl.pallas_call(
        functools.partial(_even_out_kernel, chunk=dff // 2), grid=(t // tm,),
        in_specs=[row(d), row(o_nsa.shape[1]), row(o_mla.shape[1]), _const_spec(wn.shape), _const_spec(wm.shape),
                  _const_spec((1, d)), _const_spec((d, dff)), _const_spec((d, dff)), _const_spec((dff, d))],
        out_specs=row(d), out_shape=jax.ShapeDtypeStruct((t, d), F32),
        compiler_params=_params("parallel"), name="even_out_ffn",
    )(x, o_nsa, o_mla, wn, wm, gain[None, :], w_gate.astype(BF16), w_up.astype(BF16), w_down.astype(BF16))


def _conv_kernel(x_ref, g_ref, win_ref, cw_ref, wout_ref, o_ref, vbuf_ref, *, tm):
    j = pl.program_id(1)
    x = x_ref[...]
    d = x.shape[-1]
    n = _rms(x, g_ref[...], d).astype(BF16)
    b_gate = _dot(n, win_ref[:, 0:d])
    v = _dot(n, win_ref[:, d:2 * d]) * _dot(n, win_ref[:, 2 * d:3 * d])

    @pl.when(j == 0)
    def _():
        vbuf_ref[0:8, :] = jnp.zeros((8, d), F32)

    vbuf_ref[8:8 + tm, :] = v
    cw = cw_ref[...]
    y = cw[2:3] * v + cw[1:2] * vbuf_ref[7:7 + tm, :] + cw[0:1] * vbuf_ref[6:6 + tm, :]
    vbuf_ref[0:8, :] = v[tm - 8:tm]
    o_ref[...] = x + _dot((b_gate * y).astype(BF16), wout_ref[...])


def _conv_mixer(x, gain, w_in, conv_w, w_out, tm=512):
    bsz, seq, d = x.shape
    cw = jnp.pad(conv_w.astype(F32), ((0, 8 - CONV_WIDTH), (0, 0)))
    tok = pl.BlockSpec((None, tm, d), lambda b, j: (b, j, 0))
    return pl.pallas_call(
        functools.partial(_conv_kernel, tm=tm), grid=(bsz, seq // tm),
        in_specs=[tok, _const_spec((1, d)), _const_spec((d, 3 * d)), _const_spec((8, d)), _const_spec((d, d))],
        out_specs=tok, out_shape=jax.ShapeDtypeStruct((bsz, seq, d), F32),
        scratch_shapes=[pltpu.VMEM((tm + 8, d), F32)],
        compiler_params=_params("parallel", "arbitrary"), name="conv_mixer",
    )(x, gain[None, :], w_in.astype(BF16), cw, w_out.astype(BF16))


def _router_kernel(x_ref, g_ref, wr_ref, br_ref, tri_ref, route_ref, cnt_ref, carry_ref, *, tm):
    t = pl.program_id(0)

    @pl.when(t == 0)
    def _():
        carry_ref[...] = jnp.zeros_like(carry_ref)

    n = _rms(x_ref[...], g_ref[...], x_ref.shape[-1])
    hi = n.astype(BF16)
    lo = (n - hi.astype(F32)).astype(BF16)
    whi, wlo = wr_ref[0], wr_ref[1]
    logits = _dot(hi, whi) + _dot(lo, whi) + _dot(hi, wlo) + br_ref[...]
    lane = lax.broadcasted_iota(jnp.int32, (tm, LANE), 1).astype(F32)
    big = float(LANE)
    m1 = jnp.max(logits, axis=-1, keepdims=True)
    e1 = jnp.min(jnp.where(logits == m1, lane, big), axis=-1, keepdims=True)
    rest = jnp.where(lane == e1, NEG, logits)
    m2 = jnp.max(rest, axis=-1, keepdims=True)
    e2 = jnp.min(jnp.where(rest == m2, lane, big), axis=-1, keepdims=True)
    z = jnp.exp(m2 - m1)
    w1 = 1.0 / (1.0 + z)
    w2 = z / (1.0 + z)
    oh1 = jnp.where(lane == e1, 1.0, 0.0)
    oh2 = jnp.where(lane == e2, 1.0, 0.0)
    both = oh1 + oh2
    before = _dot(tri_ref[...], both.astype(BF16)) + carry_ref[0:1, :]
    r1 = jnp.sum(oh1 * before, axis=-1, keepdims=True)
    r2 = jnp.sum(oh2 * before, axis=-1, keepdims=True)
    cols = [e1, e2, w1, w2, r1, r2]
    out = jnp.zeros((tm, LANE), F32)
    for c, val in enumerate(cols):
        out = jnp.where(lane == c, val, out)
    route_ref[...] = out
    carry_ref[0:1, :] = carry_ref[0:1, :] + jnp.sum(both, axis=0, keepdims=True)
    cnt_ref[...] = carry_ref[...]


def _moe_router(x, gain, w_router, b_router, tm=512):
    t, d = x.shape
    wr = jnp.pad(w_router.astype(F32), ((0, 0), (0, LANE - N_EXPERTS)))
    whi = wr.astype(BF16)
    wlo = (wr - whi.astype(F32)).astype(BF16)
    br = jnp.concatenate([b_router.astype(F32), jnp.full((LANE - N_EXPERTS,), NEG, F32)])[None, :]
    tri = jnp.asarray(np.tril(np.ones((tm, tm), np.float32), -1), BF16)
    route, cnt = pl.pallas_call(
        functools.partial(_router_kernel, tm=tm), grid=(t // tm,),
        in_specs=[pl.BlockSpec((tm, d), lambda i: (i, 0)), _const_spec((1, d)), _const_spec((2, d, LANE)),
                  _const_spec((1, LANE)), _const_spec((tm, tm))],
        out_specs=[pl.BlockSpec((tm, LANE), lambda i: (i, 0)), pl.BlockSpec((8, LANE), lambda i: (0, 0))],
        out_shape=[jax.ShapeDtypeStruct((t, LANE), F32), jax.ShapeDtypeStruct((8, LANE), F32)],
        scratch_shapes=[pltpu.VMEM((8, LANE), F32)],
        compiler_params=_params("arbitrary"), name="moe_router",
    )(x, gain[None, :], jnp.stack([whi, wlo]), br, tri)
    return route, cnt[0, :N_EXPERTS].astype(jnp.int32)


def _row_copy(src, i, dst, j, sem):
    return pltpu.make_async_copy(src.at[pl.ds(i, 1)], dst.at[pl.ds(j, 1)], sem)


def _scatter_kernel(fill_ref, dest_ref, x_ref, g_ref, xs_ref, xn_ref, sem, *, tm, rb):
    t = pl.program_id(0)
    xn_ref[0:tm, :] = _rms(x_ref[...], g_ref[...], x_ref.shape[-1])
    xn_ref[tm:tm + 8, :] = jnp.zeros((8, x_ref.shape[-1]), F32)

    def start(r, c):
        _row_copy(xn_ref, r, xs_ref, dest_ref[0, 0, r], sem).start()
        _row_copy(xn_ref, r, xs_ref, dest_ref[0, 0, tm + r], sem).start()
        return c

    def wait(r, c):
        _row_copy(xn_ref, 0, xs_ref, 0, sem).wait()
        return c

    lax.fori_loop(0, tm, start, 0)
    lax.fori_loop(0, 2 * tm, wait, 0)

    @pl.when(t == pl.num_programs(0) - 1)
    def _():
        for e in range(N_EXPERTS + 1):
            lo, hi = fill_ref[e], fill_ref[N_EXPERTS + 1 + e]
            lax.fori_loop(lo, hi, lambda r, c: (_row_copy(xn_ref, tm, xs_ref, r, sem).start(), c)[1], 0)
            lax.fori_loop(lo, hi, wait, 0)


def _moe_scatter(x, gain, dest, fill, n_rows, tm, rb):
    t, d = x.shape
    nt = t // tm
    grid_spec = pltpu.PrefetchScalarGridSpec(
        num_scalar_prefetch=1, grid=(nt,),
        in_specs=[pl.BlockSpec((1, 1, 2 * tm), lambda i, f: (i, 0, 0), memory_space=pltpu.SMEM),
                  pl.BlockSpec((tm, d), lambda i, f: (i, 0)),
                  pl.BlockSpec((1, d), lambda i, f: (0, 0))],
        out_specs=pl.BlockSpec(memory_space=pl.ANY),
        scratch_shapes=[pltpu.VMEM((tm + 8, d), F32), pltpu.SemaphoreType.DMA(())])
    return pl.pallas_call(
        functools.partial(_scatter_kernel, tm=tm, rb=rb), grid_spec=grid_spec,
        out_shape=jax.ShapeDtypeStruct((n_rows, d), F32),
        compiler_params=_params("arbitrary"), name="moe_scatter",
    )(fill, dest, x, gain[None, :])


def _expert_kernel(be_ref, nu_ref, x_ref, wg_ref, wu_ref, wd_ref, o_ref):
    used = pl.program_id(0) < nu_ref[0]

    @pl.when(used)
    def _():
        xb = x_ref[...].astype(BF16)
        gate = _dot(xb, wg_ref[...])
        act = (gate * jax.nn.sigmoid(gate) * _dot(xb, wu_ref[...])).astype(BF16)
        o_ref[...] = _dot(act, wd_ref[...])

    @pl.when(jnp.logical_not(used))
    def _():
        o_ref[...] = jnp.zeros_like(o_ref)


def _moe_experts(xs, blk_expert, n_used, w_gate, w_up, w_down, rb):
    n_rows, d = xs.shape
    dff = w_gate.shape[-1]
    rowblk = pl.BlockSpec((rb, d), lambda i, be, nu: (i, 0))
    wspec = lambda shape: pl.BlockSpec((None,) + shape, lambda i, be, nu: (be[i], 0, 0))
    grid_spec = pltpu.PrefetchScalarGridSpec(
        num_scalar_prefetch=2, grid=(n_rows // rb,),
        in_specs=[rowblk, wspec((d, dff)), wspec((d, dff)), wspec((dff, d))], out_specs=rowblk)
    return pl.pallas_call(
        _expert_kernel, grid_spec=grid_spec, out_shape=jax.ShapeDtypeStruct(xs.shape, F32),
        compiler_params=_params("arbitrary"), name="moe_experts",
    )(blk_expert, n_used, xs, w_gate.astype(BF16), w_up.astype(BF16), w_down.astype(BF16))


def _combine_kernel(dest_ref, x_ref, route_ref, ys_ref, o_ref, g1_ref, g2_ref, sem, *, tm):
    def start(r, c):
        _row_copy(ys_ref, dest_ref[0, 0, r], g1_ref, r, sem).start()
        _row_copy(ys_ref, dest_ref[0, 0, tm + r], g2_ref, r, sem).start()
        return c

    def wait(r, c):
        _row_copy(ys_ref, 0, g1_ref, 0, sem).wait()
        return c

    lax.fori_loop(0, tm, start, 0)
    lax.fori_loop(0, 2 * tm, wait, 0)
    route = route_ref[...]
    o_ref[...] = x_ref[...] + route[:, 2:3] * g1_ref[...] + route[:, 3:4] * g2_ref[...]


def _moe_combine(x, route, dest, ys, tm):
    t, d = x.shape
    return pl.pallas_call(
        functools.partial(_combine_kernel, tm=tm), grid=(t // tm,),
        in_specs=[pl.BlockSpec((1, 1, 2 * tm), lambda i: (i, 0, 0), memory_space=pltpu.SMEM),
                  pl.BlockSpec((tm, d), lambda i: (i, 0)), pl.BlockSpec((tm, LANE), lambda i: (i, 0)),
                  pl.BlockSpec(memory_space=pl.ANY)],
        out_specs=pl.BlockSpec((tm, d), lambda i: (i, 0)),
        out_shape=jax.ShapeDtypeStruct((t, d), F32),
        scratch_shapes=[pltpu.VMEM((tm, d), F32), pltpu.VMEM((tm, d), F32), pltpu.SemaphoreType.DMA(())],
        compiler_params=_params("arbitrary"), name="moe_combine",
    )(dest, x, route, ys)


def _moe(x, gain, w_router, b_router, w_gate, w_up, w_down, rb=512, tm=256):
    t, d = x.shape
    route, counts = _moe_router(x, gain, w_router, b_router)
    n_blk = (t * TOP_K + rb - 1) // rb + N_EXPERTS
    padded = (counts + rb - 1) // rb * rb
    pad_end = jnp.cumsum(padded)
    pad_start = pad_end - padded
    dest = pad_start[route[:, 0:2].astype(jnp.int32)] + route[:, 4:6].astype(jnp.int32)
    dest_t = jnp.concatenate([dest[:, 0].reshape(t // tm, 1, tm), dest[:, 1].reshape(t // tm, 1, tm)], axis=2)
    fill = jnp.concatenate([pad_start + counts, pad_end[-1:], pad_end, jnp.full((1,), n_blk * rb, jnp.int32)])
    n_used = pad_end[-1:] // rb
    blk = jnp.minimum(jnp.arange(n_blk, dtype=jnp.int32), n_used - 1) * rb
    blk_expert = jnp.sum(pad_end[None, :] <= blk[:, None], axis=1).astype(jnp.int32)
    xs = _moe_scatter(x, gain, dest_t, fill.astype(jnp.int32), n_blk * rb, tm, rb)
    ys = _moe_experts(xs, blk_expert, n_used.astype(jnp.int32), w_gate, w_up, w_down, rb)
    return _moe_combine(x, route, dest_t, ys, tm)


def kernel(x, rel_bias, ev_mix_norm, ev_w_in, nsa_q_norm, nsa_k_norm, nsa_cmp_pos, nsa_cmp_w1, nsa_cmp_w2, mla_cq_norm, mla_ckv_norm, mla_w_uq, mla_w_ukv, mla_q_norm, mla_k_norm, ev_w_out, ev_ffn_norm, ffn_w_gate, ffn_w_up, ffn_w_down, od_mix_norm, od_w_in, conv_w, od_w_out, od_ffn_norm, moe_w_router, moe_b_router, moe_w_gate, moe_w_up, moe_w_down):
    bsz, seq, d = x.shape
    depth = ev_mix_norm.shape[0] + od_mix_norm.shape[0]
    pad = 512
    for layer in range(depth):
        i = layer // 2
        if layer % 2 == 0:
            qn, kv, kc_raw, vc_raw, gates, qm, km, vm = _even_in_proj(
                x, ev_mix_norm[i], ev_w_in[i], nsa_q_norm[i], nsa_k_norm[i], mla_cq_norm[i], mla_ckv_norm[i],
                mla_w_uq[i], mla_w_ukv[i], mla_q_norm[i], mla_k_norm[i], tm=pad)
            kc = _compress(kc_raw, nsa_cmp_pos[i, 0], nsa_cmp_w1[i, 0], nsa_cmp_w2[i, 0], nsa_k_norm[i, 0], True)
            vc = _compress(vc_raw, nsa_cmp_pos[i, 1], nsa_cmp_w1[i, 1], nsa_cmp_w2[i, 1], nsa_k_norm[i, 0], False)
            o_nsa = _nsa_attention(qn, kv, kc, vc, gates, rel_bias, pad)
            o_mla = _mla_attention(qm, km, vm)
            x = _even_out_ffn(x.reshape(bsz * seq, d), o_nsa.reshape(bsz * seq, -1), o_mla.reshape(bsz * seq, -1),
                              ev_w_out[i], ev_ffn_norm[i], ffn_w_gate[i], ffn_w_up[i], ffn_w_down[i]).reshape(bsz, seq, d)
        else:
            x = _conv_mixer(x, od_mix_norm[i], od_w_in[i], conv_w[i], od_w_out[i])
            x = _moe(x.reshape(bsz * seq, d), od_ffn_norm[i], moe_w_router[i], moe_b_router[i],
                     moe_w_gate[i], moe_w_up[i], moe_w_down[i]).reshape(bsz, seq, d)
    return x
```

```python
import functools
import math

import jax
import jax.numpy as jnp
import numpy as np
from jax import lax
from jax.experimental import pallas as pl
from jax.experimental.pallas import tpu as pltpu

F32 = jnp.float32
BF16 = jnp.bfloat16

EPS = 1e-6
NEG = -1e30
FORCED_SCORE = 1e9
NSA_HEADS = 8
NSA_GROUPS = 2
NSA_REP = NSA_HEADS // NSA_GROUPS
NSA_DK = 64
CMP_BLOCK = 32
CMP_STRIDE = 16
CMP_HIDDEN = 256
SEL_BLOCK = 64
SEL_TOP_N = 16
WINDOW = 512
MLA_HEADS = 8
MLA_Q_RANK = 256
MLA_KV_RANK = 128
MLA_NOPE = 64
MLA_ROPE = 32
MLA_V = 64
ROPE_THETA = 10000.0
REL_BUCKETS = 32
REL_MAX_DIST = 128
CONV_WIDTH = 3
N_EXPERTS = 8
TOP_K = 2
EVEN_IN_SIZES = (512,) + (128,) * 6 + (24, 256, 128, 32)

LANE = 128
VMEM_LIMIT = 56 * 1024 * 1024
SEL_MASK = -30000.0
FAR_DIST = 128

_Q0, _KV0, _KC0, _VC0, _GT0, _CQ0, _CKV0, _KR0, _EVEN_W = 0, 1024, 2048, 2176, 2304, 2560, 2816, 2944, 3072


def _dot(a, b):
    return jnp.dot(a, b, preferred_element_type=F32)


def _dot_nt(a, b):
    return lax.dot_general(a, b, (((1,), (1,)), ((), ())), preferred_element_type=F32)


def _rms(x, gain, n):
    ss = jnp.sum(x * x, axis=-1, keepdims=True) * (1.0 / n)
    return x * lax.rsqrt(ss + EPS) * gain


def _params(*sem):
    return pltpu.CompilerParams(dimension_semantics=sem, vmem_limit_bytes=VMEM_LIMIT)


def _const_spec(shape):
    nd = len(shape)
    return pl.BlockSpec(shape, lambda *_: (0,) * nd, pipeline_mode=pl.Buffered(1))


def _rope(x, cos, sa, sb):
    return x * cos + pltpu.roll(x, 16, 1) * sa + pltpu.roll(x, LANE - 16, 1) * sb


def _even_in_kernel(x_ref, gmix_ref, w_ref, gq_ref, gks_ref, gkw_ref, gcq_ref, gckv_ref,
                    wuq_ref, wuk_ref, wuv_ref, gmq_ref, gmk_ref, cos_ref, sa_ref, sb_ref,
                    qn_ref, kv_ref, kc_ref, vc_ref, gate_ref, qm_ref, km_ref, vm_ref, *, tm):
    j = pl.program_id(1)

    @pl.when(j == 0)
    def _():
        kv_ref[...] = jnp.zeros_like(kv_ref)
        qn_ref[...] = jnp.zeros_like(qn_ref)
        kc_ref[...] = jnp.zeros_like(kc_ref)
        vc_ref[...] = jnp.zeros_like(vc_ref)
        gate_ref[...] = jnp.zeros_like(gate_ref)
        qm_ref[...] = jnp.zeros_like(qm_ref)
        km_ref[...] = jnp.zeros_like(km_ref)
        vm_ref[...] = jnp.zeros_like(vm_ref)

    @pl.when(j > 0)
    def _():
        xn = _rms(x_ref[...], gmix_ref[...], x_ref.shape[-1]).astype(BF16)
        hq = _dot(xn, w_ref[:, _Q0:_Q0 + 1024])
        gq = gq_ref[...]
        for h in range(NSA_HEADS):
            seg = hq[:, LANE * h:LANE * (h + 1)]
            qn_ref[:, LANE * h:LANE * (h + 1)] = (_rms(seg, gq, NSA_DK) * NSA_DK ** -0.5).astype(BF16)
        hkv = _dot(xn, w_ref[:, _KV0:_KV0 + 1024])
        pos = (j - 1) * tm + lax.broadcasted_iota(jnp.int32, (tm, LANE), 0)
        lane = lax.broadcasted_iota(jnp.int32, (tm, LANE), 1)
        onehot = jnp.where(lane - NSA_DK == pos // SEL_BLOCK, 1.0, 0.0)
        for g in range(NSA_GROUPS):
            o = 4 * LANE * g
            kv_ref[:, o:o + LANE] = (_rms(hkv[:, o:o + LANE], gks_ref[...], NSA_DK) + onehot).astype(BF16)
            kv_ref[:, o + LANE:o + 2 * LANE] = hkv[:, o + LANE:o + 2 * LANE].astype(BF16)
            kv_ref[:, o + 2 * LANE:o + 3 * LANE] = _rms(hkv[:, o + 2 * LANE:o + 3 * LANE], gkw_ref[...],
                                                        NSA_DK).astype(BF16)
            kv_ref[:, o + 3 * LANE:o + 4 * LANE] = hkv[:, o + 3 * LANE:o + 4 * LANE].astype(BF16)
        hc = _dot(xn, w_ref[:, _KC0:_KC0 + 256])
        kc_ref[...] = hc[:, :LANE].astype(BF16)
        vc_ref[...] = hc[:, LANE:].astype(BF16)
        gate_ref[...] = jax.nn.sigmoid(_dot(xn, w_ref[:, _GT0:_GT0 + 256]))
        hm = _dot(xn, w_ref[:, _CQ0:_EVEN_W])
        cq = _rms(hm[:, :MLA_Q_RANK], gcq_ref[...], MLA_Q_RANK).astype(BF16)
        ckv = _rms(hm[:, MLA_Q_RANK:MLA_Q_RANK + MLA_KV_RANK], gckv_ref[...], MLA_KV_RANK).astype(BF16)
        k_rope = hm[:, MLA_Q_RANK + MLA_KV_RANK:]
        qm = _dot(cq, wuq_ref[...])
        kn = _dot(ckv, wuk_ref[...])
        vm_ref[...] = _dot(ckv, wuv_ref[...]).astype(BF16)
        cos, sa, sb = cos_ref[...], sa_ref[...], sb_ref[...]
        dqk = MLA_NOPE + MLA_ROPE
        for h in range(MLA_HEADS):
            sl = slice(LANE * h, LANE * (h + 1))
            qh = _rope(_rms(qm[:, sl], gmq_ref[...], dqk), cos, sa, sb) * dqk ** -0.5
            qm_ref[:, sl] = qh.astype(BF16)
            kh = _rope(_rms(kn[:, sl] + k_rope, gmk_ref[...], dqk), cos, sa, sb)
            km_ref[:, sl] = kh.astype(BF16)


def _even_in_weights(w_in):
    offs = np.concatenate([[0], np.cumsum(EVEN_IN_SIZES)])
    part = [w_in[:, offs[n]:offs[n + 1]] for n in range(len(EVEN_IN_SIZES))]
    q, k_c, v_c, k_s, v_s, k_w, v_w, gates, c_q, c_kv, k_rope = part
    d = w_in.shape[0]
    z = lambda n: jnp.zeros((d, n), w_in.dtype)
    cols = []
    for h in range(NSA_HEADS):
        cols += [q[:, 64 * h:64 * h + 64], z(64)]
    for g in range(NSA_GROUPS):
        s = slice(64 * g, 64 * g + 64)
        cols += [k_s[:, s], z(64), v_s[:, s], v_s[:, s], k_w[:, s], z(64), v_w[:, s], v_w[:, s]]
    cols += [k_c, v_c]
    for g in range(NSA_GROUPS):
        cols += [gates[:, 12 * g:12 * g + 12], z(LANE - 12)]
    cols += [c_q, c_kv, z(64), k_rope, z(32)]
    w = jnp.concatenate(cols, axis=1)
    assert w.shape[1] == _EVEN_W
    return w.astype(BF16)


def _pad_gain(g, width):
    return jnp.pad(g.astype(F32), (0, width - g.shape[0]))[None, :]


def _rope_tables(seq):
    half = MLA_ROPE // 2
    inv_freq = ROPE_THETA ** (-jnp.arange(half, dtype=F32) / half)
    ang = jnp.arange(seq).astype(F32)[:, None] * inv_freq[None, :]
    cos, sin = jnp.cos(ang), jnp.sin(ang)
    one = jnp.ones((seq, MLA_NOPE), F32)
    zn = jnp.zeros((seq, MLA_NOPE), F32)
    zt = jnp.zeros((seq, LANE - MLA_NOPE - MLA_ROPE), F32)
    zh = jnp.zeros((seq, half), F32)
    cos_t = jnp.concatenate([one, cos, cos, zt + 1.0], axis=1)
    sa = jnp.concatenate([zn, zh, sin, zt], axis=1)
    sb = jnp.concatenate([zn, -sin, zh, zt], axis=1)
    return cos_t, sa, sb


def _even_in_proj(x, gmix, w_in, q_norm, k_norm, cq_norm, ckv_norm, w_uq, w_ukv, mq_norm, mk_norm, tm=512):
    bsz, seq, d = x.shape
    nt = seq // tm
    w = _even_in_weights(w_in)
    dqk = MLA_NOPE + MLA_ROPE
    wuq = jnp.concatenate(
        [jnp.pad(w_uq[:, dqk * h:dqk * (h + 1)], ((0, 0), (0, LANE - dqk))) for h in range(MLA_HEADS)],
        axis=1).astype(BF16)
    kvw = MLA_NOPE + MLA_V
    wuk = jnp.concatenate(
        [jnp.pad(w_ukv[:, kvw * h:kvw * h + MLA_NOPE], ((0, 0), (0, LANE - MLA_NOPE))) for h in range(MLA_HEADS)],
        axis=1).astype(BF16)
    wuv = jnp.concatenate([w_ukv[:, kvw * h + MLA_NOPE:kvw * (h + 1)] for h in range(MLA_HEADS)], axis=1).astype(BF16)
    cos_t, sa, sb = _rope_tables(seq)
    tok = lambda width: pl.BlockSpec((None, tm, width), lambda b, j: (b, jnp.maximum(j - 1, 0), 0))
    postab = pl.BlockSpec((tm, LANE), lambda b, j: (jnp.maximum(j - 1, 0), 0))
    in_specs = [tok(d), _const_spec((1, d)), _const_spec((d, _EVEN_W)),
                _const_spec((1, LANE)), _const_spec((1, LANE)), _const_spec((1, LANE)),
                _const_spec((1, MLA_Q_RANK)), _const_spec((1, MLA_KV_RANK)),
                _const_spec(wuq.shape), _const_spec(wuk.shape), _const_spec(wuv.shape),
                _const_spec((1, LANE)), _const_spec((1, LANE)), postab, postab, postab]
    out_shape = [jax.ShapeDtypeStruct((bsz, seq, 1024), BF16),
                 jax.ShapeDtypeStruct((bsz, seq + tm, 1024), BF16),
                 jax.ShapeDtypeStruct((bsz, seq, LANE), BF16),
                 jax.ShapeDtypeStruct((bsz, seq, LANE), BF16),
                 jax.ShapeDtypeStruct((bsz, seq, 2 * LANE), F32),
                 jax.ShapeDtypeStruct((bsz, seq, 1024), BF16),
                 jax.ShapeDtypeStruct((bsz, seq, 1024), BF16),
                 jax.ShapeDtypeStruct((bsz, seq, 512), BF16)]
    out_specs = [tok(1024), pl.BlockSpec((None, tm, 1024), lambda b, j: (b, j, 0)), tok(LANE), tok(LANE),
                 tok(2 * LANE), tok(1024), tok(1024), tok(512)]
    return pl.pallas_call(
        functools.partial(_even_in_kernel, tm=tm), grid=(bsz, nt + 1), in_specs=in_specs, out_specs=out_specs,
        out_shape=out_shape, compiler_params=_params("parallel", "arbitrary"), name="even_in_proj",
    )(x, gmix[None, :], w, _pad_gain(q_norm, LANE), _pad_gain(k_norm[1], LANE), _pad_gain(k_norm[2], LANE),
      cq_norm[None, :], ckv_norm[None, :], wuq, wuk, wuv, _pad_gain(mq_norm, LANE), _pad_gain(mk_norm, LANE),
      cos_t, sa, sb)


def _cmp_kernel(x_ref, wa_ref, wb_ref, pa_ref, pb_ref, w2_ref, gain_ref, o_ref, *, normalize):
    x = x_ref[...]
    ua = _dot(x, wa_ref[...])
    ub = _dot(x, wb_ref[...])
    pt = _dot(pa_ref[...], wa_ref[...]) + _dot(pb_ref[...], wb_ref[...])
    n = ub.shape[0]
    pre = ua + pltpu.roll(ub, n - 1, 0) + pt[0:1]
    hid = jax.nn.gelu(pre).astype(BF16)
    out = _dot(hid, w2_ref[...])
    if normalize:
        gain = gain_ref[...]
        for g in range(NSA_GROUPS):
            sl = slice(LANE * g, LANE * (g + 1))
            o_ref[:, sl] = _rms(out[:, sl], gain, NSA_DK).astype(BF16)
    else:
        o_ref[...] = out.astype(BF16)


def _compress(kv, pos_emb, w1, w2, gain, normalize):
    bsz, seq, _ = kv.shape
    half = CMP_BLOCK // 2
    assert CMP_STRIDE == half
    nrow = seq // CMP_STRIDE
    x = kv.reshape(bsz, nrow, CMP_STRIDE * LANE)
    w1r = w1.reshape(CMP_BLOCK, NSA_DK, CMP_HIDDEN)
    eye = jnp.eye(NSA_GROUPS, dtype=w1.dtype)
    widen = lambda w: jnp.einsum("ldn,gh->lgdhn", w, eye).reshape(half * LANE, NSA_GROUPS * CMP_HIDDEN).astype(BF16)
    wa, wb = widen(w1r[:half]), widen(w1r[half:])
    prow = lambda p: jnp.pad(jnp.broadcast_to(p[:, None, :], (half, NSA_GROUPS, NSA_DK)).reshape(1, half * LANE),
                             ((0, 15), (0, 0))).astype(BF16)
    pa, pb = prow(pos_emb[:half]), prow(pos_emb[half:])
    second = jnp.zeros_like(w2) if normalize else w2
    w2w = jnp.einsum("nd,gh->gnhd", jnp.concatenate([w2, second], axis=1), eye)
    w2w = w2w.reshape(NSA_GROUPS * CMP_HIDDEN, NSA_GROUPS * LANE).astype(BF16)
    return pl.pallas_call(
        functools.partial(_cmp_kernel, normalize=normalize), grid=(bsz,),
        in_specs=[pl.BlockSpec((None, nrow, CMP_STRIDE * LANE), lambda b: (b, 0, 0)),
                  _const_spec(wa.shape), _const_spec(wb.shape), _const_spec(pa.shape), _const_spec(pb.shape),
                  _const_spec(w2w.shape), _const_spec((1, LANE))],
        out_specs=pl.BlockSpec((None, nrow, NSA_GROUPS * LANE), lambda b: (b, 0, 0)),
        out_shape=jax.ShapeDtypeStruct((bsz, nrow, NSA_GROUPS * LANE), BF16),
        compiler_params=_params("parallel"), name="nsa_compress",
    )(x, wa, wb, pa, pb, w2w, _pad_gain(gain, LANE))


def _bucket_table():
    dist = np.arange(FAR_DIST + 1)
    max_exact = REL_BUCKETS // 2
    nf = np.maximum(dist, max_exact).astype(np.float32)
    large = max_exact + (np.log(nf / max_exact) / math.log(REL_MAX_DIST / max_exact)
                         * (REL_BUCKETS - max_exact)).astype(np.int32)
    return np.where(dist < max_exact, dist, np.minimum(large, REL_BUCKETS - 1))


CMP_NEAR = 32


def _expand(tbl, idx):
    idx = np.asarray(idx)
    onehot = jnp.asarray(np.eye(tbl.shape[1], dtype=np.float32)[idx.reshape(-1)])
    out = lax.dot_general(tbl, onehot, (((1,), (1,)), ((), ())), precision=lax.Precision.HIGHEST)
    return out.reshape((tbl.shape[0],) + idx.shape)


def _toeplitz_tile(tbl, far, width, dist0, valid):
    period = width + SEL_BLOCK + 1
    j = np.arange(period)
    dist = dist0 - np.where(j < width, j, j - period)
    gen = jnp.where(jnp.asarray(valid(dist))[None], _expand(tbl, np.clip(dist, 0, FAR_DIST)) - far, NEG)
    flat = jnp.tile(gen, (1, SEL_BLOCK))[:, :SEL_BLOCK * (period - 1)]
    return flat.reshape(tbl.shape[0], SEL_BLOCK, period - 1)[:, :, :width]


def _bias_tables(rel_bias):
    tbl = _expand(rel_bias.astype(F32).T, _bucket_table())
    far = tbl[:, FAR_DIST:]
    bias_near = _toeplitz_tile(tbl, far, 4 * SEL_BLOCK, 3 * SEL_BLOCK, lambda d: d >= 0)
    bias_win = _toeplitz_tile(tbl, 0.0, WINDOW + SEL_BLOCK, WINDOW, lambda d: (d >= 0) & (d < WINDOW))
    d_c = (np.arange(SEL_BLOCK)[:, None] + (CMP_NEAR // 2) * CMP_STRIDE - (CMP_BLOCK - 1)
           - CMP_STRIDE * np.arange(CMP_NEAR)[None, :])
    assert d_c[:, 0].min() >= FAR_DIST and d_c[:, -1].max() < 0
    bias_cmp = jnp.where(jnp.asarray(d_c >= 0)[None], _expand(tbl, np.clip(d_c, 0, FAR_DIST)) - far[:, :, None], 0.0)
    return bias_cmp, bias_near, bias_win


def _cmp_mask_dist(seq):
    a = np.tile(np.arange(SEL_BLOCK), NSA_REP)[:, None]
    c = np.arange(seq // CMP_STRIDE)[None, :]
    return jnp.asarray(c * CMP_STRIDE + CMP_BLOCK - 1 - a, jnp.int32)


def _overlap_t(seq):
    ncp = seq // CMP_STRIDE
    cs = np.arange(ncp)[None, :] * CMP_STRIDE
    ss = np.arange(SEL_BLOCK)[:, None] * SEL_BLOCK
    ov = (cs < ss + SEL_BLOCK) & (cs + CMP_BLOCK - 1 >= ss) & (np.arange(ncp)[None, :] < (seq - CMP_BLOCK) // CMP_STRIDE + 1)
    return jnp.asarray(ov, BF16)


def _softmax_step(s, v, state):
    m, l, acc = state
    mn = jnp.maximum(m, jnp.max(s, axis=-1, keepdims=True))
    alpha = jnp.exp(m - mn)
    p = jnp.exp(s - mn)
    return mn, alpha * l + jnp.sum(p, axis=-1, keepdims=True), alpha * acc + _dot(p.astype(BF16), v)


def _nsa_kernel(q_ref, kv_ref, kc_ref, vc_ref, g_ref, bc_ref, bn_ref, bw_ref, ovt_ref, dmask_ref, o_ref, *, pad):
    i = pl.program_id(2)
    qb, rep = SEL_BLOCK, NSA_REP
    rows = qb * rep
    q = q_ref[...]
    qq = jnp.concatenate([q[:, LANE * r:LANE * (r + 1)] for r in range(rep)], axis=0)

    ncp = kc_ref.shape[0]
    m_i = lax.broadcasted_iota(jnp.int32, (CMP_NEAR, ncp), 0)
    c_i = lax.broadcasted_iota(jnp.int32, (CMP_NEAR, ncp), 1)
    shift = jnp.where(c_i - m_i == (qb // CMP_STRIDE) * i - CMP_NEAR // 2, 1.0, 0.0).astype(BF16)
    base = bc_ref[...].reshape(rows, CMP_NEAR)
    b_hi = base.astype(BF16)
    b_lo = (base - b_hi.astype(F32)).astype(BF16)
    s = _dot_nt(qq, kc_ref[...]) + _dot(b_hi, shift) + _dot(b_lo, shift)
    s = jnp.where(dmask_ref[...] <= i * qb, s, NEG)
    m = jnp.maximum(jnp.max(s, axis=-1, keepdims=True), -1e20)
    e = jnp.exp(s - m)
    inv = 1.0 / jnp.maximum(jnp.sum(e, axis=-1, keepdims=True), 1e-30)
    o_c = _dot(e.astype(BF16), vc_ref[...]) * inv
    p = e * inv
    psum = p[0:qb] + p[qb:2 * qb] + p[2 * qb:3 * qb] + p[3 * qb:4 * qb]

    hi = psum.astype(BF16)
    r1 = psum - hi.astype(F32)
    mid = r1.astype(BF16)
    lo = (r1 - mid.astype(F32)).astype(BF16)
    ovt = ovt_ref[...]
    imp = _dot_nt(ovt, hi) + _dot_nt(ovt, mid) + _dot_nt(ovt, lo)
    jj = lax.broadcasted_iota(jnp.int32, (qb, qb), 0)
    imp = jnp.where((jj == i) | (jj == 0), FORCED_SCORE, imp)
    imp = jnp.where(jj > i, -1.0, imp)
    grp = [imp[8 * a:8 * a + 8] for a in range(8)]
    cnt = [jnp.zeros((8, qb), F32) for _ in range(8)]
    sub = lax.broadcasted_iota(jnp.int32, (8, qb), 0)
    for k in range(qb):
        rk = imp[k:k + 1, :]
        for a in range(8):
            gt = jnp.where(rk > grp[a], 1.0, 0.0)
            ge = jnp.where(rk >= grp[a], 1.0, 0.0)
            if 8 * a + 7 <= k:
                cnt[a] = cnt[a] + gt
            elif 8 * a > k:
                cnt[a] = cnt[a] + ge
            else:
                cnt[a] = cnt[a] + jnp.where(sub + 8 * a > k, ge, gt)
    neg = jnp.where(jnp.concatenate(cnt, axis=0) < SEL_TOP_N, 0.0, SEL_MASK)
    neg_far = jnp.where(jj >= i - 3, SEL_MASK, neg)
    eye = jnp.where(jj == lax.broadcasted_iota(jnp.int32, (qb, qb), 1), 1.0, 0.0).astype(BF16)
    zero = jnp.zeros((qb, qb), BF16)

    def aug(nt):
        a = _dot_nt(eye, jnp.concatenate([zero, nt.astype(BF16)], axis=0)).astype(BF16)
        return qq + jnp.concatenate([a] * rep, axis=0)

    q_near, q_far = aug(neg), aug(neg_far)

    def far_body(c, state):
        st = pl.multiple_of(pad + c * 256, 256)
        return _softmax_step(_dot_nt(q_far, kv_ref[pl.ds(st, 256), 0:LANE]), kv_ref[pl.ds(st, 256), LANE:2 * LANE], state)

    init = (jnp.full((rows, 1), NEG, F32), jnp.zeros((rows, 1), F32), jnp.zeros((rows, LANE), F32))
    state = lax.fori_loop(0, (jnp.maximum(i - 3, 0) + 3) // 4, far_body, init)
    st = pl.multiple_of(pad + (i - 3) * qb, qb)
    s = _dot_nt(q_near, kv_ref[pl.ds(st, 4 * qb), 0:LANE]) + bn_ref[...].reshape(rows, 4 * qb)
    lane = lax.broadcasted_iota(jnp.int32, (rows, 4 * qb), 1)
    s = jnp.where(lane + (i - 3) * qb >= 0, s, NEG)
    _, l, acc = _softmax_step(s, kv_ref[pl.ds(st, 4 * qb), LANE:2 * LANE], state)
    o_s = acc / l

    wk = WINDOW + qb
    st = pl.multiple_of(pad - WINDOW + i * qb, qb)
    s = _dot_nt(qq, kv_ref[pl.ds(st, wk), 2 * LANE:3 * LANE]) + bw_ref[...].reshape(rows, wk)
    lane = lax.broadcasted_iota(jnp.int32, (rows, wk), 1)
    s = jnp.where(lane + i * qb - WINDOW >= 0, s, NEG)
    m = jnp.max(s, axis=-1, keepdims=True)
    e = jnp.exp(s - m)
    o_w = _dot(e.astype(BF16), kv_ref[pl.ds(st, wk), 3 * LANE:4 * LANE]) / jnp.sum(e, axis=-1, keepdims=True)

    g = g_ref[...]
    outs = []
    for r in range(rep):
        sl = slice(qb * r, qb * (r + 1))
        outs.append(g[:, 3 * r:3 * r + 1] * o_c[sl] + g[:, 3 * r + 1:3 * r + 2] * o_s[sl]
                    + g[:, 3 * r + 2:3 * r + 3] * o_w[sl])
    low = lax.broadcasted_iota(jnp.int32, (qb, LANE), 1) < NSA_DK
    for pr in range(rep // 2):
        o_ref[:, LANE * pr:LANE * (pr + 1)] = jnp.where(low, outs[2 * pr], outs[2 * pr + 1]).astype(o_ref.dtype)


def _nsa_attention(qn, kv, kc, vc, gates, rel_bias, pad):
    bsz, seq, _ = qn.shape
    qb = SEL_BLOCK
    ncp = seq // CMP_STRIDE
    assert seq // qb <= qb and pad >= WINDOW
    bias_cmp, bias_near, bias_win = _bias_tables(rel_bias)
    ovt = _overlap_t(seq)
    dmask = _cmp_mask_dist(seq)
    wk = WINDOW + qb
    return pl.pallas_call(
        functools.partial(_nsa_kernel, pad=pad), grid=(bsz, NSA_GROUPS, seq // qb),
        in_specs=[pl.BlockSpec((None, qb, 4 * LANE), lambda b, g, i: (b, i, g)),
                  pl.BlockSpec((None, seq + pad, 4 * LANE), lambda b, g, i: (b, 0, g)),
                  pl.BlockSpec((None, ncp, LANE), lambda b, g, i: (b, 0, g)),
                  pl.BlockSpec((None, ncp, LANE), lambda b, g, i: (b, 0, g)),
                  pl.BlockSpec((None, qb, LANE), lambda b, g, i: (b, i, g)),
                  pl.BlockSpec((NSA_REP, qb, CMP_NEAR), lambda b, g, i: (g, 0, 0)),
                  pl.BlockSpec((NSA_REP, qb, 4 * qb), lambda b, g, i: (g, 0, 0)),
                  pl.BlockSpec((NSA_REP, qb, wk), lambda b, g, i: (g, 0, 0)),
                  pl.BlockSpec((qb, ncp), lambda b, g, i: (0, 0)),
                  pl.BlockSpec((NSA_REP * qb, ncp), lambda b, g, i: (0, 0))],
        out_specs=pl.BlockSpec((None, qb, 2 * LANE), lambda b, g, i: (b, i, g)),
        out_shape=jax.ShapeDtypeStruct((bsz, seq, NSA_HEADS * NSA_DK), BF16),
        compiler_params=_params("parallel", "parallel", "arbitrary"), name="nsa_attention",
    )(qn, kv, kc, vc, gates, bias_cmp, bias_near, bias_win, ovt, dmask)


def _mla_attn_kernel(q_ref, k_ref, v_ref, o_ref, *, tq):
    i = pl.program_id(2)
    heads = [slice(LANE * hh, LANE * (hh + 1)) for hh in range(2)]
    qs = [q_ref[:, sl] for sl in heads]

    def body(c, states):
        st = pl.multiple_of(c * tq, tq)
        v = v_ref[pl.ds(st, tq), :]
        return tuple(_softmax_step(_dot_nt(q, k_ref[pl.ds(st, tq), sl]), v, state)
                     for q, sl, state in zip(qs, heads, states))

    init = (jnp.full((tq, 1), NEG, F32), jnp.zeros((tq, 1), F32), jnp.zeros((tq, LANE), F32))
    states = lax.fori_loop(0, i, body, (init, init))
    st = pl.multiple_of(i * tq, tq)
    causal = lax.broadcasted_iota(jnp.int32, (tq, tq), 0) >= lax.broadcasted_iota(jnp.int32, (tq, tq), 1)
    outs = []
    for q, sl, state in zip(qs, heads, states):
        s = _dot_nt(q, k_ref[pl.ds(st, tq), sl])
        _, l, acc = _softmax_step(jnp.where(causal, s, NEG), v_ref[pl.ds(st, tq), :], state)
        outs.append(acc / l)
    low = lax.broadcasted_iota(jnp.int32, (tq, LANE), 1) < MLA_V
    o_ref[...] = jnp.where(low, outs[0], outs[1]).astype(o_ref.dtype)


def _mla_attention(qm, km, vm, tq=512):
    bsz, seq, _ = qm.shape
    tq = min(tq, seq)
    return pl.pallas_call(
        functools.partial(_mla_attn_kernel, tq=tq), grid=(bsz, MLA_HEADS // 2, seq // tq),
        in_specs=[pl.BlockSpec((None, tq, 2 * LANE), lambda b, h, i: (b, i, h)),
                  pl.BlockSpec((None, seq, 2 * LANE), lambda b, h, i: (b, 0, h)),
                  pl.BlockSpec((None, seq, LANE), lambda b, h, i: (b, 0, h))],
        out_specs=pl.BlockSpec((None, tq, LANE), lambda b, h, i: (b, i, h)),
        out_shape=jax.ShapeDtypeStruct((bsz, seq, MLA_HEADS * MLA_V), BF16),
        compiler_params=_params("parallel", "parallel", "arbitrary"), name="mla_attention",
    )(qm, km, vm)


def _even_out_kernel(x_ref, on_ref, om_ref, wn_ref, wm_ref, g_ref, wg_ref, wu_ref, wd_ref, o_ref, *, chunk):
    x1 = x_ref[...] + _dot(on_ref[...], wn_ref[...]) + _dot(om_ref[...], wm_ref[...])
    n = _rms(x1, g_ref[...], x1.shape[-1]).astype(BF16)
    ffn = None
    for f0 in range(0, wg_ref.shape[1], chunk):
        gate = _dot(n, wg_ref[:, f0:f0 + chunk])
        act = (gate * jax.nn.sigmoid(gate) * _dot(n, wu_ref[:, f0:f0 + chunk])).astype(BF16)
        part = _dot(act, wd_ref[f0:f0 + chunk, :])
        ffn = part if ffn is None else ffn + part
    o_ref[...] = x1 + ffn


def _even_out_ffn(x, o_nsa, o_mla, w_out, gain, w_gate, w_up, w_down, tm=512):
    t, d = x.shape
    dff = w_gate.shape[1]
    wn, wm = w_out[:o_nsa.shape[1]].astype(BF16), w_out[o_nsa.shape[1]:].astype(BF16)
    row = lambda width: pl.BlockSpec((tm, width), lambda i: (i, 0))
    return pl.pallas_call(
        functools.partial(_even_out_kernel, chunk=dff // 2), grid=(t // tm,),
        in_specs=[row(d), row(o_nsa.shape[1]), row(o_mla.shape[1]), _const_spec(wn.shape), _const_spec(wm.shape),
                  _const_spec((1, d)), _const_spec((d, dff)), _const_spec((d, dff)), _const_spec((dff, d))],
        out_specs=row(d), out_shape=jax.ShapeDtypeStruct((t, d), F32),
        compiler_params=_params("parallel"), name="even_out_ffn",
    )(x, o_nsa, o_mla, wn, wm, gain[None, :], w_gate.astype(BF16), w_up.astype(BF16), w_down.astype(BF16))


def _conv_kernel(x_ref, g_ref, win_ref, cw_ref, wout_ref, o_ref, vbuf_ref, *, tm):
    j = pl.program_id(1)
    x = x_ref[...]
    d = x.shape[-1]
    n = _rms(x, g_ref[...], d).astype(BF16)
    b_gate = _dot(n, win_ref[:, 0:d])
    v = _dot(n, win_ref[:, d:2 * d]) * _dot(n, win_ref[:, 2 * d:3 * d])

    @pl.when(j == 0)
    def _():
        vbuf_ref[0:8, :] = jnp.zeros((8, d), F32)

    vbuf_ref[8:8 + tm, :] = v
    cw = cw_ref[...]
    y = cw[2:3] * v + cw[1:2] * vbuf_ref[7:7 + tm, :] + cw[0:1] * vbuf_ref[6:6 + tm, :]
    vbuf_ref[0:8, :] = v[tm - 8:tm]
    o_ref[...] = x + _dot((b_gate * y).astype(BF16), wout_ref[...])


def _conv_mixer(x, gain, w_in, conv_w, w_out, tm=512):
    bsz, seq, d = x.shape
    cw = jnp.pad(conv_w.astype(F32), ((0, 8 - CONV_WIDTH), (0, 0)))
    tok = pl.BlockSpec((None, tm, d), lambda b, j: (b, j, 0))
    return pl.pallas_call(
        functools.partial(_conv_kernel, tm=tm), grid=(bsz, seq // tm),
        in_specs=[tok, _const_spec((1, d)), _const_spec((d, 3 * d)), _const_spec((8, d)), _const_spec((d, d))],
        out_specs=tok, out_shape=jax.ShapeDtypeStruct((bsz, seq, d), F32),
        scratch_shapes=[pltpu.VMEM((tm + 8, d), F32)],
        compiler_params=_params("parallel", "arbitrary"), name="conv_mixer",
    )(x, gain[None, :], w_in.astype(BF16), cw, w_out.astype(BF16))


def _router_kernel(x_ref, g_ref, wr_ref, br_ref, tri_ref, route_ref, cnt_ref, carry_ref, *, tm):
    t = pl.program_id(0)

    @pl.when(t == 0)
    def _():
        carry_ref[...] = jnp.zeros_like(carry_ref)

    n = _rms(x_ref[...], g_ref[...], x_ref.shape[-1])
    hi = n.astype(BF16)
    lo = (n - hi.astype(F32)).astype(BF16)
    whi, wlo = wr_ref[0], wr_ref[1]
    logits = _dot(hi, whi) + _dot(lo, whi) + _dot(hi, wlo) + br_ref[...]
    lane = lax.broadcasted_iota(jnp.int32, (tm, LANE), 1).astype(F32)
    big = float(LANE)
    m1 = jnp.max(logits, axis=-1, keepdims=True)
    e1 = jnp.min(jnp.where(logits == m1, lane, big), axis=-1, keepdims=True)
    rest = jnp.where(lane == e1, NEG, logits)
    m2 = jnp.max(rest, axis=-1, keepdims=True)
    e2 = jnp.min(jnp.where(rest == m2, lane, big), axis=-1, keepdims=True)
    z = jnp.exp(m2 - m1)
    w1 = 1.0 / (1.0 + z)
    w2 = z / (1.0 + z)
    oh1 = jnp.where(lane == e1, 1.0, 0.0)
    oh2 = jnp.where(lane == e2, 1.0, 0.0)
    both = oh1 + oh2
    before = _dot(tri_ref[...], both.astype(BF16)) + carry_ref[0:1, :]
    r1 = jnp.sum(oh1 * before, axis=-1, keepdims=True)
    r2 = jnp.sum(oh2 * before, axis=-1, keepdims=True)
    cols = [e1, e2, w1, w2, r1, r2]
    out = jnp.zeros((tm, LANE), F32)
    for c, val in enumerate(cols):
        out = jnp.where(lane == c, val, out)
    route_ref[...] = out
    carry_ref[0:1, :] = carry_ref[0:1, :] + jnp.sum(both, axis=0, keepdims=True)
    cnt_ref[...] = carry_ref[...]


def _moe_router(x, gain, w_router, b_router, tm=512):
    t, d = x.shape
    wr = jnp.pad(w_router.astype(F32), ((0, 0), (0, LANE - N_EXPERTS)))
    whi = wr.astype(BF16)
    wlo = (wr - whi.astype(F32)).astype(BF16)
    br = jnp.concatenate([b_router.astype(F32), jnp.full((LANE - N_EXPERTS,), NEG, F32)])[None, :]
    tri = jnp.asarray(np.tril(np.ones((tm, tm), np.float32), -1), BF16)
    route, cnt = pl.pallas_call(
        functools.partial(_router_kernel, tm=tm), grid=(t // tm,),
        in_specs=[pl.BlockSpec((tm, d), lambda i: (i, 0)), _const_spec((1, d)), _const_spec((2, d, LANE)),
                  _const_spec((1, LANE)), _const_spec((tm, tm))],
        out_specs=[pl.BlockSpec((tm, LANE), lambda i: (i, 0)), pl.BlockSpec((8, LANE), lambda i: (0, 0))],
        out_shape=[jax.ShapeDtypeStruct((t, LANE), F32), jax.ShapeDtypeStruct((8, LANE), F32)],
        scratch_shapes=[pltpu.VMEM((8, LANE), F32)],
        compiler_params=_params("arbitrary"), name="moe_router",
    )(x, gain[None, :], jnp.stack([whi, wlo]), br, tri)
    return route, cnt[0, :N_EXPERTS].astype(jnp.int32)


def _row_copy(src, i, dst, j, sem):
    return pltpu.make_async_copy(src.at[pl.ds(i, 1)], dst.at[pl.ds(j, 1)], sem)


def _scatter_kernel(fill_ref, dest_ref, x_ref, g_ref, xs_ref, xn_ref, sem, *, tm, rb):
    t = pl.program_id(0)
    xn_ref[0:tm, :] = _rms(x_ref[...], g_ref[...], x_ref.shape[-1])
    xn_ref[tm:tm + 8, :] = jnp.zeros((8, x_ref.shape[-1]), F32)

    def start(r, c):
        _row_copy(xn_ref, r, xs_ref, dest_ref[0, 0, r], sem).start()
        _row_copy(xn_ref, r, xs_ref, dest_ref[0, 0, tm + r], sem).start()
        return c

    def wait(r, c):
        _row_copy(xn_ref, 0, xs_ref, 0, sem).wait()
        return c

    lax.fori_loop(0, tm, start, 0)
    lax.fori_loop(0, 2 * tm, wait, 0)

    @pl.when(t == pl.num_programs(0) - 1)
    def _():
        for e in range(N_EXPERTS + 1):
            lo, hi = fill_ref[e], fill_ref[N_EXPERTS + 1 + e]
            lax.fori_loop(lo, hi, lambda r, c: (_row_copy(xn_ref, tm, xs_ref, r, sem).start(), c)[1], 0)
            lax.fori_loop(lo, hi, wait, 0)


def _moe_scatter(x, gain, dest, fill, n_rows, tm, rb):
    t, d = x.shape
    nt = t // tm
    grid_spec = pltpu.PrefetchScalarGridSpec(
        num_scalar_prefetch=1, grid=(nt,),
        in_specs=[pl.BlockSpec((1, 1, 2 * tm), lambda i, f: (i, 0, 0), memory_space=pltpu.SMEM),
                  pl.BlockSpec((tm, d), lambda i, f: (i, 0)),
                  pl.BlockSpec((1, d), lambda i, f: (0, 0))],
        out_specs=pl.BlockSpec(memory_space=pl.ANY),
        scratch_shapes=[pltpu.VMEM((tm + 8, d), F32), pltpu.SemaphoreType.DMA(())])
    return pl.pallas_call(
        functools.partial(_scatter_kernel, tm=tm, rb=rb), grid_spec=grid_spec,
        out_shape=jax.ShapeDtypeStruct((n_rows, d), F32),
        compiler_params=_params("arbitrary"), name="moe_scatter",
    )(fill, dest, x, gain[None, :])


def _expert_kernel(be_ref, nu_ref, x_ref, wg_ref, wu_ref, wd_ref, o_ref):
    used = pl.program_id(0) < nu_ref[0]

    @pl.when(used)
    def _():
        xb = x_ref[...].astype(BF16)
        gate = _dot(xb, wg_ref[...])
        act = (gate * jax.nn.sigmoid(gate) * _dot(xb, wu_ref[...])).astype(BF16)
        o_ref[...] = _dot(act, wd_ref[...])

    @pl.when(jnp.logical_not(used))
    def _():
        o_ref[...] = jnp.zeros_like(o_ref)


def _moe_experts(xs, blk_expert, n_used, w_gate, w_up, w_down, rb):
    n_rows, d = xs.shape
    dff = w_gate.shape[-1]
    rowblk = pl.BlockSpec((rb, d), lambda i, be, nu: (i, 0))
    wspec = lambda shape: pl.BlockSpec((None,) + shape, lambda i, be, nu: (be[i], 0, 0))
    grid_spec = pltpu.PrefetchScalarGridSpec(
        num_scalar_prefetch=2, grid=(n_rows // rb,),
        in_specs=[rowblk, wspec((d, dff)), wspec((d, dff)), wspec((dff, d))], out_specs=rowblk)
    return pl.pallas_call(
        _expert_kernel, grid_spec=grid_spec, out_shape=jax.ShapeDtypeStruct(xs.shape, F32),
        compiler_params=_params("arbitrary"), name="moe_experts",
    )(blk_expert, n_used, xs, w_gate.astype(BF16), w_up.astype(BF16), w_down.astype(BF16))


def _combine_kernel(dest_ref, x_ref, route_ref, ys_ref, o_ref, g1_ref, g2_ref, sem, *, tm):
    def start(r, c):
        _row_copy(ys_ref, dest_ref[0, 0, r], g1_ref, r, sem).start()
        _row_copy(ys_ref, dest_ref[0, 0, tm + r], g2_ref, r, sem).start()
        return c

    def wait(r, c):
        _row_copy(ys_ref, 0, g1_ref, 0, sem).wait()
        return c

    lax.fori_loop(0, tm, start, 0)
    lax.fori_loop(0, 2 * tm, wait, 0)
    route = route_ref[...]
    o_ref[...] = x_ref[...] + route[:, 2:3] * g1_ref[...] + route[:, 3:4] * g2_ref[...]


def _moe_combine(x, route, dest, ys, tm):
    t, d = x.shape
    return pl.pallas_call(
        functools.partial(_combine_kernel, tm=tm), grid=(t // tm,),
        in_specs=[pl.BlockSpec((1, 1, 2 * tm), lambda i: (i, 0, 0), memory_space=pltpu.SMEM),
                  pl.BlockSpec((tm, d), lambda i: (i, 0)), pl.BlockSpec((tm, LANE), lambda i: (i, 0)),
                  pl.BlockSpec(memory_space=pl.ANY)],
        out_specs=pl.BlockSpec((tm, d), lambda i: (i, 0)),
        out_shape=jax.ShapeDtypeStruct((t, d), F32),
        scratch_shapes=[pltpu.VMEM((tm, d), F32), pltpu.VMEM((tm, d), F32), pltpu.SemaphoreType.DMA(())],
        compiler_params=_params("arbitrary"), name="moe_combine",
    )(dest, x, route, ys)


def _moe(x, gain, w_router, b_router, w_gate, w_up, w_down, rb=512, tm=256):
    t, d = x.shape
    route, counts = _moe_router(x, gain, w_router, b_router)
    n_blk = (t * TOP_K + rb - 1) // rb + N_EXPERTS
    padded = (counts + rb - 1) // rb * rb
    pad_end = jnp.cumsum(padded)
    pad_start = pad_end - padded
    dest = pad_start[route[:, 0:2].astype(jnp.int32)] + route[:, 4:6].astype(jnp.int32)
    dest_t = jnp.concatenate([dest[:, 0].reshape(t // tm, 1, tm), dest[:, 1].reshape(t // tm, 1, tm)], axis=2)
    fill = jnp.concatenate([pad_start + counts, pad_end[-1:], pad_end, jnp.full((1,), n_blk * rb, jnp.int32)])
    n_used = pad_end[-1:] // rb
    blk = jnp.minimum(jnp.arange(n_blk, dtype=jnp.int32), n_used - 1) * rb
    blk_expert = jnp.sum(pad_end[None, :] <= blk[:, None], axis=1).astype(jnp.int32)
    xs = _moe_scatter(x, gain, dest_t, fill.astype(jnp.int32), n_blk * rb, tm, rb)
    ys = _moe_experts(xs, blk_expert, n_used.astype(jnp.int32), w_gate, w_up, w_down, rb)
    return _moe_combine(x, route, dest_t, ys, tm)


def kernel(x, rel_bias, ev_mix_norm, ev_w_in, nsa_q_norm, nsa_k_norm, nsa_cmp_pos, nsa_cmp_w1, nsa_cmp_w2, mla_cq_norm, mla_ckv_norm, mla_w_uq, mla_w_ukv, mla_q_norm, mla_k_norm, ev_w_out, ev_ffn_norm, ffn_w_gate, ffn_w_up, ffn_w_down, od_mix_norm, od_w_in, conv_w, od_w_out, od_ffn_norm, moe_w_router, moe_b_router, moe_w_gate, moe_w_up, moe_w_down):
    bsz, seq, d = x.shape
    depth = ev_mix_norm.shape[0] + od_mix_norm.shape[0]
    pad = 512
    for layer in range(depth):
        i = layer // 2
        if layer % 2 == 0:
            qn, kv, kc_raw, vc_raw, gates, qm, km, vm = _even_in_proj(
                x, ev_mix_norm[i], ev_w_in[i], nsa_q_norm[i], nsa_k_norm[i], mla_cq_norm[i], mla_ckv_norm[i],
                mla_w_uq[i], mla_w_ukv[i], mla_q_norm[i], mla_k_norm[i], tm=pad)
            kc = _compress(kc_raw, nsa_cmp_pos[i, 0], nsa_cmp_w1[i, 0], nsa_cmp_w2[i, 0], nsa_k_norm[i, 0], True)
            vc = _compress(vc_raw, nsa_cmp_pos[i, 1], nsa_cmp_w1[i, 1], nsa_cmp_w2[i, 1], nsa_k_norm[i, 0], False)
            o_nsa = _nsa_attention(qn, kv, kc, vc, gates, rel_bias, pad)
            o_mla = _mla_attention(qm, km, vm)
            x = _even_out_ffn(x.reshape(bsz * seq, d), o_nsa.reshape(bsz * seq, -1), o_mla.reshape(bsz * seq, -1),
                              ev_w_out[i], ev_ffn_norm[i], ffn_w_gate[i], ffn_w_up[i], ffn_w_down[i]).reshape(bsz, seq, d)
        else:
            x = _conv_mixer(x, od_mix_norm[i], od_w_in[i], conv_w[i], od_w_out[i])
            x = _moe(x.reshape(bsz * seq, d), od_ffn_norm[i], moe_w_router[i], moe_b_router[i],
                     moe_w_gate[i], moe_w_up[i], moe_w_down[i]).reshape(bsz, seq, d)
    return x
```

```python
import functools
import math

import jax
import jax.numpy as jnp
import numpy as np
from jax import lax
from jax.experimental import pallas as pl
from jax.experimental.pallas import tpu as pltpu

F32 = jnp.float32
BF16 = jnp.bfloat16

EPS = 1e-6
NEG = -1e30
FORCED_SCORE = 1e9
NSA_HEADS = 8
NSA_GROUPS = 2
NSA_REP = NSA_HEADS // NSA_GROUPS
NSA_DK = 64
CMP_BLOCK = 32
CMP_STRIDE = 16
CMP_HIDDEN = 256
SEL_BLOCK = 64
SEL_TOP_N = 16
WINDOW = 512
MLA_HEADS = 8
MLA_Q_RANK = 256
MLA_KV_RANK = 128
MLA_NOPE = 64
MLA_ROPE = 32
MLA_V = 64
ROPE_THETA = 10000.0
REL_BUCKETS = 32
REL_MAX_DIST = 128
CONV_WIDTH = 3
N_EXPERTS = 8
TOP_K = 2
EVEN_IN_SIZES = (512,) + (128,) * 6 + (24, 256, 128, 32)

LANE = 128
VMEM_LIMIT = 56 * 1024 * 1024
SEL_MASK = -30000.0
FAR_DIST = 128

_Q0, _KV0, _KC0, _VC0, _GT0, _CQ0, _CKV0, _KR0, _EVEN_W = 0, 1024, 2048, 2176, 2304, 2560, 2816, 2944, 3072


def _dot(a, b):
    return jnp.dot(a, b, preferred_element_type=F32)


def _dot_nt(a, b):
    return lax.dot_general(a, b, (((1,), (1,)), ((), ())), preferred_element_type=F32)


def _rms(x, gain, n):
    ss = jnp.sum(x * x, axis=-1, keepdims=True) * (1.0 / n)
    return x * lax.rsqrt(ss + EPS) * gain


def _params(*sem):
    return pltpu.CompilerParams(dimension_semantics=sem, vmem_limit_bytes=VMEM_LIMIT)


def _const_spec(shape):
    nd = len(shape)
    return pl.BlockSpec(shape, lambda *_: (0,) * nd, pipeline_mode=pl.Buffered(1))


def _rope(x, cos, sa, sb):
    return x * cos + pltpu.roll(x, 16, 1) * sa + pltpu.roll(x, LANE - 16, 1) * sb


def _even_in_kernel(x_ref, gmix_ref, w_ref, gq_ref, gks_ref, gkw_ref, gcq_ref, gckv_ref,
                    wuq_ref, wuk_ref, wuv_ref, gmq_ref, gmk_ref, cos_ref, sa_ref, sb_ref,
                    qn_ref, kv_ref, kc_ref, vc_ref, gate_ref, qm_ref, km_ref, vm_ref, *, tm):
    j = pl.program_id(1)

    @pl.when(j == 0)
    def _():
        kv_ref[...] = jnp.zeros_like(kv_ref)
        qn_ref[...] = jnp.zeros_like(qn_ref)
        kc_ref[...] = jnp.zeros_like(kc_ref)
        vc_ref[...] = jnp.zeros_like(vc_ref)
        gate_ref[...] = jnp.zeros_like(gate_ref)
        qm_ref[...] = jnp.zeros_like(qm_ref)
        km_ref[...] = jnp.zeros_like(km_ref)
        vm_ref[...] = jnp.zeros_like(vm_ref)

    @pl.when(j > 0)
    def _():
        xn = _rms(x_ref[...], gmix_ref[...], x_ref.shape[-1]).astype(BF16)
        hq = _dot(xn, w_ref[:, _Q0:_Q0 + 1024])
        gq = gq_ref[...]
        for h in range(NSA_HEADS):
            seg = hq[:, LANE * h:LANE * (h + 1)]
            qn_ref[:, LANE * h:LANE * (h + 1)] = (_rms(seg, gq, NSA_DK) * NSA_DK ** -0.5).astype(BF16)
        hkv = _dot(xn, w_ref[:, _KV0:_KV0 + 1024])
        pos = (j - 1) * tm + lax.broadcasted_iota(jnp.int32, (tm, LANE), 0)
        lane = lax.broadcasted_iota(jnp.int32, (tm, LANE), 1)
        onehot = jnp.where(lane - NSA_DK == pos // SEL_BLOCK, 1.0, 0.0)
        for g in range(NSA_GROUPS):
            o = 4 * LANE * g
            kv_ref[:, o:o + LANE] = (_rms(hkv[:, o:o + LANE], gks_ref[...], NSA_DK) + onehot).astype(BF16)
            kv_ref[:, o + LANE:o + 2 * LANE] = hkv[:, o + LANE:o + 2 * LANE].astype(BF16)
            kv_ref[:, o + 2 * LANE:o + 3 * LANE] = _rms(hkv[:, o + 2 * LANE:o + 3 * LANE], gkw_ref[...],
                                                        NSA_DK).astype(BF16)
            kv_ref[:, o + 3 * LANE:o + 4 * LANE] = hkv[:, o + 3 * LANE:o + 4 * LANE].astype(BF16)
        hc = _dot(xn, w_ref[:, _KC0:_KC0 + 256])
        kc_ref[...] = hc[:, :LANE].astype(BF16)
        vc_ref[...] = hc[:, LANE:].astype(BF16)
        gate_ref[...] = jax.nn.sigmoid(_dot(xn, w_ref[:, _GT0:_GT0 + 256]))
        hm = _dot(xn, w_ref[:, _CQ0:_EVEN_W])
        cq = _rms(hm[:, :MLA_Q_RANK], gcq_ref[...], MLA_Q_RANK).astype(BF16)
        ckv = _rms(hm[:, MLA_Q_RANK:MLA_Q_RANK + MLA_KV_RANK], gckv_ref[...], MLA_KV_RANK).astype(BF16)
        k_rope = hm[:, MLA_Q_RANK + MLA_KV_RANK:]
        qm = _dot(cq, wuq_ref[...])
        kn = _dot(ckv, wuk_ref[...])
        vm_ref[...] = _dot(ckv, wuv_ref[...]).astype(BF16)
        cos, sa, sb = cos_ref[...], sa_ref[...], sb_ref[...]
        dqk = MLA_NOPE + MLA_ROPE
        for h in range(MLA_HEADS):
            sl = slice(LANE * h, LANE * (h + 1))
            qh = _rope(_rms(qm[:, sl], gmq_ref[...], dqk), cos, sa, sb) * dqk ** -0.5
            qm_ref[:, sl] = qh.astype(BF16)
            kh = _rope(_rms(kn[:, sl] + k_rope, gmk_ref[...], dqk), cos, sa, sb)
            km_ref[:, sl] = kh.astype(BF16)


def _even_in_weights(w_in):
    offs = np.concatenate([[0], np.cumsum(EVEN_IN_SIZES)])
    part = [w_in[:, offs[n]:offs[n + 1]] for n in range(len(EVEN_IN_SIZES))]
    q, k_c, v_c, k_s, v_s, k_w, v_w, gates, c_q, c_kv, k_rope = part
    d = w_in.shape[0]
    z = lambda n: jnp.zeros((d, n), w_in.dtype)
    cols = []
    for h in range(NSA_HEADS):
        cols += [q[:, 64 * h:64 * h + 64], z(64)]
    for g in range(NSA_GROUPS):
        s = slice(64 * g, 64 * g + 64)
        cols += [k_s[:, s], z(64), v_s[:, s], v_s[:, s], k_w[:, s], z(64), v_w[:, s], v_w[:, s]]
    cols += [k_c, v_c]
    for g in range(NSA_GROUPS):
        cols += [gates[:, 12 * g:12 * g + 12], z(LANE - 12)]
    cols += [c_q, c_kv, z(64), k_rope, z(32)]
    w = jnp.concatenate(cols, axis=1)
    assert w.shape[1] == _EVEN_W
    return w.astype(BF16)


def _pad_gain(g, width):
    return jnp.pad(g.astype(F32), (0, width - g.shape[0]))[None, :]


def _rope_tables(seq):
    half = MLA_ROPE // 2
    inv_freq = ROPE_THETA ** (-jnp.arange(half, dtype=F32) / half)
    ang = jnp.arange(seq).astype(F32)[:, None] * inv_freq[None, :]
    cos, sin = jnp.cos(ang), jnp.sin(ang)
    one = jnp.ones((seq, MLA_NOPE), F32)
    zn = jnp.zeros((seq, MLA_NOPE), F32)
    zt = jnp.zeros((seq, LANE - MLA_NOPE - MLA_ROPE), F32)
    zh = jnp.zeros((seq, half), F32)
    cos_t = jnp.concatenate([one, cos, cos, zt + 1.0], axis=1)
    sa = jnp.concatenate([zn, zh, sin, zt], axis=1)
    sb = jnp.concatenate([zn, -sin, zh, zt], axis=1)
    return cos_t, sa, sb


def _even_in_proj(x, gmix, w_in, q_norm, k_norm, cq_norm, ckv_norm, w_uq, w_ukv, mq_norm, mk_norm, tm=512):
    bsz, seq, d = x.shape
    nt = seq // tm
    w = _even_in_weights(w_in)
    dqk = MLA_NOPE + MLA_ROPE
    wuq = jnp.concatenate(
        [jnp.pad(w_uq[:, dqk * h:dqk * (h + 1)], ((0, 0), (0, LANE - dqk))) for h in range(MLA_HEADS)],
        axis=1).astype(BF16)
    kvw = MLA_NOPE + MLA_V
    wuk = jnp.concatenate(
        [jnp.pad(w_ukv[:, kvw * h:kvw * h + MLA_NOPE], ((0, 0), (0, LANE - MLA_NOPE))) for h in range(MLA_HEADS)],
        axis=1).astype(BF16)
    wuv = jnp.concatenate([w_ukv[:, kvw * h + MLA_NOPE:kvw * (h + 1)] for h in range(MLA_HEADS)], axis=1).astype(BF16)
    cos_t, sa, sb = _rope_tables(seq)
    tok = lambda width: pl.BlockSpec((None, tm, width), lambda b, j: (b, jnp.maximum(j - 1, 0), 0))
    postab = pl.BlockSpec((tm, LANE), lambda b, j: (jnp.maximum(j - 1, 0), 0))
    in_specs = [tok(d), _const_spec((1, d)), _const_spec((d, _EVEN_W)),
                _const_spec((1, LANE)), _const_spec((1, LANE)), _const_spec((1, LANE)),
                _const_spec((1, MLA_Q_RANK)), _const_spec((1, MLA_KV_RANK)),
                _const_spec(wuq.shape), _const_spec(wuk.shape), _const_spec(wuv.shape),
                _const_spec((1, LANE)), _const_spec((1, LANE)), postab, postab, postab]
    out_shape = [jax.ShapeDtypeStruct((bsz, seq, 1024), BF16),
                 jax.ShapeDtypeStruct((bsz, seq + tm, 1024), BF16),
                 jax.ShapeDtypeStruct((bsz, seq, LANE), BF16),
                 jax.ShapeDtypeStruct((bsz, seq, LANE), BF16),
                 jax.ShapeDtypeStruct((bsz, seq, 2 * LANE), F32),
                 jax.ShapeDtypeStruct((bsz, seq, 1024), BF16),
                 jax.ShapeDtypeStruct((bsz, seq, 1024), BF16),
                 jax.ShapeDtypeStruct((bsz, seq, 512), BF16)]
    out_specs = [tok(1024), pl.BlockSpec((None, tm, 1024), lambda b, j: (b, j, 0)), tok(LANE), tok(LANE),
                 tok(2 * LANE), tok(1024), tok(1024), tok(512)]
    return pl.pallas_call(
        functools.partial(_even_in_kernel, tm=tm), grid=(bsz, nt + 1), in_specs=in_specs, out_specs=out_specs,
        out_shape=out_shape, compiler_params=_params("parallel", "arbitrary"), name="even_in_proj",
    )(x, gmix[None, :], w, _pad_gain(q_norm, LANE), _pad_gain(k_norm[1], LANE), _pad_gain(k_norm[2], LANE),
      cq_norm[None, :], ckv_norm[None, :], wuq, wuk, wuv, _pad_gain(mq_norm, LANE), _pad_gain(mk_norm, LANE),
      cos_t, sa, sb)


def _cmp_kernel(x_ref, wa_ref, wb_ref, pa_ref, pb_ref, w2_ref, gain_ref, o_ref, *, normalize):
    x = x_ref[...]
    ua = _dot(x, wa_ref[...])
    ub = _dot(x, wb_ref[...])
    pt = _dot(pa_ref[...], wa_ref[...]) + _dot(pb_ref[...], wb_ref[...])
    n = ub.shape[0]
    pre = ua + pltpu.roll(ub, n - 1, 0) + pt[0:1]
    hid = jax.nn.gelu(pre).astype(BF16)
    out = _dot(hid, w2_ref[...])
    if normalize:
        gain = gain_ref[...]
        for g in range(NSA_GROUPS):
            sl = slice(LANE * g, LANE * (g + 1))
            o_ref[:, sl] = _rms(out[:, sl], gain, NSA_DK).astype(BF16)
    else:
        o_ref[...] = out.astype(BF16)


def _compress(kv, pos_emb, w1, w2, gain, normalize):
    bsz, seq, _ = kv.shape
    half = CMP_BLOCK // 2
    assert CMP_STRIDE == half
    nrow = seq // CMP_STRIDE
    x = kv.reshape(bsz, nrow, CMP_STRIDE * LANE)
    w1r = w1.reshape(CMP_BLOCK, NSA_DK, CMP_HIDDEN)
    eye = jnp.eye(NSA_GROUPS, dtype=w1.dtype)
    widen = lambda w: jnp.einsum("ldn,gh->lgdhn", w, eye).reshape(half * LANE, NSA_GROUPS * CMP_HIDDEN).astype(BF16)
    wa, wb = widen(w1r[:half]), widen(w1r[half:])
    prow = lambda p: jnp.pad(jnp.broadcast_to(p[:, None, :], (half, NSA_GROUPS, NSA_DK)).reshape(1, half * LANE),
                             ((0, 15), (0, 0))).astype(BF16)
    pa, pb = prow(pos_emb[:half]), prow(pos_emb[half:])
    second = jnp.zeros_like(w2) if normalize else w2
    w2w = jnp.einsum("nd,gh->gnhd", jnp.concatenate([w2, second], axis=1), eye)
    w2w = w2w.reshape(NSA_GROUPS * CMP_HIDDEN, NSA_GROUPS * LANE).astype(BF16)
    return pl.pallas_call(
        functools.partial(_cmp_kernel, normalize=normalize), grid=(bsz,),
        in_specs=[pl.BlockSpec((None, nrow, CMP_STRIDE * LANE), lambda b: (b, 0, 0)),
                  _const_spec(wa.shape), _const_spec(wb.shape), _const_spec(pa.shape), _const_spec(pb.shape),
                  _const_spec(w2w.shape), _const_spec((1, LANE))],
        out_specs=pl.BlockSpec((None, nrow, NSA_GROUPS * LANE), lambda b: (b, 0, 0)),
        out_shape=jax.ShapeDtypeStruct((bsz, nrow, NSA_GROUPS * LANE), BF16),
        compiler_params=_params("parallel"), name="nsa_compress",
    )(x, wa, wb, pa, pb, w2w, _pad_gain(gain, LANE))


def _bucket_table():
    dist = np.arange(FAR_DIST + 1)
    max_exact = REL_BUCKETS // 2
    nf = np.maximum(dist, max_exact).astype(np.float32)
    large = max_exact + (np.log(nf / max_exact) / math.log(REL_MAX_DIST / max_exact)
                         * (REL_BUCKETS - max_exact)).astype(np.int32)
    return np.where(dist < max_exact, dist, np.minimum(large, REL_BUCKETS - 1))


CMP_NEAR = 32


def _expand(tbl, idx):
    idx = np.asarray(idx)
    onehot = jnp.asarray(np.eye(tbl.shape[1], dtype=np.float32)[idx.reshape(-1)])
    out = lax.dot_general(tbl, onehot, (((1,), (1,)), ((), ())), precision=lax.Precision.HIGHEST)
    return out.reshape((tbl.shape[0],) + idx.shape)


def _toeplitz_tile(tbl, far, width, dist0, valid):
    period = width + SEL_BLOCK + 1
    j = np.arange(period)
    dist = dist0 - np.where(j < width, j, j - period)
    gen = jnp.where(jnp.asarray(valid(dist))[None], _expand(tbl, np.clip(dist, 0, FAR_DIST)) - far, NEG)
    flat = jnp.tile(gen, (1, SEL_BLOCK))[:, :SEL_BLOCK * (period - 1)]
    return flat.reshape(tbl.shape[0], SEL_BLOCK, period - 1)[:, :, :width]


def _bias_tables(rel_bias):
    tbl = _expand(rel_bias.astype(F32).T, _bucket_table())
    far = tbl[:, FAR_DIST:]
    bias_near = _toeplitz_tile(tbl, far, 4 * SEL_BLOCK, 3 * SEL_BLOCK, lambda d: d >= 0)
    bias_win = _toeplitz_tile(tbl, 0.0, WINDOW + SEL_BLOCK, WINDOW, lambda d: (d >= 0) & (d < WINDOW))
    d_c = (np.arange(SEL_BLOCK)[:, None] + (CMP_NEAR // 2) * CMP_STRIDE - (CMP_BLOCK - 1)
           - CMP_STRIDE * np.arange(CMP_NEAR)[None, :])
    assert d_c[:, 0].min() >= FAR_DIST and d_c[:, -1].max() < 0
    bias_cmp = jnp.where(jnp.asarray(d_c >= 0)[None], _expand(tbl, np.clip(d_c, 0, FAR_DIST)) - far[:, :, None], 0.0)
    return bias_cmp, bias_near, bias_win


def _cmp_mask_dist(seq):
    a = np.tile(np.arange(SEL_BLOCK), NSA_REP)[:, None]
    c = np.arange(seq // CMP_STRIDE)[None, :]
    return jnp.asarray(c * CMP_STRIDE + CMP_BLOCK - 1 - a, jnp.int32)


def _overlap_t(seq):
    ncp = seq // CMP_STRIDE
    cs = np.arange(ncp)[None, :] * CMP_STRIDE
    ss = np.arange(SEL_BLOCK)[:, None] * SEL_BLOCK
    ov = (cs < ss + SEL_BLOCK) & (cs + CMP_BLOCK - 1 >= ss) & (np.arange(ncp)[None, :] < (seq - CMP_BLOCK) // CMP_STRIDE + 1)
    return jnp.asarray(ov, BF16)


def _softmax_step(s, v, state):
    m, l, acc = state
    mn = jnp.maximum(m, jnp.max(s, axis=-1, keepdims=True))
    alpha = jnp.exp(m - mn)
    p = jnp.exp(s - mn)
    return mn, alpha * l + jnp.sum(p, axis=-1, keepdims=True), alpha * acc + _dot(p.astype(BF16), v)


FAR_CHUNK = 512


def _lane_fold(x, op):
    out = x[:, 0:LANE]
    for c in range(1, x.shape[1] // LANE):
        out = op(out, x[:, LANE * c:LANE * (c + 1)])
    return out


def _nsa_kernel(q_ref, kv_ref, kc_ref, vc_ref, g_ref, bc_ref, bn_ref, bw_ref, ovt_ref, dmask_ref, o_ref, s_ref, *, pad):
    i = pl.program_id(1)
    qb, rep, ng = SEL_BLOCK, NSA_REP, NSA_GROUPS
    rows = qb * rep
    gw = 4 * LANE
    ncp = kc_ref.shape[0]
    qs = [jnp.concatenate([q_ref[:, gw * g + LANE * r:gw * g + LANE * (r + 1)] for r in range(rep)], axis=0)
          for g in range(ng)]

    m_i = lax.broadcasted_iota(jnp.int32, (CMP_NEAR, ncp), 0)
    c_i = lax.broadcasted_iota(jnp.int32, (CMP_NEAR, ncp), 1)
    shift = jnp.where(c_i - m_i == (qb // CMP_STRIDE) * i - CMP_NEAR // 2, 1.0, 0.0).astype(BF16)
    visible = dmask_ref[...] <= i * qb
    o_c, psums = [], []
    for g in range(ng):
        base = bc_ref[rep * g:rep * (g + 1)].reshape(rows, CMP_NEAR)
        b_hi = base.astype(BF16)
        b_lo = (base - b_hi.astype(F32)).astype(BF16)
        s = _dot_nt(qs[g], kc_ref[:, LANE * g:LANE * (g + 1)]) + _dot(b_hi, shift) + _dot(b_lo, shift)
        s = jnp.where(visible, s, NEG)
        m = jnp.maximum(jnp.max(s, axis=-1, keepdims=True), -1e20)
        e = jnp.exp(s - m)
        inv = 1.0 / jnp.maximum(jnp.sum(e, axis=-1, keepdims=True), 1e-30)
        o_c.append(_dot(e.astype(BF16), vc_ref[:, LANE * g:LANE * (g + 1)]) * inv)
        p = e * inv
        psums.append(p[0:qb] + p[qb:2 * qb] + p[2 * qb:3 * qb] + p[3 * qb:4 * qb])

    psum = jnp.concatenate(psums, axis=0)
    hi = psum.astype(BF16)
    r1 = psum - hi.astype(F32)
    mid = r1.astype(BF16)
    lo = (r1 - mid.astype(F32)).astype(BF16)
    ovt = ovt_ref[...]
    imp = _dot_nt(ovt, hi) + _dot_nt(ovt, mid) + _dot_nt(ovt, lo)
    jj = lax.broadcasted_iota(jnp.int32, (qb, ng * qb), 0)
    imp = jnp.where((jj == i) | (jj == 0), FORCED_SCORE, imp)
    imp = jnp.where(jj > i, -1.0, imp)
    grp = [imp[8 * a:8 * a + 8] for a in range(8)]
    cnt = [jnp.zeros((8, ng * qb), F32) for _ in range(8)]
    sub = lax.broadcasted_iota(jnp.int32, (8, ng * qb), 0)
    for k in range(qb):
        rk = imp[k:k + 1, :]
        for a in range(8):
            if 8 * a + 7 <= k:
                cnt[a] = cnt[a] + jnp.where(rk > grp[a], 1.0, 0.0)
            elif 8 * a > k:
                cnt[a] = cnt[a] + jnp.where(rk >= grp[a], 1.0, 0.0)
            else:
                cnt[a] = cnt[a] + jnp.where(sub + 8 * a > k, jnp.where(rk >= grp[a], 1.0, 0.0),
                                            jnp.where(rk > grp[a], 1.0, 0.0))
    neg = jnp.where(jnp.concatenate(cnt, axis=0) < SEL_TOP_N, 0.0, SEL_MASK)
    neg_far = jnp.where(jj >= i - 3, SEL_MASK, neg)
    zero = jnp.zeros((qb, ng * qb), BF16)
    ext_near = jnp.concatenate([zero, neg.astype(BF16)], axis=0)
    ext_far = jnp.concatenate([zero, neg_far.astype(BF16)], axis=0)
    q_row = lax.broadcasted_iota(jnp.int32, (qb, ng * qb), 0)
    q_lane = lax.broadcasted_iota(jnp.int32, (qb, ng * qb), 1)
    q_near, q_far = [], []
    for g in range(ng):
        pick = jnp.where(q_lane == q_row + qb * g, 1.0, 0.0).astype(BF16)
        q_near.append(qs[g] + jnp.concatenate([_dot_nt(pick, ext_near).astype(BF16)] * rep, axis=0))
        q_far.append(qs[g] + jnp.concatenate([_dot_nt(pick, ext_far).astype(BF16)] * rep, axis=0))

    st_n = pl.multiple_of(pad + (i - 3) * qb, qb)
    in_seq = lax.broadcasted_iota(jnp.int32, (rows, 4 * qb), 1) + (i - 3) * qb >= 0
    s_near = [jnp.where(in_seq, _dot_nt(q_near[g], kv_ref[pl.ds(st_n, 4 * qb), gw * g:gw * g + LANE])
                        + bn_ref[rep * g:rep * (g + 1)].reshape(rows, 4 * qb), NEG) for g in range(ng)]

    nch = (jnp.maximum(i - 3, 0) * qb + FAR_CHUNK - 1) // FAR_CHUNK

    def pass1(c, mx):
        st = pl.multiple_of(pad + c * FAR_CHUNK, FAR_CHUNK)
        col = pl.multiple_of(c * FAR_CHUNK, FAR_CHUNK)
        out = []
        for g in range(ng):
            s = _dot_nt(q_far[g], kv_ref[pl.ds(st, FAR_CHUNK), gw * g:gw * g + LANE])
            s_ref[g, :, pl.ds(col, FAR_CHUNK)] = s
            out.append(jnp.maximum(mx[g], _lane_fold(s, jnp.maximum)))
        return tuple(out)

    mx = lax.fori_loop(0, nch, pass1, tuple(jnp.full((rows, LANE), NEG, F32) for _ in range(ng)))
    ms = [jnp.max(jnp.maximum(mx[g], _lane_fold(s_near[g], jnp.maximum)), axis=-1, keepdims=True) for g in range(ng)]

    def pass2(c, state):
        st = pl.multiple_of(pad + c * FAR_CHUNK, FAR_CHUNK)
        col = pl.multiple_of(c * FAR_CHUNK, FAR_CHUNK)
        out = []
        for g in range(ng):
            l, acc = state[g]
            p = jnp.exp(s_ref[g, :, pl.ds(col, FAR_CHUNK)] - ms[g])
            pv = _dot(p.astype(BF16), kv_ref[pl.ds(st, FAR_CHUNK), gw * g + LANE:gw * g + 2 * LANE])
            out.append((l + _lane_fold(p, jnp.add), acc + pv))
        return tuple(out)

    zero_state = (jnp.zeros((rows, LANE), F32), jnp.zeros((rows, LANE), F32))
    far = lax.fori_loop(0, nch, pass2, tuple(zero_state for _ in range(ng)))
    o_s = []
    for g in range(ng):
        p = jnp.exp(s_near[g] - ms[g])
        l = jnp.sum(far[g][0] + _lane_fold(p, jnp.add), axis=-1, keepdims=True)
        pv = _dot(p.astype(BF16), kv_ref[pl.ds(st_n, 4 * qb), gw * g + LANE:gw * g + 2 * LANE])
        o_s.append((pv + far[g][1]) / l)

    wk = WINDOW + qb
    st_w = pl.multiple_of(pad - WINDOW + i * qb, qb)
    in_seq = lax.broadcasted_iota(jnp.int32, (rows, wk), 1) + i * qb - WINDOW >= 0
    o_w = []
    for g in range(ng):
        s = _dot_nt(qs[g], kv_ref[pl.ds(st_w, wk), gw * g + 2 * LANE:gw * g + 3 * LANE])
        s = jnp.where(in_seq, s + bw_ref[rep * g:rep * (g + 1)].reshape(rows, wk), NEG)
        e = jnp.exp(s - jnp.max(s, axis=-1, keepdims=True))
        pv = _dot(e.astype(BF16), kv_ref[pl.ds(st_w, wk), gw * g + 3 * LANE:gw * g + 4 * LANE])
        o_w.append(pv / jnp.sum(e, axis=-1, keepdims=True))

    gates = g_ref[...]
    low = lax.broadcasted_iota(jnp.int32, (qb, LANE), 1) < NSA_DK
    for g in range(ng):
        outs = []
        for r in range(rep):
            sl = slice(qb * r, qb * (r + 1))
            c0 = LANE * g + 3 * r
            outs.append(gates[:, c0:c0 + 1] * o_c[g][sl] + gates[:, c0 + 1:c0 + 2] * o_s[g][sl]
                        + gates[:, c0 + 2:c0 + 3] * o_w[g][sl])
        for pr in range(rep // 2):
            c0 = 2 * LANE * g + LANE * pr
            o_ref[:, c0:c0 + LANE] = jnp.where(low, outs[2 * pr], outs[2 * pr + 1]).astype(o_ref.dtype)


def _nsa_attention(qn, kv, kc, vc, gates, rel_bias, pad):
    bsz, seq, _ = qn.shape
    qb, ng = SEL_BLOCK, NSA_GROUPS
    ncp = seq // CMP_STRIDE
    assert seq // qb <= qb and pad >= WINDOW and seq % FAR_CHUNK == 0
    bias_cmp, bias_near, bias_win = _bias_tables(rel_bias)
    ovt = _overlap_t(seq)
    dmask = _cmp_mask_dist(seq)
    tok = lambda width: pl.BlockSpec((None, qb, width), lambda b, i: (b, i, 0))
    per_b = lambda n, width: pl.BlockSpec((None, n, width), lambda b, i: (b, 0, 0))
    return pl.pallas_call(
        functools.partial(_nsa_kernel, pad=pad), grid=(bsz, seq // qb),
        in_specs=[tok(ng * 4 * LANE), per_b(seq + pad, ng * 4 * LANE), per_b(ncp, ng * LANE), per_b(ncp, ng * LANE),
                  tok(ng * LANE), _const_spec(bias_cmp.shape), _const_spec(bias_near.shape),
                  _const_spec(bias_win.shape), _const_spec(ovt.shape), _const_spec(dmask.shape)],
        out_specs=tok(NSA_HEADS * NSA_DK),
        out_shape=jax.ShapeDtypeStruct((bsz, seq, NSA_HEADS * NSA_DK), BF16),
        scratch_shapes=[pltpu.VMEM((ng, NSA_REP * qb, seq), F32)],
        compiler_params=_params("parallel", "arbitrary"), name="nsa_attention",
    )(qn, kv, kc, vc, gates, bias_cmp, bias_near, bias_win, ovt, dmask)


def _mla_attn_kernel(q_ref, k_ref, v_ref, o_ref, *, tq):
    i = pl.program_id(2)
    heads = [slice(LANE * hh, LANE * (hh + 1)) for hh in range(2)]
    qs = [q_ref[:, sl] for sl in heads]

    def body(c, states):
        st = pl.multiple_of(c * tq, tq)
        v = v_ref[pl.ds(st, tq), :]
        return tuple(_softmax_step(_dot_nt(q, k_ref[pl.ds(st, tq), sl]), v, state)
                     for q, sl, state in zip(qs, heads, states))

    init = (jnp.full((tq, 1), NEG, F32), jnp.zeros((tq, 1), F32), jnp.zeros((tq, LANE), F32))
    states = lax.fori_loop(0, i, body, (init, init))
    st = pl.multiple_of(i * tq, tq)
    causal = lax.broadcasted_iota(jnp.int32, (tq, tq), 0) >= lax.broadcasted_iota(jnp.int32, (tq, tq), 1)
    outs = []
    for q, sl, state in zip(qs, heads, states):
        s = _dot_nt(q, k_ref[pl.ds(st, tq), sl])
        _, l, acc = _softmax_step(jnp.where(causal, s, NEG), v_ref[pl.ds(st, tq), :], state)
        outs.append(acc / l)
    low = lax.broadcasted_iota(jnp.int32, (tq, LANE), 1) < MLA_V
    o_ref[...] = jnp.where(low, outs[0], outs[1]).astype(o_ref.dtype)


def _mla_attention(qm, km, vm, tq=512):
    bsz, seq, _ = qm.shape
    tq = min(tq, seq)
    return pl.pallas_call(
        functools.partial(_mla_attn_kernel, tq=tq), grid=(bsz, MLA_HEADS // 2, seq // tq),
        in_specs=[pl.BlockSpec((None, tq, 2 * LANE), lambda b, h, i: (b, i, h)),
                  pl.BlockSpec((None, seq, 2 * LANE), lambda b, h, i: (b, 0, h)),
                  pl.BlockSpec((None, seq, LANE), lambda b, h, i: (b, 0, h))],
        out_specs=pl.BlockSpec((None, tq, LANE), lambda b, h, i: (b, i, h)),
        out_shape=jax.ShapeDtypeStruct((bsz, seq, MLA_HEADS * MLA_V), BF16),
        compiler_params=_params("parallel", "parallel", "arbitrary"), name="mla_attention",
    )(qm, km, vm)


def _even_out_kernel(x_ref, on_ref, om_ref, wn_ref, wm_ref, g_ref, wg_ref, wu_ref, wd_ref, o_ref, *, chunk):
    x1 = x_ref[...] + _dot(on_ref[...], wn_ref[...]) + _dot(om_ref[...], wm_ref[...])
    n = _rms(x1, g_ref[...], x1.shape[-1]).astype(BF16)
    ffn = None
    for f0 in range(0, wg_ref.shape[1], chunk):
        gate = _dot(n, wg_ref[:, f0:f0 + chunk])
        act = (gate * jax.nn.sigmoid(gate) * _dot(n, wu_ref[:, f0:f0 + chunk])).astype(BF16)
        part = _dot(act, wd_ref[f0:f0 + chunk, :])
        ffn = part if ffn is None else ffn + part
    o_ref[...] = x1 + ffn


def _even_out_ffn(x, o_nsa, o_mla, w_out, gain, w_gate, w_up, w_down, tm=512):
    t, d = x.shape
    dff = w_gate.shape[1]
    wn, wm = w_out[:o_nsa.shape[1]].astype(BF16), w_out[o_nsa.shape[1]:].astype(BF16)
    row = lambda width: pl.BlockSpec((tm, width), lambda i: (i, 0))
    return pl.pallas_call(
        functools.partial(_even_out_kernel, chunk=dff // 2), grid=(t // tm,),
        in_specs=[row(d), row(o_nsa.shape[1]), row(o_mla.shape[1]), _const_spec(wn.shape), _const_spec(wm.shape),
                  _const_spec((1, d)), _const_spec((d, dff)), _const_spec((d, dff)), _const_spec((dff, d))],
        out_specs=row(d), out_shape=jax.ShapeDtypeStruct((t, d), F32),
        compiler_params=_params("parallel"), name="even_out_ffn",
    )(x, o_nsa, o_mla, wn, wm, gain[None, :], w_gate.astype(BF16), w_up.astype(BF16), w_down.astype(BF16))


def _conv_kernel(x_ref, g_ref, win_ref, cw_ref, wout_ref, o_ref, vbuf_ref, *, tm):
    j = pl.program_id(1)
    x = x_ref[...]
    d = x.shape[-1]
    n = _rms(x, g_ref[...], d).astype(BF16)
    b_gate = _dot(n, win_ref[:, 0:d])
    v = _dot(n, win_ref[:, d:2 * d]) * _dot(n, win_ref[:, 2 * d:3 * d])

    @pl.when(j == 0)
    def _():
        vbuf_ref[0:8, :] = jnp.zeros((8, d), F32)

    vbuf_ref[8:8 + tm, :] = v
    cw = cw_ref[...]
    y = cw[2:3] * v + cw[1:2] * vbuf_ref[7:7 + tm, :] + cw[0:1] * vbuf_ref[6:6 + tm, :]
    vbuf_ref[0:8, :] = v[tm - 8:tm]
    o_ref[...] = x + _dot((b_gate * y).astype(BF16), wout_ref[...])


def _conv_mixer(x, gain, w_in, conv_w, w_out, tm=512):
    bsz, seq, d = x.shape
    cw = jnp.pad(conv_w.astype(F32), ((0, 8 - CONV_WIDTH), (0, 0)))
    tok = pl.BlockSpec((None, tm, d), lambda b, j: (b, j, 0))
    return pl.pallas_call(
        functools.partial(_conv_kernel, tm=tm), grid=(bsz, seq // tm),
        in_specs=[tok, _const_spec((1, d)), _const_spec((d, 3 * d)), _const_spec((8, d)), _const_spec((d, d))],
        out_specs=tok, out_shape=jax.ShapeDtypeStruct((bsz, seq, d), F32),
        scratch_shapes=[pltpu.VMEM((tm + 8, d), F32)],
        compiler_params=_params("parallel", "arbitrary"), name="conv_mixer",
    )(x, gain[None, :], w_in.astype(BF16), cw, w_out.astype(BF16))


def _router_kernel(x_ref, g_ref, wr_ref, br_ref, tri_ref, route_ref, cnt_ref, carry_ref, *, tm):
    t = pl.program_id(0)

    @pl.when(t == 0)
    def _():
        carry_ref[...] = jnp.zeros_like(carry_ref)

    n = _rms(x_ref[...], g_ref[...], x_ref.shape[-1])
    hi = n.astype(BF16)
    lo = (n - hi.astype(F32)).astype(BF16)
    whi, wlo = wr_ref[0], wr_ref[1]
    logits = _dot(hi, whi) + _dot(lo, whi) + _dot(hi, wlo) + br_ref[...]
    lane = lax.broadcasted_iota(jnp.int32, (tm, LANE), 1).astype(F32)
    big = float(LANE)
    m1 = jnp.max(logits, axis=-1, keepdims=True)
    e1 = jnp.min(jnp.where(logits == m1, lane, big), axis=-1, keepdims=True)
    rest = jnp.where(lane == e1, NEG, logits)
    m2 = jnp.max(rest, axis=-1, keepdims=True)
    e2 = jnp.min(jnp.where(rest == m2, lane, big), axis=-1, keepdims=True)
    z = jnp.exp(m2 - m1)
    w1 = 1.0 / (1.0 + z)
    w2 = z / (1.0 + z)
    oh1 = jnp.where(lane == e1, 1.0, 0.0)
    oh2 = jnp.where(lane == e2, 1.0, 0.0)
    both = oh1 + oh2
    before = _dot(tri_ref[...], both.astype(BF16)) + carry_ref[0:1, :]
    r1 = jnp.sum(oh1 * before, axis=-1, keepdims=True)
    r2 = jnp.sum(oh2 * before, axis=-1, keepdims=True)
    cols = [e1, e2, w1, w2, r1, r2]
    out = jnp.zeros((tm, LANE), F32)
    for c, val in enumerate(cols):
        out = jnp.where(lane == c, val, out)
    route_ref[...] = out
    carry_ref[0:1, :] = carry_ref[0:1, :] + jnp.sum(both, axis=0, keepdims=True)
    cnt_ref[...] = carry_ref[...]


def _moe_router(x, gain, w_router, b_router, tm=512):
    t, d = x.shape
    wr = jnp.pad(w_router.astype(F32), ((0, 0), (0, LANE - N_EXPERTS)))
    whi = wr.astype(BF16)
    wlo = (wr - whi.astype(F32)).astype(BF16)
    br = jnp.concatenate([b_router.astype(F32), jnp.full((LANE - N_EXPERTS,), NEG, F32)])[None, :]
    tri = jnp.asarray(np.tril(np.ones((tm, tm), np.float32), -1), BF16)
    route, cnt = pl.pallas_call(
        functools.partial(_router_kernel, tm=tm), grid=(t // tm,),
        in_specs=[pl.BlockSpec((tm, d), lambda i: (i, 0)), _const_spec((1, d)), _const_spec((2, d, LANE)),
                  _const_spec((1, LANE)), _const_spec((tm, tm))],
        out_specs=[pl.BlockSpec((tm, LANE), lambda i: (i, 0)), pl.BlockSpec((8, LANE), lambda i: (0, 0))],
        out_shape=[jax.ShapeDtypeStruct((t, LANE), F32), jax.ShapeDtypeStruct((8, LANE), F32)],
        scratch_shapes=[pltpu.VMEM((8, LANE), F32)],
        compiler_params=_params("arbitrary"), name="moe_router",
    )(x, gain[None, :], jnp.stack([whi, wlo]), br, tri)
    return route, cnt[0, :N_EXPERTS].astype(jnp.int32)


def _row_copy(src, i, dst, j, sem):
    return pltpu.make_async_copy(src.at[pl.ds(i, 1)], dst.at[pl.ds(j, 1)], sem)


def _scatter_kernel(fill_ref, dest_ref, x_ref, g_ref, xs_ref, xn_ref, zero_ref, sem, *, tm, rb):
    @pl.when(pl.program_id(0) == 0)
    def _():
        zero_ref[...] = jnp.zeros_like(zero_ref)
        sizes = [rb >> s for s in range(rb.bit_length() - 3)]
        for e in range(N_EXPERTS + 1):
            lo, n = fill_ref[e], fill_ref[N_EXPERTS + 1 + e] - fill_ref[e]
            whole = n // rb

            def copy(off, size):
                return pltpu.make_async_copy(zero_ref.at[pl.ds(0, size)],
                                             xs_ref.at[pl.ds(pl.multiple_of(off, 8), size)], sem)

            def blocks(k, c, lo=lo, copy=copy):
                copy(lo + k * rb, rb).start()
                copy(lo + k * rb, rb).wait()
                return c

            lax.fori_loop(0, whole, blocks, 0)
            off = lo + whole * rb
            for size in sizes[1:]:
                @pl.when((n & size) != 0)
                def _(off=off, size=size, copy=copy):
                    copy(off, size).start()
                    copy(off, size).wait()
                off = off + (n & size)

    xn_ref[...] = _rms(x_ref[...], g_ref[...], x_ref.shape[-1])

    def start(r, c):
        _row_copy(xn_ref, r, xs_ref, dest_ref[0, 0, r], sem).start()
        _row_copy(xn_ref, r, xs_ref, dest_ref[0, 0, tm + r], sem).start()
        return c

    lax.fori_loop(0, tm, start, 0)
    for _ in range(TOP_K):
        pltpu.make_async_copy(xn_ref, xs_ref.at[pl.ds(0, tm)], sem).wait()


def _moe_scatter(x, gain, dest, fill, n_rows, tm, rb):
    t, d = x.shape
    nt = t // tm
    grid_spec = pltpu.PrefetchScalarGridSpec(
        num_scalar_prefetch=1, grid=(nt,),
        in_specs=[pl.BlockSpec((1, 1, 2 * tm), lambda i, f: (i, 0, 0), memory_space=pltpu.SMEM),
                  pl.BlockSpec((tm, d), lambda i, f: (i, 0)),
                  pl.BlockSpec((1, d), lambda i, f: (0, 0))],
        out_specs=pl.BlockSpec(memory_space=pl.ANY),
        scratch_shapes=[pltpu.VMEM((tm, d), F32), pltpu.VMEM((rb, d), F32), pltpu.SemaphoreType.DMA(())])
    return pl.pallas_call(
        functools.partial(_scatter_kernel, tm=tm, rb=rb), grid_spec=grid_spec,
        out_shape=jax.ShapeDtypeStruct((n_rows, d), F32),
        compiler_params=_params("arbitrary"), name="moe_scatter",
    )(fill, dest, x, gain[None, :])


def _expert_kernel(be_ref, nu_ref, x_ref, wg_ref, wu_ref, wd_ref, o_ref):
    used = pl.program_id(0) < nu_ref[0]

    @pl.when(used)
    def _():
        xb = x_ref[...].astype(BF16)
        gate = _dot(xb, wg_ref[...])
        act = (gate * jax.nn.sigmoid(gate) * _dot(xb, wu_ref[...])).astype(BF16)
        o_ref[...] = _dot(act, wd_ref[...])

    @pl.when(jnp.logical_not(used))
    def _():
        o_ref[...] = jnp.zeros_like(o_ref)


def _moe_experts(xs, blk_expert, n_used, w_gate, w_up, w_down, rb):
    n_rows, d = xs.shape
    dff = w_gate.shape[-1]
    rowblk = pl.BlockSpec((rb, d), lambda i, be, nu: (i, 0))
    wspec = lambda shape: pl.BlockSpec((None,) + shape, lambda i, be, nu: (be[i], 0, 0))
    grid_spec = pltpu.PrefetchScalarGridSpec(
        num_scalar_prefetch=2, grid=(n_rows // rb,),
        in_specs=[rowblk, wspec((d, dff)), wspec((d, dff)), wspec((dff, d))], out_specs=rowblk)
    return pl.pallas_call(
        _expert_kernel, grid_spec=grid_spec, out_shape=jax.ShapeDtypeStruct(xs.shape, F32),
        compiler_params=_params("arbitrary"), name="moe_experts",
    )(blk_expert, n_used, xs, w_gate.astype(BF16), w_up.astype(BF16), w_down.astype(BF16))


def _combine_kernel(dest_ref, x_ref, route_ref, ys_ref, o_ref, g1_ref, g2_ref, sem, *, tm):
    def start(r, c):
        _row_copy(ys_ref, dest_ref[0, 0, r], g1_ref, r, sem).start()
        _row_copy(ys_ref, dest_ref[0, 0, tm + r], g2_ref, r, sem).start()
        return c

    lax.fori_loop(0, tm, start, 0)
    for buf in (g1_ref, g2_ref):
        pltpu.make_async_copy(ys_ref.at[pl.ds(0, tm)], buf, sem).wait()
    route = route_ref[...]
    o_ref[...] = x_ref[...] + route[:, 2:3] * g1_ref[...] + route[:, 3:4] * g2_ref[...]


def _moe_combine(x, route, dest, ys, tm):
    t, d = x.shape
    return pl.pallas_call(
        functools.partial(_combine_kernel, tm=tm), grid=(t // tm,),
        in_specs=[pl.BlockSpec((1, 1, 2 * tm), lambda i: (i, 0, 0), memory_space=pltpu.SMEM),
                  pl.BlockSpec((tm, d), lambda i: (i, 0)), pl.BlockSpec((tm, LANE), lambda i: (i, 0)),
                  pl.BlockSpec(memory_space=pl.ANY)],
        out_specs=pl.BlockSpec((tm, d), lambda i: (i, 0)),
        out_shape=jax.ShapeDtypeStruct((t, d), F32),
        scratch_shapes=[pltpu.VMEM((tm, d), F32), pltpu.VMEM((tm, d), F32), pltpu.SemaphoreType.DMA(())],
        compiler_params=_params("arbitrary"), name="moe_combine",
    )(dest, x, route, ys)


def _moe(x, gain, w_router, b_router, w_gate, w_up, w_down, rb=512, tm=256):
    t, d = x.shape
    route, counts = _moe_router(x, gain, w_router, b_router)
    n_blk = (t * TOP_K + rb - 1) // rb + N_EXPERTS
    padded = (counts + rb - 1) // rb * rb
    pad_end = jnp.cumsum(padded)
    pad_start = pad_end - padded
    dest = pad_start[route[:, 0:2].astype(jnp.int32)] + route[:, 4:6].astype(jnp.int32)
    dest_t = jnp.concatenate([dest[:, 0].reshape(t // tm, 1, tm), dest[:, 1].reshape(t // tm, 1, tm)], axis=2)
    fill = jnp.concatenate([(pad_start + counts) // 8 * 8, pad_end[-1:], pad_end, jnp.full((1,), n_blk * rb, jnp.int32)])
    n_used = pad_end[-1:] // rb
    blk = jnp.minimum(jnp.arange(n_blk, dtype=jnp.int32), n_used - 1) * rb
    blk_expert = jnp.sum(pad_end[None, :] <= blk[:, None], axis=1).astype(jnp.int32)
    xs = _moe_scatter(x, gain, dest_t, fill.astype(jnp.int32), n_blk * rb, tm, rb)
    ys = _moe_experts(xs, blk_expert, n_used.astype(jnp.int32), w_gate, w_up, w_down, rb)
    return _moe_combine(x, route, dest_t, ys, tm)


def kernel(x, rel_bias, ev_mix_norm, ev_w_in, nsa_q_norm, nsa_k_norm, nsa_cmp_pos, nsa_cmp_w1, nsa_cmp_w2, mla_cq_norm, mla_ckv_norm, mla_w_uq, mla_w_ukv, mla_q_norm, mla_k_norm, ev_w_out, ev_ffn_norm, ffn_w_gate, ffn_w_up, ffn_w_down, od_mix_norm, od_w_in, conv_w, od_w_out, od_ffn_norm, moe_w_router, moe_b_router, moe_w_gate, moe_w_up, moe_w_down):
    bsz, seq, d = x.shape
    depth = ev_mix_norm.shape[0] + od_mix_norm.shape[0]
    pad = 512
    for layer in range(depth):
        i = layer // 2
        if layer % 2 == 0:
            qn, kv, kc_raw, vc_raw, gates, qm, km, vm = _even_in_proj(
                x, ev_mix_norm[i], ev_w_in[i], nsa_q_norm[i], nsa_k_norm[i], mla_cq_norm[i], mla_ckv_norm[i],
                mla_w_uq[i], mla_w_ukv[i], mla_q_norm[i], mla_k_norm[i], tm=pad)
            kc = _compress(kc_raw, nsa_cmp_pos[i, 0], nsa_cmp_w1[i, 0], nsa_cmp_w2[i, 0], nsa_k_norm[i, 0], True)
            vc = _compress(vc_raw, nsa_cmp_pos[i, 1], nsa_cmp_w1[i, 1], nsa_cmp_w2[i, 1], nsa_k_norm[i, 0], False)
            o_nsa = _nsa_attention(qn, kv, kc, vc, gates, rel_bias, pad)
            o_mla = _mla_attention(qm, km, vm)
            x = _even_out_ffn(x.reshape(bsz * seq, d), o_nsa.reshape(bsz * seq, -1), o_mla.reshape(bsz * seq, -1),
                              ev_w_out[i], ev_ffn_norm[i], ffn_w_gate[i], ffn_w_up[i], ffn_w_down[i]).reshape(bsz, seq, d)
        else:
            x = _conv_mixer(x, od_mix_norm[i], od_w_in[i], conv_w[i], od_w_out[i])
            x = _moe(x.reshape(bsz * seq, d), od_ffn_norm[i], moe_w_router[i], moe_b_router[i],
                     moe_w_gate[i], moe_w_up[i], moe_w_down[i]).reshape(bsz, seq, d)
    return x
```

```python
import functools
import math

import jax
import jax.numpy as jnp
import numpy as np
from jax import lax
from jax.experimental import pallas as pl
from jax.experimental.pallas import tpu as pltpu

F32 = jnp.float32
BF16 = jnp.bfloat16

EPS = 1e-6
NEG = -1e30
FORCED_SCORE = 1e9
NSA_HEADS = 8
NSA_GROUPS = 2
NSA_REP = NSA_HEADS // NSA_GROUPS
NSA_DK = 64
CMP_BLOCK = 32
CMP_STRIDE = 16
CMP_HIDDEN = 256
SEL_BLOCK = 64
SEL_TOP_N = 16
WINDOW = 512
MLA_HEADS = 8
MLA_Q_RANK = 256
MLA_KV_RANK = 128
MLA_NOPE = 64
MLA_ROPE = 32
MLA_V = 64
ROPE_THETA = 10000.0
REL_BUCKETS = 32
REL_MAX_DIST = 128
CONV_WIDTH = 3
N_EXPERTS = 8
TOP_K = 2
EVEN_IN_SIZES = (512,) + (128,) * 6 + (24, 256, 128, 32)

LANE = 128
VMEM_LIMIT = 56 * 1024 * 1024
SEL_MASK = -30000.0
FAR_DIST = 128

_Q0, _KV0, _KC0, _VC0, _GT0, _CQ0, _CKV0, _KR0, _EVEN_W = 0, 1024, 2048, 2176, 2304, 2560, 2816, 2944, 3072


def _dot(a, b):
    return jnp.dot(a, b, preferred_element_type=F32)


def _dot_nt(a, b):
    return lax.dot_general(a, b, (((1,), (1,)), ((), ())), preferred_element_type=F32)


def _rms(x, gain, n):
    ss = jnp.sum(x * x, axis=-1, keepdims=True) * (1.0 / n)
    return x * lax.rsqrt(ss + EPS) * gain


def _params(*sem):
    return pltpu.CompilerParams(dimension_semantics=sem, vmem_limit_bytes=VMEM_LIMIT)


def _const_spec(shape):
    nd = len(shape)
    return pl.BlockSpec(shape, lambda *_: (0,) * nd, pipeline_mode=pl.Buffered(1))


def _rope(x, cos, sa, sb):
    return x * cos + pltpu.roll(x, 16, 1) * sa + pltpu.roll(x, LANE - 16, 1) * sb


def _even_in_kernel(x_ref, gmix_ref, w_ref, gq_ref, gks_ref, gkw_ref, gcq_ref, gckv_ref,
                    wuq_ref, wuk_ref, wuv_ref, gmq_ref, gmk_ref, cos_ref, sa_ref, sb_ref,
                    qn_ref, kv_ref, kc_ref, vc_ref, gate_ref, qm_ref, km_ref, vm_ref, *, tm):
    j = pl.program_id(1)

    @pl.when(j == 0)
    def _():
        kv_ref[...] = jnp.zeros_like(kv_ref)
        qn_ref[...] = jnp.zeros_like(qn_ref)
        kc_ref[...] = jnp.zeros_like(kc_ref)
        vc_ref[...] = jnp.zeros_like(vc_ref)
        gate_ref[...] = jnp.zeros_like(gate_ref)
        qm_ref[...] = jnp.zeros_like(qm_ref)
        km_ref[...] = jnp.zeros_like(km_ref)
        vm_ref[...] = jnp.zeros_like(vm_ref)

    @pl.when(j > 0)
    def _():
        xn = _rms(x_ref[...], gmix_ref[...], x_ref.shape[-1]).astype(BF16)
        hq = _dot(xn, w_ref[:, _Q0:_Q0 + 1024])
        gq = gq_ref[...]
        for h in range(NSA_HEADS):
            seg = hq[:, LANE * h:LANE * (h + 1)]
            qn_ref[:, LANE * h:LANE * (h + 1)] = (_rms(seg, gq, NSA_DK) * NSA_DK ** -0.5).astype(BF16)
        hkv = _dot(xn, w_ref[:, _KV0:_KV0 + 1024])
        pos = (j - 1) * tm + lax.broadcasted_iota(jnp.int32, (tm, LANE), 0)
        lane = lax.broadcasted_iota(jnp.int32, (tm, LANE), 1)
        onehot = jnp.where(lane - NSA_DK == pos // SEL_BLOCK, 1.0, 0.0)
        for g in range(NSA_GROUPS):
            o = 4 * LANE * g
            kv_ref[:, o:o + LANE] = (_rms(hkv[:, o:o + LANE], gks_ref[...], NSA_DK) + onehot).astype(BF16)
            kv_ref[:, o + LANE:o + 2 * LANE] = hkv[:, o + LANE:o + 2 * LANE].astype(BF16)
            kv_ref[:, o + 2 * LANE:o + 3 * LANE] = _rms(hkv[:, o + 2 * LANE:o + 3 * LANE], gkw_ref[...],
                                                        NSA_DK).astype(BF16)
            kv_ref[:, o + 3 * LANE:o + 4 * LANE] = hkv[:, o + 3 * LANE:o + 4 * LANE].astype(BF16)
        hc = _dot(xn, w_ref[:, _KC0:_KC0 + 256])
        kc_ref[...] = hc[:, :LANE].astype(BF16)
        vc_ref[...] = hc[:, LANE:].astype(BF16)
        gate_ref[...] = jax.nn.sigmoid(_dot(xn, w_ref[:, _GT0:_GT0 + 256]))
        hm = _dot(xn, w_ref[:, _CQ0:_EVEN_W])
        cq = _rms(hm[:, :MLA_Q_RANK], gcq_ref[...], MLA_Q_RANK).astype(BF16)
        ckv = _rms(hm[:, MLA_Q_RANK:MLA_Q_RANK + MLA_KV_RANK], gckv_ref[...], MLA_KV_RANK).astype(BF16)
        k_rope = hm[:, MLA_Q_RANK + MLA_KV_RANK:]
        qm = _dot(cq, wuq_ref[...])
        kn = _dot(ckv, wuk_ref[...])
        vm_ref[...] = _dot(ckv, wuv_ref[...]).astype(BF16)
        cos, sa, sb = cos_ref[...], sa_ref[...], sb_ref[...]
        dqk = MLA_NOPE + MLA_ROPE
        for h in range(MLA_HEADS):
            sl = slice(LANE * h, LANE * (h + 1))
            qh = _rope(_rms(qm[:, sl], gmq_ref[...], dqk), cos, sa, sb) * dqk ** -0.5
            qm_ref[:, sl] = qh.astype(BF16)
            kh = _rope(_rms(kn[:, sl] + k_rope, gmk_ref[...], dqk), cos, sa, sb)
            km_ref[:, sl] = kh.astype(BF16)


def _even_in_weights(w_in):
    offs = np.concatenate([[0], np.cumsum(EVEN_IN_SIZES)])
    part = [w_in[:, offs[n]:offs[n + 1]] for n in range(len(EVEN_IN_SIZES))]
    q, k_c, v_c, k_s, v_s, k_w, v_w, gates, c_q, c_kv, k_rope = part
    d = w_in.shape[0]
    z = lambda n: jnp.zeros((d, n), w_in.dtype)
    cols = []
    for h in range(NSA_HEADS):
        cols += [q[:, 64 * h:64 * h + 64], z(64)]
    for g in range(NSA_GROUPS):
        s = slice(64 * g, 64 * g + 64)
        cols += [k_s[:, s], z(64), v_s[:, s], v_s[:, s], k_w[:, s], z(64), v_w[:, s], v_w[:, s]]
    cols += [k_c, v_c]
    for g in range(NSA_GROUPS):
        cols += [gates[:, 12 * g:12 * g + 12], z(LANE - 12)]
    cols += [c_q, c_kv, z(64), k_rope, z(32)]
    w = jnp.concatenate(cols, axis=1)
    assert w.shape[1] == _EVEN_W
    return w.astype(BF16)


def _pad_gain(g, width):
    return jnp.pad(g.astype(F32), (0, width - g.shape[0]))[None, :]


def _rope_tables(seq):
    half = MLA_ROPE // 2
    inv_freq = ROPE_THETA ** (-jnp.arange(half, dtype=F32) / half)
    ang = jnp.arange(seq).astype(F32)[:, None] * inv_freq[None, :]
    cos, sin = jnp.cos(ang), jnp.sin(ang)
    one = jnp.ones((seq, MLA_NOPE), F32)
    zn = jnp.zeros((seq, MLA_NOPE), F32)
    zt = jnp.zeros((seq, LANE - MLA_NOPE - MLA_ROPE), F32)
    zh = jnp.zeros((seq, half), F32)
    cos_t = jnp.concatenate([one, cos, cos, zt + 1.0], axis=1)
    sa = jnp.concatenate([zn, zh, sin, zt], axis=1)
    sb = jnp.concatenate([zn, -sin, zh, zt], axis=1)
    return cos_t, sa, sb


def _even_in_proj(x, gmix, w_in, q_norm, k_norm, cq_norm, ckv_norm, w_uq, w_ukv, mq_norm, mk_norm, tm=512):
    bsz, seq, d = x.shape
    nt = seq // tm
    w = _even_in_weights(w_in)
    dqk = MLA_NOPE + MLA_ROPE
    wuq = jnp.concatenate(
        [jnp.pad(w_uq[:, dqk * h:dqk * (h + 1)], ((0, 0), (0, LANE - dqk))) for h in range(MLA_HEADS)],
        axis=1).astype(BF16)
    kvw = MLA_NOPE + MLA_V
    wuk = jnp.concatenate(
        [jnp.pad(w_ukv[:, kvw * h:kvw * h + MLA_NOPE], ((0, 0), (0, LANE - MLA_NOPE))) for h in range(MLA_HEADS)],
        axis=1).astype(BF16)
    wuv = jnp.concatenate([w_ukv[:, kvw * h + MLA_NOPE:kvw * (h + 1)] for h in range(MLA_HEADS)], axis=1).astype(BF16)
    cos_t, sa, sb = _rope_tables(seq)
    tok = lambda width: pl.BlockSpec((None, tm, width), lambda b, j: (b, jnp.maximum(j - 1, 0), 0))
    postab = pl.BlockSpec((tm, LANE), lambda b, j: (jnp.maximum(j - 1, 0), 0))
    in_specs = [tok(d), _const_spec((1, d)), _const_spec((d, _EVEN_W)),
                _const_spec((1, LANE)), _const_spec((1, LANE)), _const_spec((1, LANE)),
                _const_spec((1, MLA_Q_RANK)), _const_spec((1, MLA_KV_RANK)),
                _const_spec(wuq.shape), _const_spec(wuk.shape), _const_spec(wuv.shape),
                _const_spec((1, LANE)), _const_spec((1, LANE)), postab, postab, postab]
    out_shape = [jax.ShapeDtypeStruct((bsz, seq, 1024), BF16),
                 jax.ShapeDtypeStruct((bsz, seq + tm, 1024), BF16),
                 jax.ShapeDtypeStruct((bsz, seq, LANE), BF16),
                 jax.ShapeDtypeStruct((bsz, seq, LANE), BF16),
                 jax.ShapeDtypeStruct((bsz, seq, 2 * LANE), F32),
                 jax.ShapeDtypeStruct((bsz, seq, 1024), BF16),
                 jax.ShapeDtypeStruct((bsz, seq, 1024), BF16),
                 jax.ShapeDtypeStruct((bsz, seq, 512), BF16)]
    out_specs = [tok(1024), pl.BlockSpec((None, tm, 1024), lambda b, j: (b, j, 0)), tok(LANE), tok(LANE),
                 tok(2 * LANE), tok(1024), tok(1024), tok(512)]
    return pl.pallas_call(
        functools.partial(_even_in_kernel, tm=tm), grid=(bsz, nt + 1), in_specs=in_specs, out_specs=out_specs,
        out_shape=out_shape, compiler_params=_params("parallel", "arbitrary"), name="even_in_proj",
    )(x, gmix[None, :], w, _pad_gain(q_norm, LANE), _pad_gain(k_norm[1], LANE), _pad_gain(k_norm[2], LANE),
      cq_norm[None, :], ckv_norm[None, :], wuq, wuk, wuv, _pad_gain(mq_norm, LANE), _pad_gain(mk_norm, LANE),
      cos_t, sa, sb)


def _cmp_kernel(x_ref, wa_ref, wb_ref, pa_ref, pb_ref, w2_ref, gain_ref, o_ref, *, normalize):
    x = x_ref[...]
    ua = _dot(x, wa_ref[...])
    ub = _dot(x, wb_ref[...])
    pt = _dot(pa_ref[...], wa_ref[...]) + _dot(pb_ref[...], wb_ref[...])
    n = ub.shape[0]
    pre = ua + pltpu.roll(ub, n - 1, 0) + pt[0:1]
    hid = jax.nn.gelu(pre).astype(BF16)
    out = _dot(hid, w2_ref[...])
    if normalize:
        gain = gain_ref[...]
        for g in range(NSA_GROUPS):
            sl = slice(LANE * g, LANE * (g + 1))
            o_ref[:, sl] = _rms(out[:, sl], gain, NSA_DK).astype(BF16)
    else:
        o_ref[...] = out.astype(BF16)


def _compress(kv, pos_emb, w1, w2, gain, normalize):
    bsz, seq, _ = kv.shape
    half = CMP_BLOCK // 2
    assert CMP_STRIDE == half
    nrow = seq // CMP_STRIDE
    x = kv.reshape(bsz, nrow, CMP_STRIDE * LANE)
    w1r = w1.reshape(CMP_BLOCK, NSA_DK, CMP_HIDDEN)
    eye = jnp.eye(NSA_GROUPS, dtype=w1.dtype)
    widen = lambda w: jnp.einsum("ldn,gh->lgdhn", w, eye).reshape(half * LANE, NSA_GROUPS * CMP_HIDDEN).astype(BF16)
    wa, wb = widen(w1r[:half]), widen(w1r[half:])
    prow = lambda p: jnp.pad(jnp.broadcast_to(p[:, None, :], (half, NSA_GROUPS, NSA_DK)).reshape(1, half * LANE),
                             ((0, 15), (0, 0))).astype(BF16)
    pa, pb = prow(pos_emb[:half]), prow(pos_emb[half:])
    second = jnp.zeros_like(w2) if normalize else w2
    w2w = jnp.einsum("nd,gh->gnhd", jnp.concatenate([w2, second], axis=1), eye)
    w2w = w2w.reshape(NSA_GROUPS * CMP_HIDDEN, NSA_GROUPS * LANE).astype(BF16)
    return pl.pallas_call(
        functools.partial(_cmp_kernel, normalize=normalize), grid=(bsz,),
        in_specs=[pl.BlockSpec((None, nrow, CMP_STRIDE * LANE), lambda b: (b, 0, 0)),
                  _const_spec(wa.shape), _const_spec(wb.shape), _const_spec(pa.shape), _const_spec(pb.shape),
                  _const_spec(w2w.shape), _const_spec((1, LANE))],
        out_specs=pl.BlockSpec((None, nrow, NSA_GROUPS * LANE), lambda b: (b, 0, 0)),
        out_shape=jax.ShapeDtypeStruct((bsz, nrow, NSA_GROUPS * LANE), BF16),
        compiler_params=_params("parallel"), name="nsa_compress",
    )(x, wa, wb, pa, pb, w2w, _pad_gain(gain, LANE))


def _bucket_table():
    dist = np.arange(FAR_DIST + 1)
    max_exact = REL_BUCKETS // 2
    nf = np.maximum(dist, max_exact).astype(np.float32)
    large = max_exact + (np.log(nf / max_exact) / math.log(REL_MAX_DIST / max_exact)
                         * (REL_BUCKETS - max_exact)).astype(np.int32)
    return np.where(dist < max_exact, dist, np.minimum(large, REL_BUCKETS - 1))


CMP_NEAR = 32


def _expand(tbl, idx):
    idx = np.asarray(idx)
    onehot = jnp.asarray(np.eye(tbl.shape[1], dtype=np.float32)[idx.reshape(-1)])
    out = lax.dot_general(tbl, onehot, (((1,), (1,)), ((), ())), precision=lax.Precision.HIGHEST)
    return out.reshape((tbl.shape[0],) + idx.shape)


def _toeplitz_tile(tbl, far, width, dist0, valid):
    period = width + SEL_BLOCK + 1
    j = np.arange(period)
    dist = dist0 - np.where(j < width, j, j - period)
    gen = jnp.where(jnp.asarray(valid(dist))[None], _expand(tbl, np.clip(dist, 0, FAR_DIST)) - far, NEG)
    flat = jnp.tile(gen, (1, SEL_BLOCK))[:, :SEL_BLOCK * (period - 1)]
    return flat.reshape(tbl.shape[0], SEL_BLOCK, period - 1)[:, :, :width]


def _bias_tables(rel_bias):
    tbl = _expand(rel_bias.astype(F32).T, _bucket_table())
    far = tbl[:, FAR_DIST:]
    bias_near = _toeplitz_tile(tbl, far, 4 * SEL_BLOCK, 3 * SEL_BLOCK, lambda d: d >= 0)
    bias_win = _toeplitz_tile(tbl, 0.0, WINDOW + SEL_BLOCK, WINDOW, lambda d: (d >= 0) & (d < WINDOW))
    d_c = (np.arange(SEL_BLOCK)[:, None] + (CMP_NEAR // 2) * CMP_STRIDE - (CMP_BLOCK - 1)
           - CMP_STRIDE * np.arange(CMP_NEAR)[None, :])
    assert d_c[:, 0].min() >= FAR_DIST and d_c[:, -1].max() < 0
    bias_cmp = jnp.where(jnp.asarray(d_c >= 0)[None], _expand(tbl, np.clip(d_c, 0, FAR_DIST)) - far[:, :, None], 0.0)
    return bias_cmp, bias_near, bias_win


def _cmp_mask_dist(seq):
    a = np.tile(np.arange(SEL_BLOCK), NSA_REP)[:, None]
    c = np.arange(seq // CMP_STRIDE)[None, :]
    return jnp.asarray(c * CMP_STRIDE + CMP_BLOCK - 1 - a, jnp.int32)


def _overlap_t(seq):
    ncp = seq // CMP_STRIDE
    cs = np.arange(ncp)[None, :] * CMP_STRIDE
    ss = np.arange(SEL_BLOCK)[:, None] * SEL_BLOCK
    ov = (cs < ss + SEL_BLOCK) & (cs + CMP_BLOCK - 1 >= ss) & (np.arange(ncp)[None, :] < (seq - CMP_BLOCK) // CMP_STRIDE + 1)
    return jnp.asarray(ov, BF16)


def _softmax_step(s, v, state):
    m, l, acc = state
    mn = jnp.maximum(m, jnp.max(s, axis=-1, keepdims=True))
    alpha = jnp.exp(m - mn)
    p = jnp.exp(s - mn)
    return mn, alpha * l + jnp.sum(p, axis=-1, keepdims=True), alpha * acc + _dot(p.astype(BF16), v)


FAR_CHUNK = 1024


def _lane_fold(x, op):
    out = x[:, 0:LANE]
    for c in range(1, x.shape[1] // LANE):
        out = op(out, x[:, LANE * c:LANE * (c + 1)])
    return out


def _nsa_kernel(q_ref, kv_ref, kc_ref, vc_ref, g_ref, bc_ref, bn_ref, bw_ref, ovt_ref, dmask_ref, o_ref, s_ref, *, pad):
    i = pl.program_id(1)
    qb, rep, ng = SEL_BLOCK, NSA_REP, NSA_GROUPS
    rows = qb * rep
    gw = 4 * LANE
    ncp = kc_ref.shape[0]
    qs = [jnp.concatenate([q_ref[:, gw * g + LANE * r:gw * g + LANE * (r + 1)] for r in range(rep)], axis=0)
          for g in range(ng)]

    m_i = lax.broadcasted_iota(jnp.int32, (CMP_NEAR, ncp), 0)
    c_i = lax.broadcasted_iota(jnp.int32, (CMP_NEAR, ncp), 1)
    shift = jnp.where(c_i - m_i == (qb // CMP_STRIDE) * i - CMP_NEAR // 2, 1.0, 0.0).astype(BF16)
    visible = dmask_ref[...] <= i * qb
    o_c, psums = [], []
    for g in range(ng):
        base = bc_ref[rep * g:rep * (g + 1)].reshape(rows, CMP_NEAR)
        b_hi = base.astype(BF16)
        b_lo = (base - b_hi.astype(F32)).astype(BF16)
        s = _dot_nt(qs[g], kc_ref[:, LANE * g:LANE * (g + 1)]) + _dot(b_hi, shift) + _dot(b_lo, shift)
        s = jnp.where(visible, s, NEG)
        m = jnp.maximum(jnp.max(s, axis=-1, keepdims=True), -1e20)
        e = jnp.exp(s - m)
        inv = 1.0 / jnp.maximum(jnp.sum(e, axis=-1, keepdims=True), 1e-30)
        o_c.append(_dot(e.astype(BF16), vc_ref[:, LANE * g:LANE * (g + 1)]) * inv)
        p = e * inv
        psums.append(p[0:qb] + p[qb:2 * qb] + p[2 * qb:3 * qb] + p[3 * qb:4 * qb])

    psum = jnp.concatenate(psums, axis=0)
    hi = psum.astype(BF16)
    r1 = psum - hi.astype(F32)
    mid = r1.astype(BF16)
    lo = (r1 - mid.astype(F32)).astype(BF16)
    ovt = ovt_ref[...]
    imp = _dot_nt(ovt, hi) + _dot_nt(ovt, mid) + _dot_nt(ovt, lo)
    jj = lax.broadcasted_iota(jnp.int32, (qb, ng * qb), 0)
    imp = jnp.where((jj == i) | (jj == 0), FORCED_SCORE, imp)
    imp = jnp.where(jj > i, -1.0, imp)
    grp = [imp[8 * a:8 * a + 8] for a in range(8)]
    cnt = [jnp.zeros((8, ng * qb), F32) for _ in range(8)]
    sub = lax.broadcasted_iota(jnp.int32, (8, ng * qb), 0)
    for k in range(qb):
        rk = imp[k:k + 1, :]
        for a in range(8):
            if 8 * a + 7 <= k:
                cnt[a] = cnt[a] + jnp.where(rk > grp[a], 1.0, 0.0)
            elif 8 * a > k:
                cnt[a] = cnt[a] + jnp.where(rk >= grp[a], 1.0, 0.0)
            else:
                cnt[a] = cnt[a] + jnp.where(sub + 8 * a > k, jnp.where(rk >= grp[a], 1.0, 0.0),
                                            jnp.where(rk > grp[a], 1.0, 0.0))
    neg = jnp.where(jnp.concatenate(cnt, axis=0) < SEL_TOP_N, 0.0, SEL_MASK)
    neg_far = jnp.where(jj >= i - 3, SEL_MASK, neg)
    zero = jnp.zeros((qb, ng * qb), BF16)
    ext_near = jnp.concatenate([zero, neg.astype(BF16)], axis=0)
    ext_far = jnp.concatenate([zero, neg_far.astype(BF16)], axis=0)
    q_row = lax.broadcasted_iota(jnp.int32, (qb, ng * qb), 0)
    q_lane = lax.broadcasted_iota(jnp.int32, (qb, ng * qb), 1)
    q_near, q_far = [], []
    for g in range(ng):
        pick = jnp.where(q_lane == q_row + qb * g, 1.0, 0.0).astype(BF16)
        q_near.append(qs[g] + jnp.concatenate([_dot_nt(pick, ext_near).astype(BF16)] * rep, axis=0))
        q_far.append(qs[g] + jnp.concatenate([_dot_nt(pick, ext_far).astype(BF16)] * rep, axis=0))

    st_n = pl.multiple_of(pad + (i - 3) * qb, qb)
    in_seq = lax.broadcasted_iota(jnp.int32, (rows, 4 * qb), 1) + (i - 3) * qb >= 0
    s_near = [jnp.where(in_seq, _dot_nt(q_near[g], kv_ref[pl.ds(st_n, 4 * qb), gw * g:gw * g + LANE])
                        + bn_ref[rep * g:rep * (g + 1)].reshape(rows, 4 * qb), NEG) for g in range(ng)]

    nch = (jnp.maximum(i - 3, 0) * qb + FAR_CHUNK - 1) // FAR_CHUNK

    def pass1(c, mx):
        st = pl.multiple_of(pad + c * FAR_CHUNK, FAR_CHUNK)
        col = pl.multiple_of(c * FAR_CHUNK, FAR_CHUNK)
        out = []
        for g in range(ng):
            s = _dot_nt(q_far[g], kv_ref[pl.ds(st, FAR_CHUNK), gw * g:gw * g + LANE])
            s_ref[g, :, pl.ds(col, FAR_CHUNK)] = s
            out.append(jnp.maximum(mx[g], _lane_fold(s, jnp.maximum)))
        return tuple(out)

    mx = lax.fori_loop(0, nch, pass1, tuple(jnp.full((rows, LANE), NEG, F32) for _ in range(ng)))
    ms = [jnp.max(jnp.maximum(mx[g], _lane_fold(s_near[g], jnp.maximum)), axis=-1, keepdims=True) for g in range(ng)]

    def pass2(c, state):
        st = pl.multiple_of(pad + c * FAR_CHUNK, FAR_CHUNK)
        col = pl.multiple_of(c * FAR_CHUNK, FAR_CHUNK)
        out = []
        for g in range(ng):
            l, acc = state[g]
            p = jnp.exp(s_ref[g, :, pl.ds(col, FAR_CHUNK)] - ms[g])
            pv = _dot(p.astype(BF16), kv_ref[pl.ds(st, FAR_CHUNK), gw * g + LANE:gw * g + 2 * LANE])
            out.append((l + _lane_fold(p, jnp.add), acc + pv))
        return tuple(out)

    zero_state = (jnp.zeros((rows, LANE), F32), jnp.zeros((rows, LANE), F32))
    far = lax.fori_loop(0, nch, pass2, tuple(zero_state for _ in range(ng)))
    o_s = []
    for g in range(ng):
        p = jnp.exp(s_near[g] - ms[g])
        l = jnp.sum(far[g][0] + _lane_fold(p, jnp.add), axis=-1, keepdims=True)
        pv = _dot(p.astype(BF16), kv_ref[pl.ds(st_n, 4 * qb), gw * g + LANE:gw * g + 2 * LANE])
        o_s.append((pv + far[g][1]) / l)

    wk = WINDOW + qb
    st_w = pl.multiple_of(pad - WINDOW + i * qb, qb)
    in_seq = lax.broadcasted_iota(jnp.int32, (rows, wk), 1) + i * qb - WINDOW >= 0
    o_w = []
    for g in range(ng):
        s = _dot_nt(qs[g], kv_ref[pl.ds(st_w, wk), gw * g + 2 * LANE:gw * g + 3 * LANE])
        s = jnp.where(in_seq, s + bw_ref[rep * g:rep * (g + 1)].reshape(rows, wk), NEG)
        e = jnp.exp(s - jnp.max(s, axis=-1, keepdims=True))
        pv = _dot(e.astype(BF16), kv_ref[pl.ds(st_w, wk), gw * g + 3 * LANE:gw * g + 4 * LANE])
        o_w.append(pv / jnp.sum(e, axis=-1, keepdims=True))

    gates = g_ref[...]
    low = lax.broadcasted_iota(jnp.int32, (qb, LANE), 1) < NSA_DK
    for g in range(ng):
        outs = []
        for r in range(rep):
            sl = slice(qb * r, qb * (r + 1))
            c0 = LANE * g + 3 * r
            outs.append(gates[:, c0:c0 + 1] * o_c[g][sl] + gates[:, c0 + 1:c0 + 2] * o_s[g][sl]
                        + gates[:, c0 + 2:c0 + 3] * o_w[g][sl])
        for pr in range(rep // 2):
            c0 = 2 * LANE * g + LANE * pr
            o_ref[:, c0:c0 + LANE] = jnp.where(low, outs[2 * pr], outs[2 * pr + 1]).astype(o_ref.dtype)


def _nsa_attention(qn, kv, kc, vc, gates, rel_bias, pad):
    bsz, seq, _ = qn.shape
    qb, ng = SEL_BLOCK, NSA_GROUPS
    ncp = seq // CMP_STRIDE
    assert seq // qb <= qb and pad >= WINDOW and seq % FAR_CHUNK == 0
    bias_cmp, bias_near, bias_win = _bias_tables(rel_bias)
    ovt = _overlap_t(seq)
    dmask = _cmp_mask_dist(seq)
    tok = lambda width: pl.BlockSpec((None, qb, width), lambda b, i: (b, i, 0))
    per_b = lambda n, width: pl.BlockSpec((None, n, width), lambda b, i: (b, 0, 0))
    return pl.pallas_call(
        functools.partial(_nsa_kernel, pad=pad), grid=(bsz, seq // qb),
        in_specs=[tok(ng * 4 * LANE), per_b(seq + pad, ng * 4 * LANE), per_b(ncp, ng * LANE), per_b(ncp, ng * LANE),
                  tok(ng * LANE), _const_spec(bias_cmp.shape), _const_spec(bias_near.shape),
                  _const_spec(bias_win.shape), _const_spec(ovt.shape), _const_spec(dmask.shape)],
        out_specs=tok(NSA_HEADS * NSA_DK),
        out_shape=jax.ShapeDtypeStruct((bsz, seq, NSA_HEADS * NSA_DK), BF16),
        scratch_shapes=[pltpu.VMEM((ng, NSA_REP * qb, seq), F32)],
        compiler_params=_params("parallel", "arbitrary"), name="nsa_attention",
    )(qn, kv, kc, vc, gates, bias_cmp, bias_near, bias_win, ovt, dmask)


def _mla_attn_kernel(q_ref, k_ref, v_ref, o_ref, *, tq, ck):
    i = pl.program_id(2)
    heads = [slice(LANE * hh, LANE * (hh + 1)) for hh in range(2)]
    qs = [q_ref[:, sl] for sl in heads]

    def body(c, states):
        st = pl.multiple_of(c * ck, ck)
        v = v_ref[pl.ds(st, ck), :]
        return tuple(_softmax_step(_dot_nt(q, k_ref[pl.ds(st, ck), sl]), v, state)
                     for q, sl, state in zip(qs, heads, states))

    init = (jnp.full((tq, 1), NEG, F32), jnp.zeros((tq, 1), F32), jnp.zeros((tq, LANE), F32))
    states = lax.fori_loop(0, i * (tq // ck), body, (init, init))
    st = pl.multiple_of(i * tq, tq)
    causal = lax.broadcasted_iota(jnp.int32, (tq, tq), 0) >= lax.broadcasted_iota(jnp.int32, (tq, tq), 1)
    outs = []
    for q, sl, state in zip(qs, heads, states):
        s = _dot_nt(q, k_ref[pl.ds(st, tq), sl])
        _, l, acc = _softmax_step(jnp.where(causal, s, NEG), v_ref[pl.ds(st, tq), :], state)
        outs.append(acc / l)
    low = lax.broadcasted_iota(jnp.int32, (tq, LANE), 1) < MLA_V
    o_ref[...] = jnp.where(low, outs[0], outs[1]).astype(o_ref.dtype)


def _mla_attention(qm, km, vm, tq=1024, ck=512):
    bsz, seq, _ = qm.shape
    tq = min(tq, seq)
    return pl.pallas_call(
        functools.partial(_mla_attn_kernel, tq=tq, ck=min(ck, tq)), grid=(bsz, MLA_HEADS // 2, seq // tq),
        in_specs=[pl.BlockSpec((None, tq, 2 * LANE), lambda b, h, i: (b, i, h)),
                  pl.BlockSpec((None, seq, 2 * LANE), lambda b, h, i: (b, 0, h)),
                  pl.BlockSpec((None, seq, LANE), lambda b, h, i: (b, 0, h))],
        out_specs=pl.BlockSpec((None, tq, LANE), lambda b, h, i: (b, i, h)),
        out_shape=jax.ShapeDtypeStruct((bsz, seq, MLA_HEADS * MLA_V), BF16),
        compiler_params=_params("parallel", "parallel", "arbitrary"), name="mla_attention",
    )(qm, km, vm)


def _even_out_kernel(x_ref, on_ref, om_ref, wn_ref, wm_ref, g_ref, wg_ref, wu_ref, wd_ref, o_ref, *, chunk):
    x1 = x_ref[...] + _dot(on_ref[...], wn_ref[...]) + _dot(om_ref[...], wm_ref[...])
    n = _rms(x1, g_ref[...], x1.shape[-1]).astype(BF16)
    ffn = None
    for f0 in range(0, wg_ref.shape[1], chunk):
        gate = _dot(n, wg_ref[:, f0:f0 + chunk])
        act = (gate * jax.nn.sigmoid(gate) * _dot(n, wu_ref[:, f0:f0 + chunk])).astype(BF16)
        part = _dot(act, wd_ref[f0:f0 + chunk, :])
        ffn = part if ffn is None else ffn + part
    o_ref[...] = x1 + ffn


def _even_out_ffn(x, o_nsa, o_mla, w_out, gain, w_gate, w_up, w_down, tm=512):
    t, d = x.shape
    dff = w_gate.shape[1]
    wn, wm = w_out[:o_nsa.shape[1]].astype(BF16), w_out[o_nsa.shape[1]:].astype(BF16)
    row = lambda width: pl.BlockSpec((tm, width), lambda i: (i, 0))
    return pl.pallas_call(
        functools.partial(_even_out_kernel, chunk=dff // 2), grid=(t // tm,),
        in_specs=[row(d), row(o_nsa.shape[1]), row(o_mla.shape[1]), _const_spec(wn.shape), _const_spec(wm.shape),
                  _const_spec((1, d)), _const_spec((d, dff)), _const_spec((d, dff)), _const_spec((dff, d))],
        out_specs=row(d), out_shape=jax.ShapeDtypeStruct((t, d), F32),
        compiler_params=_params("parallel"), name="even_out_ffn",
    )(x, o_nsa, o_mla, wn, wm, gain[None, :], w_gate.astype(BF16), w_up.astype(BF16), w_down.astype(BF16))


def _conv_kernel(x_ref, g_ref, win_ref, cw_ref, wout_ref, o_ref, vbuf_ref, *, tm):
    j = pl.program_id(1)
    x = x_ref[...]
    d = x.shape[-1]
    n = _rms(x, g_ref[...], d).astype(BF16)
    b_gate = _dot(n, win_ref[:, 0:d])
    v = _dot(n, win_ref[:, d:2 * d]) * _dot(n, win_ref[:, 2 * d:3 * d])

    @pl.when(j == 0)
    def _():
        vbuf_ref[0:8, :] = jnp.zeros((8, d), F32)

    vbuf_ref[8:8 + tm, :] = v
    cw = cw_ref[...]
    y = cw[2:3] * v + cw[1:2] * vbuf_ref[7:7 + tm, :] + cw[0:1] * vbuf_ref[6:6 + tm, :]
    vbuf_ref[0:8, :] = v[tm - 8:tm]
    o_ref[...] = x + _dot((b_gate * y).astype(BF16), wout_ref[...])


def _conv_mixer(x, gain, w_in, conv_w, w_out, tm=512):
    bsz, seq, d = x.shape
    cw = jnp.pad(conv_w.astype(F32), ((0, 8 - CONV_WIDTH), (0, 0)))
    tok = pl.BlockSpec((None, tm, d), lambda b, j: (b, j, 0))
    return pl.pallas_call(
        functools.partial(_conv_kernel, tm=tm), grid=(bsz, seq // tm),
        in_specs=[tok, _const_spec((1, d)), _const_spec((d, 3 * d)), _const_spec((8, d)), _const_spec((d, d))],
        out_specs=tok, out_shape=jax.ShapeDtypeStruct((bsz, seq, d), F32),
        scratch_shapes=[pltpu.VMEM((tm + 8, d), F32)],
        compiler_params=_params("parallel", "arbitrary"), name="conv_mixer",
    )(x, gain[None, :], w_in.astype(BF16), cw, w_out.astype(BF16))


def _router_kernel(x_ref, g_ref, wr_ref, br_ref, tri_ref, route_ref, cnt_ref, carry_ref, *, tm):
    t = pl.program_id(0)

    @pl.when(t == 0)
    def _():
        carry_ref[...] = jnp.zeros_like(carry_ref)

    n = _rms(x_ref[...], g_ref[...], x_ref.shape[-1])
    hi = n.astype(BF16)
    lo = (n - hi.astype(F32)).astype(BF16)
    whi, wlo = wr_ref[0], wr_ref[1]
    logits = _dot(hi, whi) + _dot(lo, whi) + _dot(hi, wlo) + br_ref[...]
    lane = lax.broadcasted_iota(jnp.int32, (tm, LANE), 1).astype(F32)
    big = float(LANE)
    m1 = jnp.max(logits, axis=-1, keepdims=True)
    e1 = jnp.min(jnp.where(logits == m1, lane, big), axis=-1, keepdims=True)
    rest = jnp.where(lane == e1, NEG, logits)
    m2 = jnp.max(rest, axis=-1, keepdims=True)
    e2 = jnp.min(jnp.where(rest == m2, lane, big), axis=-1, keepdims=True)
    z = jnp.exp(m2 - m1)
    w1 = 1.0 / (1.0 + z)
    w2 = z / (1.0 + z)
    oh1 = jnp.where(lane == e1, 1.0, 0.0)
    oh2 = jnp.where(lane == e2, 1.0, 0.0)
    both = oh1 + oh2
    before = _dot(tri_ref[...], both.astype(BF16)) + carry_ref[0:1, :]
    r1 = jnp.sum(oh1 * before, axis=-1, keepdims=True)
    r2 = jnp.sum(oh2 * before, axis=-1, keepdims=True)
    cols = [e1, e2, w1, w2, r1, r2]
    out = jnp.zeros((tm, LANE), F32)
    for c, val in enumerate(cols):
        out = jnp.where(lane == c, val, out)
    route_ref[...] = out
    carry_ref[0:1, :] = carry_ref[0:1, :] + jnp.sum(both, axis=0, keepdims=True)
    cnt_ref[...] = carry_ref[...]


def _moe_router(x, gain, w_router, b_router, tm=512):
    t, d = x.shape
    wr = jnp.pad(w_router.astype(F32), ((0, 0), (0, LANE - N_EXPERTS)))
    whi = wr.astype(BF16)
    wlo = (wr - whi.astype(F32)).astype(BF16)
    br = jnp.concatenate([b_router.astype(F32), jnp.full((LANE - N_EXPERTS,), NEG, F32)])[None, :]
    tri = jnp.asarray(np.tril(np.ones((tm, tm), np.float32), -1), BF16)
    route, cnt = pl.pallas_call(
        functools.partial(_router_kernel, tm=tm), grid=(t // tm,),
        in_specs=[pl.BlockSpec((tm, d), lambda i: (i, 0)), _const_spec((1, d)), _const_spec((2, d, LANE)),
                  _const_spec((1, LANE)), _const_spec((tm, tm))],
        out_specs=[pl.BlockSpec((tm, LANE), lambda i: (i, 0)), pl.BlockSpec((8, LANE), lambda i: (0, 0))],
        out_shape=[jax.ShapeDtypeStruct((t, LANE), F32), jax.ShapeDtypeStruct((8, LANE), F32)],
        scratch_shapes=[pltpu.VMEM((8, LANE), F32)],
        compiler_params=_params("arbitrary"), name="moe_router",
    )(x, gain[None, :], jnp.stack([whi, wlo]), br, tri)
    return route, cnt[0, :N_EXPERTS].astype(jnp.int32)


def _row_copy(src, i, dst, j, sem):
    return pltpu.make_async_copy(src.at[pl.ds(i, 1)], dst.at[pl.ds(j, 1)], sem)


def _scatter_kernel(fill_ref, dest_ref, x_ref, g_ref, xs_ref, xn_ref, zero_ref, sems, *, tm, rb):
    @pl.when(pl.program_id(0) == 0)
    def _():
        sem = sems.at[0]
        zero_ref[...] = jnp.zeros_like(zero_ref)
        sizes = [rb >> s for s in range(rb.bit_length() - 3)]
        for e in range(N_EXPERTS + 1):
            lo, n = fill_ref[e], fill_ref[N_EXPERTS + 1 + e] - fill_ref[e]
            whole = n // rb

            def copy(off, size):
                return pltpu.make_async_copy(zero_ref.at[pl.ds(0, size)],
                                             xs_ref.at[pl.ds(pl.multiple_of(off, 8), size)], sem)

            def blocks(k, c, lo=lo, copy=copy):
                copy(lo + k * rb, rb).start()
                copy(lo + k * rb, rb).wait()
                return c

            lax.fori_loop(0, whole, blocks, 0)
            off = lo + whole * rb
            for size in sizes[1:]:
                @pl.when((n & size) != 0)
                def _(off=off, size=size, copy=copy):
                    copy(off, size).start()
                    copy(off, size).wait()
                off = off + (n & size)

    t = pl.program_id(0)
    slot = t % 2
    rows, row_sem = xn_ref.at[slot], sems.at[slot]
    rows[...] = _rms(x_ref[...], g_ref[...], x_ref.shape[-1])

    def start(r, c):
        _row_copy(rows, r, xs_ref, dest_ref[0, 0, r], row_sem).start()
        _row_copy(rows, r, xs_ref, dest_ref[0, 0, tm + r], row_sem).start()
        return c

    lax.fori_loop(0, tm, start, 0)

    def drain(s):
        for _ in range(TOP_K):
            pltpu.make_async_copy(xn_ref.at[s], xs_ref.at[pl.ds(0, tm)], sems.at[s]).wait()

    @pl.when(t > 0)
    def _():
        drain(1 - slot)

    @pl.when(t == pl.num_programs(0) - 1)
    def _():
        drain(slot)


def _moe_scatter(x, gain, dest, fill, n_rows, tm, rb):
    t, d = x.shape
    nt = t // tm
    grid_spec = pltpu.PrefetchScalarGridSpec(
        num_scalar_prefetch=1, grid=(nt,),
        in_specs=[pl.BlockSpec((1, 1, 2 * tm), lambda i, f: (i, 0, 0), memory_space=pltpu.SMEM),
                  pl.BlockSpec((tm, d), lambda i, f: (i, 0)),
                  pl.BlockSpec((1, d), lambda i, f: (0, 0))],
        out_specs=pl.BlockSpec(memory_space=pl.ANY),
        scratch_shapes=[pltpu.VMEM((2, tm, d), F32), pltpu.VMEM((rb, d), F32), pltpu.SemaphoreType.DMA((2,))])
    return pl.pallas_call(
        functools.partial(_scatter_kernel, tm=tm, rb=rb), grid_spec=grid_spec,
        out_shape=jax.ShapeDtypeStruct((n_rows, d), F32),
        compiler_params=_params("arbitrary"), name="moe_scatter",
    )(fill, dest, x, gain[None, :])


def _expert_kernel(be_ref, nu_ref, x_ref, wg_ref, wu_ref, wd_ref, o_ref):
    used = pl.program_id(0) < nu_ref[0]

    @pl.when(used)
    def _():
        xb = x_ref[...].astype(BF16)
        gate = _dot(xb, wg_ref[...])
        act = (gate * jax.nn.sigmoid(gate) * _dot(xb, wu_ref[...])).astype(BF16)
        o_ref[...] = _dot(act, wd_ref[...])

    @pl.when(jnp.logical_not(used))
    def _():
        o_ref[...] = jnp.zeros_like(o_ref)


def _moe_experts(xs, blk_expert, n_used, w_gate, w_up, w_down, rb):
    n_rows, d = xs.shape
    dff = w_gate.shape[-1]
    rowblk = pl.BlockSpec((rb, d), lambda i, be, nu: (i, 0))
    wspec = lambda shape: pl.BlockSpec((None,) + shape, lambda i, be, nu: (be[i], 0, 0))
    grid_spec = pltpu.PrefetchScalarGridSpec(
        num_scalar_prefetch=2, grid=(n_rows // rb,),
        in_specs=[rowblk, wspec((d, dff)), wspec((d, dff)), wspec((dff, d))], out_specs=rowblk)
    return pl.pallas_call(
        _expert_kernel, grid_spec=grid_spec, out_shape=jax.ShapeDtypeStruct(xs.shape, F32),
        compiler_params=_params("arbitrary"), name="moe_experts",
    )(blk_expert, n_used, xs, w_gate.astype(BF16), w_up.astype(BF16), w_down.astype(BF16))


def _combine_kernel(dest_ref, next_ref, x_ref, route_ref, ys_ref, o_ref, got_ref, sems, *, tm):
    t = pl.program_id(0)
    slot = t % 2

    def gather(idx_ref, s):
        def start(r, c):
            for k in range(TOP_K):
                _row_copy(ys_ref, idx_ref[0, 0, k * tm + r], got_ref.at[s, k], r, sems.at[s]).start()
            return c

        lax.fori_loop(0, tm, start, 0)

    @pl.when(t == 0)
    def _():
        gather(dest_ref, 0)

    @pl.when(t + 1 < pl.num_programs(0))
    def _():
        gather(next_ref, 1 - slot)

    for k in range(TOP_K):
        pltpu.make_async_copy(ys_ref.at[pl.ds(0, tm)], got_ref.at[slot, k], sems.at[slot]).wait()
    route = route_ref[...]
    o_ref[...] = x_ref[...] + route[:, 2:3] * got_ref[slot, 0] + route[:, 3:4] * got_ref[slot, 1]


def _moe_combine(x, route, dest, ys, tm):
    t, d = x.shape
    nt = t // tm
    idx = lambda nxt: pl.BlockSpec((1, 1, TOP_K * tm), lambda i: (jnp.minimum(i + nxt, nt - 1), 0, 0),
                                   memory_space=pltpu.SMEM)
    return pl.pallas_call(
        functools.partial(_combine_kernel, tm=tm), grid=(nt,),
        in_specs=[idx(0), idx(1), pl.BlockSpec((tm, d), lambda i: (i, 0)), pl.BlockSpec((tm, LANE), lambda i: (i, 0)),
                  pl.BlockSpec(memory_space=pl.ANY)],
        out_specs=pl.BlockSpec((tm, d), lambda i: (i, 0)),
        out_shape=jax.ShapeDtypeStruct((t, d), F32),
        scratch_shapes=[pltpu.VMEM((2, TOP_K, tm, d), F32), pltpu.SemaphoreType.DMA((2,))],
        compiler_params=_params("arbitrary"), name="moe_combine",
    )(dest, dest, x, route, ys)


def _moe(x, gain, w_router, b_router, w_gate, w_up, w_down, rb=512, tm=256):
    t, d = x.shape
    route, counts = _moe_router(x, gain, w_router, b_router)
    n_blk = (t * TOP_K + rb - 1) // rb + N_EXPERTS
    padded = (counts + rb - 1) // rb * rb
    pad_end = jnp.cumsum(padded)
    pad_start = pad_end - padded
    dest = pad_start[route[:, 0:2].astype(jnp.int32)] + route[:, 4:6].astype(jnp.int32)
    dest_t = jnp.concatenate([dest[:, 0].reshape(t // tm, 1, tm), dest[:, 1].reshape(t // tm, 1, tm)], axis=2)
    fill = jnp.concatenate([(pad_start + counts) // 8 * 8, pad_end[-1:], pad_end, jnp.full((1,), n_blk * rb, jnp.int32)])
    n_used = pad_end[-1:] // rb
    blk = jnp.minimum(jnp.arange(n_blk, dtype=jnp.int32), n_used - 1) * rb
    blk_expert = jnp.sum(pad_end[None, :] <= blk[:, None], axis=1).astype(jnp.int32)
    xs = _moe_scatter(x, gain, dest_t, fill.astype(jnp.int32), n_blk * rb, tm, rb)
    ys = _moe_experts(xs, blk_expert, n_used.astype(jnp.int32), w_gate, w_up, w_down, rb)
    return _moe_combine(x, route, dest_t, ys, tm)


def kernel(x, rel_bias, ev_mix_norm, ev_w_in, nsa_q_norm, nsa_k_norm, nsa_cmp_pos, nsa_cmp_w1, nsa_cmp_w2, mla_cq_norm, mla_ckv_norm, mla_w_uq, mla_w_ukv, mla_q_norm, mla_k_norm, ev_w_out, ev_ffn_norm, ffn_w_gate, ffn_w_up, ffn_w_down, od_mix_norm, od_w_in, conv_w, od_w_out, od_ffn_norm, moe_w_router, moe_b_router, moe_w_gate, moe_w_up, moe_w_down):
    bsz, seq, d = x.shape
    depth = ev_mix_norm.shape[0] + od_mix_norm.shape[0]
    pad = 512
    for layer in range(depth):
        i = layer // 2
        if layer % 2 == 0:
            qn, kv, kc_raw, vc_raw, gates, qm, km, vm = _even_in_proj(
                x, ev_mix_norm[i], ev_w_in[i], nsa_q_norm[i], nsa_k_norm[i], mla_cq_norm[i], mla_ckv_norm[i],
                mla_w_uq[i], mla_w_ukv[i], mla_q_norm[i], mla_k_norm[i], tm=pad)
            kc = _compress(kc_raw, nsa_cmp_pos[i, 0], nsa_cmp_w1[i, 0], nsa_cmp_w2[i, 0], nsa_k_norm[i, 0], True)
            vc = _compress(vc_raw, nsa_cmp_pos[i, 1], nsa_cmp_w1[i, 1], nsa_cmp_w2[i, 1], nsa_k_norm[i, 0], False)
            o_nsa = _nsa_attention(qn, kv, kc, vc, gates, rel_bias, pad)
            o_mla = _mla_attention(qm, km, vm)
            x = _even_out_ffn(x.reshape(bsz * seq, d), o_nsa.reshape(bsz * seq, -1), o_mla.reshape(bsz * seq, -1),
                              ev_w_out[i], ev_ffn_norm[i], ffn_w_gate[i], ffn_w_up[i], ffn_w_down[i]).reshape(bsz, seq, d)
        else:
            x = _conv_mixer(x, od_mix_norm[i], od_w_in[i], conv_w[i], od_w_out[i])
            x = _moe(x.reshape(bsz * seq, d), od_ffn_norm[i], moe_w_router[i], moe_b_router[i],
                     moe_w_gate[i], moe_w_up[i], moe_w_down[i]).reshape(bsz, seq, d)
    return x
```

```python
import functools
import math

import jax
import jax.numpy as jnp
import numpy as np
from jax import lax
from jax.experimental import pallas as pl
from jax.experimental.pallas import tpu as pltpu

F32 = jnp.float32
BF16 = jnp.bfloat16

EPS = 1e-6
NEG = -1e30
FORCED_SCORE = 1e9
NSA_HEADS = 8
NSA_GROUPS = 2
NSA_REP = NSA_HEADS // NSA_GROUPS
NSA_DK = 64
CMP_BLOCK = 32
CMP_STRIDE = 16
CMP_HIDDEN = 256
SEL_BLOCK = 64
SEL_TOP_N = 16
WINDOW = 512
MLA_HEADS = 8
MLA_Q_RANK = 256
MLA_KV_RANK = 128
MLA_NOPE = 64
MLA_ROPE = 32
MLA_V = 64
ROPE_THETA = 10000.0
REL_BUCKETS = 32
REL_MAX_DIST = 128
CONV_WIDTH = 3
N_EXPERTS = 8
TOP_K = 2
EVEN_IN_SIZES = (512,) + (128,) * 6 + (24, 256, 128, 32)

LANE = 128
VMEM_LIMIT = 56 * 1024 * 1024
SEL_MASK = -30000.0
FAR_DIST = 128

_Q0, _KV0, _KC0, _VC0, _GT0, _CQ0, _CKV0, _KR0, _EVEN_W = 0, 1024, 2048, 2176, 2304, 2560, 2816, 2944, 3072


def _dot(a, b):
    return jnp.dot(a, b, preferred_element_type=F32)


def _dot_nt(a, b):
    return lax.dot_general(a, b, (((1,), (1,)), ((), ())), preferred_element_type=F32)


def _rms(x, gain, n):
    ss = jnp.sum(x * x, axis=-1, keepdims=True) * (1.0 / n)
    return x * lax.rsqrt(ss + EPS) * gain


def _params(*sem):
    return pltpu.CompilerParams(dimension_semantics=sem, vmem_limit_bytes=VMEM_LIMIT)


def _const_spec(shape):
    nd = len(shape)
    return pl.BlockSpec(shape, lambda *_: (0,) * nd, pipeline_mode=pl.Buffered(1))


def _rope(x, cos, sa, sb):
    return x * cos + pltpu.roll(x, 16, 1) * sa + pltpu.roll(x, LANE - 16, 1) * sb


def _even_in_kernel(x_ref, gmix_ref, w_ref, gq_ref, gks_ref, gkw_ref, gcq_ref, gckv_ref,
                    wuq_ref, wuk_ref, wuv_ref, gmq_ref, gmk_ref, cos_ref, sa_ref, sb_ref,
                    qn_ref, kv_ref, kc_ref, vc_ref, gate_ref, qm_ref, km_ref, vm_ref, *, tm):
    j = pl.program_id(1)

    @pl.when(j == 0)
    def _():
        kv_ref[...] = jnp.zeros_like(kv_ref)
        qn_ref[...] = jnp.zeros_like(qn_ref)
        kc_ref[...] = jnp.zeros_like(kc_ref)
        vc_ref[...] = jnp.zeros_like(vc_ref)
        gate_ref[...] = jnp.zeros_like(gate_ref)
        qm_ref[...] = jnp.zeros_like(qm_ref)
        km_ref[...] = jnp.zeros_like(km_ref)
        vm_ref[...] = jnp.zeros_like(vm_ref)

    @pl.when(j > 0)
    def _():
        xn = _rms(x_ref[...], gmix_ref[...], x_ref.shape[-1]).astype(BF16)
        hq = _dot(xn, w_ref[:, _Q0:_Q0 + 1024])
        gq = gq_ref[...]
        for h in range(NSA_HEADS):
            seg = hq[:, LANE * h:LANE * (h + 1)]
            qn_ref[:, LANE * h:LANE * (h + 1)] = (_rms(seg, gq, NSA_DK) * NSA_DK ** -0.5).astype(BF16)
        hkv = _dot(xn, w_ref[:, _KV0:_KV0 + 1024])
        pos = (j - 1) * tm + lax.broadcasted_iota(jnp.int32, (tm, LANE), 0)
        lane = lax.broadcasted_iota(jnp.int32, (tm, LANE), 1)
        onehot = jnp.where(lane - NSA_DK == pos // SEL_BLOCK, 1.0, 0.0)
        ones = jnp.where(lane >= NSA_DK, 1.0, 0.0)
        for g in range(NSA_GROUPS):
            o = 4 * LANE * g
            kv_ref[:, o:o + LANE] = (_rms(hkv[:, o:o + LANE], gks_ref[...], NSA_DK) + onehot).astype(BF16)
            kv_ref[:, o + LANE:o + 2 * LANE] = (hkv[:, o + LANE:o + 2 * LANE] + ones).astype(BF16)
            kv_ref[:, o + 2 * LANE:o + 3 * LANE] = _rms(hkv[:, o + 2 * LANE:o + 3 * LANE], gkw_ref[...],
                                                        NSA_DK).astype(BF16)
            kv_ref[:, o + 3 * LANE:o + 4 * LANE] = (hkv[:, o + 3 * LANE:o + 4 * LANE] + ones).astype(BF16)
        hc = _dot(xn, w_ref[:, _KC0:_KC0 + 256])
        kc_ref[...] = hc[:, :LANE].astype(BF16)
        vc_ref[...] = hc[:, LANE:].astype(BF16)
        gate_ref[...] = jax.nn.sigmoid(_dot(xn, w_ref[:, _GT0:_GT0 + 256]))
        hm = _dot(xn, w_ref[:, _CQ0:_EVEN_W])
        cq = _rms(hm[:, :MLA_Q_RANK], gcq_ref[...], MLA_Q_RANK).astype(BF16)
        ckv = _rms(hm[:, MLA_Q_RANK:MLA_Q_RANK + MLA_KV_RANK], gckv_ref[...], MLA_KV_RANK).astype(BF16)
        k_rope = hm[:, MLA_Q_RANK + MLA_KV_RANK:]
        qm = _dot(cq, wuq_ref[...])
        kn = _dot(ckv, wuk_ref[...])
        vm_ref[...] = _dot(ckv, wuv_ref[...]).astype(BF16)
        cos, sa, sb = cos_ref[...], sa_ref[...], sb_ref[...]
        dqk = MLA_NOPE + MLA_ROPE
        for h in range(MLA_HEADS):
            sl = slice(LANE * h, LANE * (h + 1))
            qh = _rope(_rms(qm[:, sl], gmq_ref[...], dqk), cos, sa, sb) * dqk ** -0.5
            qm_ref[:, sl] = qh.astype(BF16)
            kh = _rope(_rms(kn[:, sl] + k_rope, gmk_ref[...], dqk), cos, sa, sb)
            km_ref[:, sl] = kh.astype(BF16)


def _even_in_weights(w_in):
    offs = np.concatenate([[0], np.cumsum(EVEN_IN_SIZES)])
    part = [w_in[:, offs[n]:offs[n + 1]] for n in range(len(EVEN_IN_SIZES))]
    q, k_c, v_c, k_s, v_s, k_w, v_w, gates, c_q, c_kv, k_rope = part
    d = w_in.shape[0]
    z = lambda n: jnp.zeros((d, n), w_in.dtype)
    cols = []
    for h in range(NSA_HEADS):
        cols += [q[:, 64 * h:64 * h + 64], z(64)]
    for g in range(NSA_GROUPS):
        s = slice(64 * g, 64 * g + 64)
        cols += [k_s[:, s], z(64), v_s[:, s], z(64), k_w[:, s], z(64), v_w[:, s], z(64)]
    cols += [k_c, v_c]
    for g in range(NSA_GROUPS):
        cols += [gates[:, 12 * g:12 * g + 12], z(LANE - 12)]
    cols += [c_q, c_kv, z(64), k_rope, z(32)]
    w = jnp.concatenate(cols, axis=1)
    assert w.shape[1] == _EVEN_W
    return w.astype(BF16)


def _pad_gain(g, width):
    return jnp.pad(g.astype(F32), (0, width - g.shape[0]))[None, :]


def _rope_tables(seq):
    half = MLA_ROPE // 2
    inv_freq = ROPE_THETA ** (-jnp.arange(half, dtype=F32) / half)
    ang = jnp.arange(seq).astype(F32)[:, None] * inv_freq[None, :]
    cos, sin = jnp.cos(ang), jnp.sin(ang)
    one = jnp.ones((seq, MLA_NOPE), F32)
    zn = jnp.zeros((seq, MLA_NOPE), F32)
    zt = jnp.zeros((seq, LANE - MLA_NOPE - MLA_ROPE), F32)
    zh = jnp.zeros((seq, half), F32)
    cos_t = jnp.concatenate([one, cos, cos, zt + 1.0], axis=1)
    sa = jnp.concatenate([zn, zh, sin, zt], axis=1)
    sb = jnp.concatenate([zn, -sin, zh, zt], axis=1)
    return cos_t, sa, sb


def _even_in_proj(x, gmix, w_in, q_norm, k_norm, cq_norm, ckv_norm, w_uq, w_ukv, mq_norm, mk_norm, tm=512):
    bsz, seq, d = x.shape
    nt = seq // tm
    w = _even_in_weights(w_in)
    dqk = MLA_NOPE + MLA_ROPE
    wuq = jnp.concatenate(
        [jnp.pad(w_uq[:, dqk * h:dqk * (h + 1)], ((0, 0), (0, LANE - dqk))) for h in range(MLA_HEADS)],
        axis=1).astype(BF16)
    kvw = MLA_NOPE + MLA_V
    wuk = jnp.concatenate(
        [jnp.pad(w_ukv[:, kvw * h:kvw * h + MLA_NOPE], ((0, 0), (0, LANE - MLA_NOPE))) for h in range(MLA_HEADS)],
        axis=1).astype(BF16)
    wuv = jnp.concatenate([w_ukv[:, kvw * h + MLA_NOPE:kvw * (h + 1)] for h in range(MLA_HEADS)], axis=1).astype(BF16)
    cos_t, sa, sb = _rope_tables(seq)
    tok = lambda width: pl.BlockSpec((None, tm, width), lambda b, j: (b, jnp.maximum(j - 1, 0), 0))
    postab = pl.BlockSpec((tm, LANE), lambda b, j: (jnp.maximum(j - 1, 0), 0))
    in_specs = [tok(d), _const_spec((1, d)), _const_spec((d, _EVEN_W)),
                _const_spec((1, LANE)), _const_spec((1, LANE)), _const_spec((1, LANE)),
                _const_spec((1, MLA_Q_RANK)), _const_spec((1, MLA_KV_RANK)),
                _const_spec(wuq.shape), _const_spec(wuk.shape), _const_spec(wuv.shape),
                _const_spec((1, LANE)), _const_spec((1, LANE)), postab, postab, postab]
    out_shape = [jax.ShapeDtypeStruct((bsz, seq, 1024), BF16),
                 jax.ShapeDtypeStruct((bsz, seq + tm, 1024), BF16),
                 jax.ShapeDtypeStruct((bsz, seq, LANE), BF16),
                 jax.ShapeDtypeStruct((bsz, seq, LANE), BF16),
                 jax.ShapeDtypeStruct((bsz, seq, 2 * LANE), F32),
                 jax.ShapeDtypeStruct((bsz, seq, 1024), BF16),
                 jax.ShapeDtypeStruct((bsz, seq, 1024), BF16),
                 jax.ShapeDtypeStruct((bsz, seq, 512), BF16)]
    out_specs = [tok(1024), pl.BlockSpec((None, tm, 1024), lambda b, j: (b, j, 0)), tok(LANE), tok(LANE),
                 tok(2 * LANE), tok(1024), tok(1024), tok(512)]
    return pl.pallas_call(
        functools.partial(_even_in_kernel, tm=tm), grid=(bsz, nt + 1), in_specs=in_specs, out_specs=out_specs,
        out_shape=out_shape, compiler_params=_params("parallel", "arbitrary"), name="even_in_proj",
    )(x, gmix[None, :], w, _pad_gain(q_norm, LANE), _pad_gain(k_norm[1], LANE), _pad_gain(k_norm[2], LANE),
      cq_norm[None, :], ckv_norm[None, :], wuq, wuk, wuv, _pad_gain(mq_norm, LANE), _pad_gain(mk_norm, LANE),
      cos_t, sa, sb)


def _cmp_kernel(x_ref, wa_ref, wb_ref, pa_ref, pb_ref, w2_ref, gain_ref, o_ref, *, normalize):
    x = x_ref[...]
    ua = _dot(x, wa_ref[...])
    ub = _dot(x, wb_ref[...])
    pt = _dot(pa_ref[...], wa_ref[...]) + _dot(pb_ref[...], wb_ref[...])
    n = ub.shape[0]
    pre = ua + pltpu.roll(ub, n - 1, 0) + pt[0:1]
    hid = jax.nn.gelu(pre).astype(BF16)
    out = _dot(hid, w2_ref[...])
    if normalize:
        gain = gain_ref[...]
        for g in range(NSA_GROUPS):
            sl = slice(LANE * g, LANE * (g + 1))
            o_ref[:, sl] = _rms(out[:, sl], gain, NSA_DK).astype(BF16)
    else:
        o_ref[...] = out.astype(BF16)


def _compress(kv, pos_emb, w1, w2, gain, normalize):
    bsz, seq, _ = kv.shape
    half = CMP_BLOCK // 2
    assert CMP_STRIDE == half
    nrow = seq // CMP_STRIDE
    x = kv.reshape(bsz, nrow, CMP_STRIDE * LANE)
    w1r = w1.reshape(CMP_BLOCK, NSA_DK, CMP_HIDDEN)
    eye = jnp.eye(NSA_GROUPS, dtype=w1.dtype)
    widen = lambda w: jnp.einsum("ldn,gh->lgdhn", w, eye).reshape(half * LANE, NSA_GROUPS * CMP_HIDDEN).astype(BF16)
    wa, wb = widen(w1r[:half]), widen(w1r[half:])
    prow = lambda p: jnp.pad(jnp.broadcast_to(p[:, None, :], (half, NSA_GROUPS, NSA_DK)).reshape(1, half * LANE),
                             ((0, 15), (0, 0))).astype(BF16)
    pa, pb = prow(pos_emb[:half]), prow(pos_emb[half:])
    second = jnp.zeros_like(w2) if normalize else w2
    w2w = jnp.einsum("nd,gh->gnhd", jnp.concatenate([w2, second], axis=1), eye)
    w2w = w2w.reshape(NSA_GROUPS * CMP_HIDDEN, NSA_GROUPS * LANE).astype(BF16)
    return pl.pallas_call(
        functools.partial(_cmp_kernel, normalize=normalize), grid=(bsz,),
        in_specs=[pl.BlockSpec((None, nrow, CMP_STRIDE * LANE), lambda b: (b, 0, 0)),
                  _const_spec(wa.shape), _const_spec(wb.shape), _const_spec(pa.shape), _const_spec(pb.shape),
                  _const_spec(w2w.shape), _const_spec((1, LANE))],
        out_specs=pl.BlockSpec((None, nrow, NSA_GROUPS * LANE), lambda b: (b, 0, 0)),
        out_shape=jax.ShapeDtypeStruct((bsz, nrow, NSA_GROUPS * LANE), BF16),
        compiler_params=_params("parallel"), name="nsa_compress",
    )(x, wa, wb, pa, pb, w2w, _pad_gain(gain, LANE))


def _bucket_table():
    dist = np.arange(FAR_DIST + 1)
    max_exact = REL_BUCKETS // 2
    nf = np.maximum(dist, max_exact).astype(np.float32)
    large = max_exact + (np.log(nf / max_exact) / math.log(REL_MAX_DIST / max_exact)
                         * (REL_BUCKETS - max_exact)).astype(np.int32)
    return np.where(dist < max_exact, dist, np.minimum(large, REL_BUCKETS - 1))


CMP_NEAR = 32


def _expand(tbl, idx):
    idx = np.asarray(idx)
    onehot = jnp.asarray(np.eye(tbl.shape[1], dtype=np.float32)[idx.reshape(-1)])
    out = lax.dot_general(tbl, onehot, (((1,), (1,)), ((), ())), precision=lax.Precision.HIGHEST)
    return out.reshape((tbl.shape[0],) + idx.shape)


def _toeplitz_tile(tbl, far, width, dist0, valid):
    period = width + SEL_BLOCK + 1
    j = np.arange(period)
    dist = dist0 - np.where(j < width, j, j - period)
    gen = jnp.where(jnp.asarray(valid(dist))[None], _expand(tbl, np.clip(dist, 0, FAR_DIST)) - far, NEG)
    flat = jnp.tile(gen, (1, SEL_BLOCK))[:, :SEL_BLOCK * (period - 1)]
    return flat.reshape(tbl.shape[0], SEL_BLOCK, period - 1)[:, :, :width]


def _bias_tables(rel_bias):
    tbl = _expand(rel_bias.astype(F32).T, _bucket_table())
    far = tbl[:, FAR_DIST:]
    bias_near = _toeplitz_tile(tbl, far, 4 * SEL_BLOCK, 3 * SEL_BLOCK, lambda d: d >= 0)
    bias_win = _toeplitz_tile(tbl, 0.0, WINDOW + SEL_BLOCK, WINDOW, lambda d: (d >= 0) & (d < WINDOW))
    d_c = (np.arange(SEL_BLOCK)[:, None] + (CMP_NEAR // 2) * CMP_STRIDE - (CMP_BLOCK - 1)
           - CMP_STRIDE * np.arange(CMP_NEAR)[None, :])
    assert d_c[:, 0].min() >= FAR_DIST and d_c[:, -1].max() < 0
    bias_cmp = jnp.where(jnp.asarray(d_c >= 0)[None], _expand(tbl, np.clip(d_c, 0, FAR_DIST)) - far[:, :, None], 0.0)
    return bias_cmp, bias_near, bias_win


def _cmp_mask_dist(seq):
    a = np.tile(np.arange(SEL_BLOCK), NSA_REP)[:, None]
    c = np.arange(seq // CMP_STRIDE)[None, :]
    return jnp.asarray(c * CMP_STRIDE + CMP_BLOCK - 1 - a, jnp.int32)


def _overlap_t(seq):
    ncp = seq // CMP_STRIDE
    cs = np.arange(ncp)[None, :] * CMP_STRIDE
    ss = np.arange(SEL_BLOCK)[:, None] * SEL_BLOCK
    ov = (cs < ss + SEL_BLOCK) & (cs + CMP_BLOCK - 1 >= ss) & (np.arange(ncp)[None, :] < (seq - CMP_BLOCK) // CMP_STRIDE + 1)
    return jnp.asarray(ov, BF16)


def _softmax_step(s, v, state):
    m, l, acc = state
    mn = jnp.maximum(m, jnp.max(s, axis=-1, keepdims=True))
    alpha = jnp.exp(m - mn)
    p = jnp.exp(s - mn)
    return mn, alpha * l + jnp.sum(p, axis=-1, keepdims=True), alpha * acc + _dot(p.astype(BF16), v)


FAR_CHUNK = 1024


def _lane_fold(x, op):
    out = x[:, 0:LANE]
    for c in range(1, x.shape[1] // LANE):
        out = op(out, x[:, LANE * c:LANE * (c + 1)])
    return out


def _nsa_kernel(q_ref, kv_ref, kc_ref, vc_ref, g_ref, bc_ref, bn_ref, bw_ref, ovt_ref, dmask_ref, o_ref, s_ref, *, pad):
    i = pl.program_id(1)
    qb, rep, ng = SEL_BLOCK, NSA_REP, NSA_GROUPS
    rows = qb * rep
    gw = 4 * LANE
    ncp = kc_ref.shape[0]
    qs = [jnp.concatenate([q_ref[:, gw * g + LANE * r:gw * g + LANE * (r + 1)] for r in range(rep)], axis=0)
          for g in range(ng)]

    m_i = lax.broadcasted_iota(jnp.int32, (CMP_NEAR, ncp), 0)
    c_i = lax.broadcasted_iota(jnp.int32, (CMP_NEAR, ncp), 1)
    shift = jnp.where(c_i - m_i == (qb // CMP_STRIDE) * i - CMP_NEAR // 2, 1.0, 0.0).astype(BF16)
    visible = dmask_ref[...] <= i * qb
    o_c, psums = [], []
    for g in range(ng):
        base = bc_ref[rep * g:rep * (g + 1)].reshape(rows, CMP_NEAR)
        b_hi = base.astype(BF16)
        b_lo = (base - b_hi.astype(F32)).astype(BF16)
        s = _dot_nt(qs[g], kc_ref[:, LANE * g:LANE * (g + 1)]) + _dot(b_hi, shift) + _dot(b_lo, shift)
        s = jnp.where(visible, s, NEG)
        m = jnp.maximum(jnp.max(s, axis=-1, keepdims=True), -1e20)
        e = jnp.exp(s - m)
        inv = 1.0 / jnp.maximum(jnp.sum(e, axis=-1, keepdims=True), 1e-30)
        o_c.append(_dot(e.astype(BF16), vc_ref[:, LANE * g:LANE * (g + 1)]) * inv)
        p = e * inv
        psums.append(p[0:qb] + p[qb:2 * qb] + p[2 * qb:3 * qb] + p[3 * qb:4 * qb])

    psum = jnp.concatenate(psums, axis=0)
    hi = psum.astype(BF16)
    r1 = psum - hi.astype(F32)
    mid = r1.astype(BF16)
    lo = (r1 - mid.astype(F32)).astype(BF16)
    ovt = ovt_ref[...]
    imp = _dot_nt(ovt, hi) + _dot_nt(ovt, mid) + _dot_nt(ovt, lo)
    jj = lax.broadcasted_iota(jnp.int32, (qb, ng * qb), 0)
    imp = jnp.where((jj == i) | (jj == 0), FORCED_SCORE, imp)
    imp = jnp.where(jj > i, -1.0, imp)
    grp = [imp[8 * a:8 * a + 8] for a in range(8)]
    cnt = [jnp.zeros((8, ng * qb), F32) for _ in range(8)]
    sub = lax.broadcasted_iota(jnp.int32, (8, ng * qb), 0)
    for k in range(qb):
        rk = imp[k:k + 1, :]
        for a in range(8):
            if 8 * a + 7 <= k:
                cnt[a] = cnt[a] + jnp.where(rk > grp[a], 1.0, 0.0)
            elif 8 * a > k:
                cnt[a] = cnt[a] + jnp.where(rk >= grp[a], 1.0, 0.0)
            else:
                cnt[a] = cnt[a] + jnp.where(sub + 8 * a > k, jnp.where(rk >= grp[a], 1.0, 0.0),
                                            jnp.where(rk > grp[a], 1.0, 0.0))
    neg = jnp.where(jnp.concatenate(cnt, axis=0) < SEL_TOP_N, 0.0, SEL_MASK)
    neg_far = jnp.where(jj >= i - 3, SEL_MASK, neg)
    zero = jnp.zeros((qb, ng * qb), BF16)
    ext_near = jnp.concatenate([zero, neg.astype(BF16)], axis=0)
    ext_far = jnp.concatenate([zero, neg_far.astype(BF16)], axis=0)
    q_row = lax.broadcasted_iota(jnp.int32, (qb, ng * qb), 0)
    q_lane = lax.broadcasted_iota(jnp.int32, (qb, ng * qb), 1)
    q_near, q_far = [], []
    for g in range(ng):
        pick = jnp.where(q_lane == q_row + qb * g, 1.0, 0.0).astype(BF16)
        q_near.append(qs[g] + jnp.concatenate([_dot_nt(pick, ext_near).astype(BF16)] * rep, axis=0))
        q_far.append(qs[g] + jnp.concatenate([_dot_nt(pick, ext_far).astype(BF16)] * rep, axis=0))

    st_n = pl.multiple_of(pad + (i - 3) * qb, qb)
    in_seq = lax.broadcasted_iota(jnp.int32, (rows, 4 * qb), 1) + (i - 3) * qb >= 0
    s_near = [jnp.where(in_seq, _dot_nt(q_near[g], kv_ref[pl.ds(st_n, 4 * qb), gw * g:gw * g + LANE])
                        + bn_ref[rep * g:rep * (g + 1)].reshape(rows, 4 * qb), NEG) for g in range(ng)]

    nch = (jnp.maximum(i - 3, 0) * qb + FAR_CHUNK - 1) // FAR_CHUNK

    def pass1(c, mx):
        st = pl.multiple_of(pad + c * FAR_CHUNK, math.gcd(pad, FAR_CHUNK))
        col = pl.multiple_of(c * FAR_CHUNK, FAR_CHUNK)
        out = []
        for g in range(ng):
            s = _dot_nt(q_far[g], kv_ref[pl.ds(st, FAR_CHUNK), gw * g:gw * g + LANE])
            s_ref[g, :, pl.ds(col, FAR_CHUNK)] = s
            out.append(jnp.maximum(mx[g], _lane_fold(s, jnp.maximum)))
        return tuple(out)

    mx = lax.fori_loop(0, nch, pass1, tuple(jnp.full((rows, LANE), NEG, F32) for _ in range(ng)))
    ms = [jnp.max(jnp.maximum(mx[g], _lane_fold(s_near[g], jnp.maximum)), axis=-1, keepdims=True) for g in range(ng)]

    def pass2(c, state):
        st = pl.multiple_of(pad + c * FAR_CHUNK, math.gcd(pad, FAR_CHUNK))
        col = pl.multiple_of(c * FAR_CHUNK, FAR_CHUNK)
        out = []
        for g in range(ng):
            p = jnp.exp(s_ref[g, :, pl.ds(col, FAR_CHUNK)] - ms[g])
            out.append(state[g] + _dot(p.astype(BF16), kv_ref[pl.ds(st, FAR_CHUNK), gw * g + LANE:gw * g + 2 * LANE]))
        return tuple(out)

    far = lax.fori_loop(0, nch, pass2, tuple(jnp.zeros((rows, LANE), F32) for _ in range(ng)))
    o_s = []
    for g in range(ng):
        p = jnp.exp(s_near[g] - ms[g])
        pv = far[g] + _dot(p.astype(BF16), kv_ref[pl.ds(st_n, 4 * qb), gw * g + LANE:gw * g + 2 * LANE])
        o_s.append(pv / pv[:, NSA_DK:NSA_DK + 1])

    wk = WINDOW + qb
    st_w = pl.multiple_of(pad - WINDOW + i * qb, qb)
    in_seq = lax.broadcasted_iota(jnp.int32, (rows, wk), 1) + i * qb - WINDOW >= 0
    o_w = []
    for g in range(ng):
        s = _dot_nt(qs[g], kv_ref[pl.ds(st_w, wk), gw * g + 2 * LANE:gw * g + 3 * LANE])
        s = jnp.where(in_seq, s + bw_ref[rep * g:rep * (g + 1)].reshape(rows, wk), NEG)
        e = jnp.exp(s - jnp.max(s, axis=-1, keepdims=True))
        pv = _dot(e.astype(BF16), kv_ref[pl.ds(st_w, wk), gw * g + 3 * LANE:gw * g + 4 * LANE])
        o_w.append(pv / pv[:, NSA_DK:NSA_DK + 1])

    gates = g_ref[...]
    low = lax.broadcasted_iota(jnp.int32, (qb, LANE), 1) < NSA_DK
    for g in range(ng):
        outs = []
        for r in range(rep):
            sl = slice(qb * r, qb * (r + 1))
            c0 = LANE * g + 3 * r
            outs.append(gates[:, c0:c0 + 1] * o_c[g][sl] + gates[:, c0 + 1:c0 + 2] * o_s[g][sl]
                        + gates[:, c0 + 2:c0 + 3] * o_w[g][sl])
        for pr in range(rep // 2):
            c0 = 2 * LANE * g + LANE * pr
            pair = jnp.where(low, outs[2 * pr], pltpu.roll(outs[2 * pr + 1], NSA_DK, 1))
            o_ref[:, c0:c0 + LANE] = pair.astype(o_ref.dtype)


def _nsa_attention(qn, kv, kc, vc, gates, rel_bias, pad):
    bsz, seq, _ = qn.shape
    qb, ng = SEL_BLOCK, NSA_GROUPS
    ncp = seq // CMP_STRIDE
    assert seq // qb <= qb and pad >= WINDOW and seq % FAR_CHUNK == 0
    bias_cmp, bias_near, bias_win = _bias_tables(rel_bias)
    ovt = _overlap_t(seq)
    dmask = _cmp_mask_dist(seq)
    tok = lambda width: pl.BlockSpec((None, qb, width), lambda b, i: (b, i, 0))
    per_b = lambda n, width: pl.BlockSpec((None, n, width), lambda b, i: (b, 0, 0))
    return pl.pallas_call(
        functools.partial(_nsa_kernel, pad=pad), grid=(bsz, seq // qb),
        in_specs=[tok(ng * 4 * LANE), per_b(seq + pad, ng * 4 * LANE), per_b(ncp, ng * LANE), per_b(ncp, ng * LANE),
                  tok(ng * LANE), _const_spec(bias_cmp.shape), _const_spec(bias_near.shape),
                  _const_spec(bias_win.shape), _const_spec(ovt.shape), _const_spec(dmask.shape)],
        out_specs=tok(NSA_HEADS * NSA_DK),
        out_shape=jax.ShapeDtypeStruct((bsz, seq, NSA_HEADS * NSA_DK), BF16),
        scratch_shapes=[pltpu.VMEM((ng, NSA_REP * qb, seq), F32)],
        compiler_params=_params("parallel", "arbitrary"), name="nsa_attention",
    )(qn, kv, kc, vc, gates, bias_cmp, bias_near, bias_win, ovt, dmask)


def _mla_attn_kernel(q_ref, k_ref, v_ref, o_ref, *, tq, ck):
    i = pl.program_id(2)
    heads = [slice(LANE * hh, LANE * (hh + 1)) for hh in range(2)]
    qs = [q_ref[:, sl] for sl in heads]

    def body(c, states):
        st = pl.multiple_of(c * ck, ck)
        v = v_ref[pl.ds(st, ck), :]
        return tuple(_softmax_step(_dot_nt(q, k_ref[pl.ds(st, ck), sl]), v, state)
                     for q, sl, state in zip(qs, heads, states))

    init = (jnp.full((tq, 1), NEG, F32), jnp.zeros((tq, 1), F32), jnp.zeros((tq, LANE), F32))
    states = lax.fori_loop(0, i * (tq // ck), body, (init, init))
    st = pl.multiple_of(i * tq, tq)
    causal = lax.broadcasted_iota(jnp.int32, (tq, tq), 0) >= lax.broadcasted_iota(jnp.int32, (tq, tq), 1)
    outs = []
    for q, sl, state in zip(qs, heads, states):
        s = _dot_nt(q, k_ref[pl.ds(st, tq), sl])
        _, l, acc = _softmax_step(jnp.where(causal, s, NEG), v_ref[pl.ds(st, tq), :], state)
        outs.append(acc / l)
    low = lax.broadcasted_iota(jnp.int32, (tq, LANE), 1) < MLA_V
    o_ref[...] = jnp.where(low, outs[0], outs[1]).astype(o_ref.dtype)


def _mla_attention(qm, km, vm, tq=1024, ck=512):
    bsz, seq, _ = qm.shape
    tq = min(tq, seq)
    return pl.pallas_call(
        functools.partial(_mla_attn_kernel, tq=tq, ck=min(ck, tq)), grid=(bsz, MLA_HEADS // 2, seq // tq),
        in_specs=[pl.BlockSpec((None, tq, 2 * LANE), lambda b, h, i: (b, i, h)),
                  pl.BlockSpec((None, seq, 2 * LANE), lambda b, h, i: (b, 0, h)),
                  pl.BlockSpec((None, seq, LANE), lambda b, h, i: (b, 0, h))],
        out_specs=pl.BlockSpec((None, tq, LANE), lambda b, h, i: (b, i, h)),
        out_shape=jax.ShapeDtypeStruct((bsz, seq, MLA_HEADS * MLA_V), BF16),
        compiler_params=_params("parallel", "parallel", "arbitrary"), name="mla_attention",
    )(qm, km, vm)


def _even_out_kernel(x_ref, on_ref, om_ref, wn_ref, wm_ref, g_ref, wg_ref, wu_ref, wd_ref, o_ref, *, chunk):
    x1 = x_ref[...] + _dot(on_ref[...], wn_ref[...]) + _dot(om_ref[...], wm_ref[...])
    n = _rms(x1, g_ref[...], x1.shape[-1]).astype(BF16)
    ffn = None
    for f0 in range(0, wg_ref.shape[1], chunk):
        gate = _dot(n, wg_ref[:, f0:f0 + chunk])
        act = (gate * jax.nn.sigmoid(gate) * _dot(n, wu_ref[:, f0:f0 + chunk])).astype(BF16)
        part = _dot(act, wd_ref[f0:f0 + chunk, :])
        ffn = part if ffn is None else ffn + part
    o_ref[...] = x1 + ffn


def _even_out_ffn(x, o_nsa, o_mla, w_out, gain, w_gate, w_up, w_down, tm=512):
    t, d = x.shape
    dff = w_gate.shape[1]
    wn, wm = w_out[:o_nsa.shape[1]].astype(BF16), w_out[o_nsa.shape[1]:].astype(BF16)
    row = lambda width: pl.BlockSpec((tm, width), lambda i: (i, 0))
    return pl.pallas_call(
        functools.partial(_even_out_kernel, chunk=dff // 2), grid=(t // tm,),
        in_specs=[row(d), row(o_nsa.shape[1]), row(o_mla.shape[1]), _const_spec(wn.shape), _const_spec(wm.shape),
                  _const_spec((1, d)), _const_spec((d, dff)), _const_spec((d, dff)), _const_spec((dff, d))],
        out_specs=row(d), out_shape=jax.ShapeDtypeStruct((t, d), F32),
        compiler_params=_params("parallel"), name="even_out_ffn",
    )(x, o_nsa, o_mla, wn, wm, gain[None, :], w_gate.astype(BF16), w_up.astype(BF16), w_down.astype(BF16))


def _conv_kernel(x_ref, g_ref, win_ref, cw_ref, wout_ref, o_ref, vbuf_ref, *, tm):
    j = pl.program_id(1)
    x = x_ref[...]
    d = x.shape[-1]
    n = _rms(x, g_ref[...], d).astype(BF16)
    b_gate = _dot(n, win_ref[:, 0:d])
    v = _dot(n, win_ref[:, d:2 * d]) * _dot(n, win_ref[:, 2 * d:3 * d])

    @pl.when(j == 0)
    def _():
        vbuf_ref[0:8, :] = jnp.zeros((8, d), F32)

    vbuf_ref[8:8 + tm, :] = v
    cw = cw_ref[...]
    y = cw[2:3] * v + cw[1:2] * vbuf_ref[7:7 + tm, :] + cw[0:1] * vbuf_ref[6:6 + tm, :]
    vbuf_ref[0:8, :] = v[tm - 8:tm]
    o_ref[...] = x + _dot((b_gate * y).astype(BF16), wout_ref[...])


def _conv_mixer(x, gain, w_in, conv_w, w_out, tm=512):
    bsz, seq, d = x.shape
    cw = jnp.pad(conv_w.astype(F32), ((0, 8 - CONV_WIDTH), (0, 0)))
    tok = pl.BlockSpec((None, tm, d), lambda b, j: (b, j, 0))
    return pl.pallas_call(
        functools.partial(_conv_kernel, tm=tm), grid=(bsz, seq // tm),
        in_specs=[tok, _const_spec((1, d)), _const_spec((d, 3 * d)), _const_spec((8, d)), _const_spec((d, d))],
        out_specs=tok, out_shape=jax.ShapeDtypeStruct((bsz, seq, d), F32),
        scratch_shapes=[pltpu.VMEM((tm + 8, d), F32)],
        compiler_params=_params("parallel", "arbitrary"), name="conv_mixer",
    )(x, gain[None, :], w_in.astype(BF16), cw, w_out.astype(BF16))


def _router_kernel(x_ref, g_ref, wr_ref, br_ref, tri_ref, route_ref, cnt_ref, carry_ref, *, tm):
    t = pl.program_id(0)

    @pl.when(t == 0)
    def _():
        carry_ref[...] = jnp.zeros_like(carry_ref)

    n = _rms(x_ref[...], g_ref[...], x_ref.shape[-1])
    hi = n.astype(BF16)
    lo = (n - hi.astype(F32)).astype(BF16)
    whi, wlo = wr_ref[0], wr_ref[1]
    logits = _dot(hi, whi) + _dot(lo, whi) + _dot(hi, wlo) + br_ref[...]
    lane = lax.broadcasted_iota(jnp.int32, (tm, LANE), 1).astype(F32)
    big = float(LANE)
    m1 = jnp.max(logits, axis=-1, keepdims=True)
    e1 = jnp.min(jnp.where(logits == m1, lane, big), axis=-1, keepdims=True)
    rest = jnp.where(lane == e1, NEG, logits)
    m2 = jnp.max(rest, axis=-1, keepdims=True)
    e2 = jnp.min(jnp.where(rest == m2, lane, big), axis=-1, keepdims=True)
    z = jnp.exp(m2 - m1)
    w1 = 1.0 / (1.0 + z)
    w2 = z / (1.0 + z)
    oh1 = jnp.where(lane == e1, 1.0, 0.0)
    oh2 = jnp.where(lane == e2, 1.0, 0.0)
    both = oh1 + oh2
    before = _dot(tri_ref[...], both.astype(BF16)) + carry_ref[0:1, :]
    r1 = jnp.sum(oh1 * before, axis=-1, keepdims=True)
    r2 = jnp.sum(oh2 * before, axis=-1, keepdims=True)
    cols = [e1, e2, w1, w2, r1, r2]
    out = jnp.zeros((tm, LANE), F32)
    for c, val in enumerate(cols):
        out = jnp.where(lane == c, val, out)
    route_ref[...] = out
    carry_ref[0:1, :] = carry_ref[0:1, :] + jnp.sum(both, axis=0, keepdims=True)
    cnt_ref[...] = carry_ref[...]


def _moe_router(x, gain, w_router, b_router, tm=512):
    t, d = x.shape
    wr = jnp.pad(w_router.astype(F32), ((0, 0), (0, LANE - N_EXPERTS)))
    whi = wr.astype(BF16)
    wlo = (wr - whi.astype(F32)).astype(BF16)
    br = jnp.concatenate([b_router.astype(F32), jnp.full((LANE - N_EXPERTS,), NEG, F32)])[None, :]
    tri = jnp.asarray(np.tril(np.ones((tm, tm), np.float32), -1), BF16)
    route, cnt = pl.pallas_call(
        functools.partial(_router_kernel, tm=tm), grid=(t // tm,),
        in_specs=[pl.BlockSpec((tm, d), lambda i: (i, 0)), _const_spec((1, d)), _const_spec((2, d, LANE)),
                  _const_spec((1, LANE)), _const_spec((tm, tm))],
        out_specs=[pl.BlockSpec((tm, LANE), lambda i: (i, 0)), pl.BlockSpec((8, LANE), lambda i: (0, 0))],
        out_shape=[jax.ShapeDtypeStruct((t, LANE), F32), jax.ShapeDtypeStruct((8, LANE), F32)],
        scratch_shapes=[pltpu.VMEM((8, LANE), F32)],
        compiler_params=_params("arbitrary"), name="moe_router",
    )(x, gain[None, :], jnp.stack([whi, wlo]), br, tri)
    return route, cnt[0, :N_EXPERTS].astype(jnp.int32)


def _row_copy(src, i, dst, j, sem):
    return pltpu.make_async_copy(src.at[pl.ds(i, 1)], dst.at[pl.ds(j, 1)], sem)


def _scatter_kernel(fill_ref, dest_ref, x_ref, g_ref, xs_ref, xn_ref, zero_ref, sems, *, tm, rb):
    @pl.when(pl.program_id(0) == 0)
    def _():
        sem = sems.at[0]
        zero_ref[...] = jnp.zeros_like(zero_ref)
        sizes = [rb >> s for s in range(rb.bit_length() - 3)]
        for e in range(N_EXPERTS + 1):
            lo, n = fill_ref[e], fill_ref[N_EXPERTS + 1 + e] - fill_ref[e]
            whole = n // rb

            def copy(off, size):
                return pltpu.make_async_copy(zero_ref.at[pl.ds(0, size)],
                                             xs_ref.at[pl.ds(pl.multiple_of(off, 8), size)], sem)

            def blocks(k, c, lo=lo, copy=copy):
                copy(lo + k * rb, rb).start()
                copy(lo + k * rb, rb).wait()
                return c

            lax.fori_loop(0, whole, blocks, 0)
            off = lo + whole * rb
            for size in sizes[1:]:
                @pl.when((n & size) != 0)
                def _(off=off, size=size, copy=copy):
                    copy(off, size).start()
                    copy(off, size).wait()
                off = off + (n & size)

    t = pl.program_id(0)
    slot = t % 2
    rows, row_sem = xn_ref.at[slot], sems.at[slot]
    rows[...] = _rms(x_ref[...], g_ref[...], x_ref.shape[-1])

    def start(r, c):
        _row_copy(rows, r, xs_ref, dest_ref[0, 0, r], row_sem).start()
        _row_copy(rows, r, xs_ref, dest_ref[0, 0, tm + r], row_sem).start()
        return c

    lax.fori_loop(0, tm, start, 0)

    def drain(s):
        for _ in range(TOP_K):
            pltpu.make_async_copy(xn_ref.at[s], xs_ref.at[pl.ds(0, tm)], sems.at[s]).wait()

    @pl.when(t > 0)
    def _():
        drain(1 - slot)

    @pl.when(t == pl.num_programs(0) - 1)
    def _():
        drain(slot)


def _moe_scatter(x, gain, dest, fill, n_rows, tm, rb):
    t, d = x.shape
    nt = t // tm
    grid_spec = pltpu.PrefetchScalarGridSpec(
        num_scalar_prefetch=1, grid=(nt,),
        in_specs=[pl.BlockSpec((1, 1, 2 * tm), lambda i, f: (i, 0, 0), memory_space=pltpu.SMEM),
                  pl.BlockSpec((tm, d), lambda i, f: (i, 0)),
                  pl.BlockSpec((1, d), lambda i, f: (0, 0))],
        out_specs=pl.BlockSpec(memory_space=pl.ANY),
        scratch_shapes=[pltpu.VMEM((2, tm, d), F32), pltpu.VMEM((rb, d), F32), pltpu.SemaphoreType.DMA((2,))])
    return pl.pallas_call(
        functools.partial(_scatter_kernel, tm=tm, rb=rb), grid_spec=grid_spec,
        out_shape=jax.ShapeDtypeStruct((n_rows, d), F32),
        compiler_params=_params("arbitrary"), name="moe_scatter",
    )(fill, dest, x, gain[None, :])


def _expert_kernel(be_ref, nu_ref, x_ref, wg_ref, wu_ref, wd_ref, o_ref):
    used = pl.program_id(0) < nu_ref[0]

    @pl.when(used)
    def _():
        xb = x_ref[...].astype(BF16)
        gate = _dot(xb, wg_ref[...])
        act = (gate * jax.nn.sigmoid(gate) * _dot(xb, wu_ref[...])).astype(BF16)
        o_ref[...] = _dot(act, wd_ref[...])

    @pl.when(jnp.logical_not(used))
    def _():
        o_ref[...] = jnp.zeros_like(o_ref)


def _moe_experts(xs, blk_expert, n_used, w_gate, w_up, w_down, rb):
    n_rows, d = xs.shape
    dff = w_gate.shape[-1]
    rowblk = pl.BlockSpec((rb, d), lambda i, be, nu: (i, 0))
    wspec = lambda shape: pl.BlockSpec((None,) + shape, lambda i, be, nu: (be[i], 0, 0))
    grid_spec = pltpu.PrefetchScalarGridSpec(
        num_scalar_prefetch=2, grid=(n_rows // rb,),
        in_specs=[rowblk, wspec((d, dff)), wspec((d, dff)), wspec((dff, d))], out_specs=rowblk)
    return pl.pallas_call(
        _expert_kernel, grid_spec=grid_spec, out_shape=jax.ShapeDtypeStruct(xs.shape, F32),
        compiler_params=_params("arbitrary"), name="moe_experts",
    )(blk_expert, n_used, xs, w_gate.astype(BF16), w_up.astype(BF16), w_down.astype(BF16))


def _combine_kernel(dest_ref, next_ref, x_ref, route_ref, ys_ref, o_ref, got_ref, sems, *, tm):
    t = pl.program_id(0)
    slot = t % 2

    def gather(idx_ref, s):
        def start(r, c):
            for k in range(TOP_K):
                _row_copy(ys_ref, idx_ref[0, 0, k * tm + r], got_ref.at[s, k], r, sems.at[s]).start()
            return c

        lax.fori_loop(0, tm, start, 0)

    @pl.when(t == 0)
    def _():
        gather(dest_ref, 0)

    @pl.when(t + 1 < pl.num_programs(0))
    def _():
        gather(next_ref, 1 - slot)

    for k in range(TOP_K):
        pltpu.make_async_copy(ys_ref.at[pl.ds(0, tm)], got_ref.at[slot, k], sems.at[slot]).wait()
    route = route_ref[...]
    o_ref[...] = x_ref[...] + route[:, 2:3] * got_ref[slot, 0] + route[:, 3:4] * got_ref[slot, 1]


def _moe_combine(x, route, dest, ys, tm):
    t, d = x.shape
    nt = t // tm
    idx = lambda nxt: pl.BlockSpec((1, 1, TOP_K * tm), lambda i: (jnp.minimum(i + nxt, nt - 1), 0, 0),
                                   memory_space=pltpu.SMEM)
    return pl.pallas_call(
        functools.partial(_combine_kernel, tm=tm), grid=(nt,),
        in_specs=[idx(0), idx(1), pl.BlockSpec((tm, d), lambda i: (i, 0)), pl.BlockSpec((tm, LANE), lambda i: (i, 0)),
                  pl.BlockSpec(memory_space=pl.ANY)],
        out_specs=pl.BlockSpec((tm, d), lambda i: (i, 0)),
        out_shape=jax.ShapeDtypeStruct((t, d), F32),
        scratch_shapes=[pltpu.VMEM((2, TOP_K, tm, d), F32), pltpu.SemaphoreType.DMA((2,))],
        compiler_params=_params("arbitrary"), name="moe_combine",
    )(dest, dest, x, route, ys)


def _moe(x, gain, w_router, b_router, w_gate, w_up, w_down, rb=512, tm=256):
    t, d = x.shape
    route, counts = _moe_router(x, gain, w_router, b_router)
    n_blk = (t * TOP_K + rb - 1) // rb + N_EXPERTS
    padded = (counts + rb - 1) // rb * rb
    pad_end = jnp.cumsum(padded)
    pad_start = pad_end - padded
    dest = pad_start[route[:, 0:2].astype(jnp.int32)] + route[:, 4:6].astype(jnp.int32)
    dest_t = jnp.concatenate([dest[:, 0].reshape(t // tm, 1, tm), dest[:, 1].reshape(t // tm, 1, tm)], axis=2)
    fill = jnp.concatenate([(pad_start + counts) // 8 * 8, pad_end[-1:], pad_end, jnp.full((1,), n_blk * rb, jnp.int32)])
    n_used = pad_end[-1:] // rb
    blk = jnp.minimum(jnp.arange(n_blk, dtype=jnp.int32), n_used - 1) * rb
    blk_expert = jnp.sum(pad_end[None, :] <= blk[:, None], axis=1).astype(jnp.int32)
    xs = _moe_scatter(x, gain, dest_t, fill.astype(jnp.int32), n_blk * rb, tm, rb)
    ys = _moe_experts(xs, blk_expert, n_used.astype(jnp.int32), w_gate, w_up, w_down, rb)
    return _moe_combine(x, route, dest_t, ys, tm)


def kernel(x, rel_bias, ev_mix_norm, ev_w_in, nsa_q_norm, nsa_k_norm, nsa_cmp_pos, nsa_cmp_w1, nsa_cmp_w2, mla_cq_norm, mla_ckv_norm, mla_w_uq, mla_w_ukv, mla_q_norm, mla_k_norm, ev_w_out, ev_ffn_norm, ffn_w_gate, ffn_w_up, ffn_w_down, od_mix_norm, od_w_in, conv_w, od_w_out, od_ffn_norm, moe_w_router, moe_b_router, moe_w_gate, moe_w_up, moe_w_down):
    bsz, seq, d = x.shape
    depth = ev_mix_norm.shape[0] + od_mix_norm.shape[0]
    pad = 512
    for layer in range(depth):
        i = layer // 2
        if layer % 2 == 0:
            qn, kv, kc_raw, vc_raw, gates, qm, km, vm = _even_in_proj(
                x, ev_mix_norm[i], ev_w_in[i], nsa_q_norm[i], nsa_k_norm[i], mla_cq_norm[i], mla_ckv_norm[i],
                mla_w_uq[i], mla_w_ukv[i], mla_q_norm[i], mla_k_norm[i], tm=pad)
            kc = _compress(kc_raw, nsa_cmp_pos[i, 0], nsa_cmp_w1[i, 0], nsa_cmp_w2[i, 0], nsa_k_norm[i, 0], True)
            vc = _compress(vc_raw, nsa_cmp_pos[i, 1], nsa_cmp_w1[i, 1], nsa_cmp_w2[i, 1], nsa_k_norm[i, 0], False)
            o_nsa = _nsa_attention(qn, kv, kc, vc, gates, rel_bias, pad)
            o_mla = _mla_attention(qm, km, vm)
            x = _even_out_ffn(x.reshape(bsz * seq, d), o_nsa.reshape(bsz * seq, -1), o_mla.reshape(bsz * seq, -1),
                              ev_w_out[i], ev_ffn_norm[i], ffn_w_gate[i], ffn_w_up[i], ffn_w_down[i]).reshape(bsz, seq, d)
        else:
            x = _conv_mixer(x, od_mix_norm[i], od_w_in[i], conv_w[i], od_w_out[i])
            x = _moe(x.reshape(bsz * seq, d), od_ffn_norm[i], moe_w_router[i], moe_b_router[i],
                     moe_w_gate[i], moe_w_up[i], moe_w_down[i]).reshape(bsz, seq, d)
    return x
```

```python
import functools
import math

import jax
import jax.numpy as jnp
import numpy as np
from jax import lax
from jax.experimental import pallas as pl
from jax.experimental.pallas import tpu as pltpu

F32 = jnp.float32
BF16 = jnp.bfloat16

EPS = 1e-6
NEG = -1e30
FORCED_SCORE = 1e9
NSA_HEADS = 8
NSA_GROUPS = 2
NSA_REP = NSA_HEADS // NSA_GROUPS
NSA_DK = 64
CMP_BLOCK = 32
CMP_STRIDE = 16
CMP_HIDDEN = 256
SEL_BLOCK = 64
SEL_TOP_N = 16
WINDOW = 512
MLA_HEADS = 8
MLA_Q_RANK = 256
MLA_KV_RANK = 128
MLA_NOPE = 64
MLA_ROPE = 32
MLA_V = 64
ROPE_THETA = 10000.0
REL_BUCKETS = 32
REL_MAX_DIST = 128
CONV_WIDTH = 3
N_EXPERTS = 8
TOP_K = 2
EVEN_IN_SIZES = (512,) + (128,) * 6 + (24, 256, 128, 32)

LANE = 128
VMEM_LIMIT = 56 * 1024 * 1024
SEL_MASK = -30000.0
FAR_DIST = 128

_Q0, _KV0, _KC0, _VC0, _GT0, _CQ0, _CKV0, _KR0, _EVEN_W = 0, 1024, 2048, 2176, 2304, 2560, 2816, 2944, 3072


def _dot(a, b):
    return jnp.dot(a, b, preferred_element_type=F32)


def _dot_nt(a, b):
    return lax.dot_general(a, b, (((1,), (1,)), ((), ())), preferred_element_type=F32)


def _rms(x, gain, n):
    ss = jnp.sum(x * x, axis=-1, keepdims=True) * (1.0 / n)
    return x * lax.rsqrt(ss + EPS) * gain


def _params(*sem):
    return pltpu.CompilerParams(dimension_semantics=sem, vmem_limit_bytes=VMEM_LIMIT)


def _const_spec(shape):
    nd = len(shape)
    return pl.BlockSpec(shape, lambda *_: (0,) * nd, pipeline_mode=pl.Buffered(1))


def _rope(x, cos, sa, sb):
    return x * cos + pltpu.roll(x, 16, 1) * sa + pltpu.roll(x, LANE - 16, 1) * sb


def _even_in_kernel(x_ref, gmix_ref, w_ref, gq_ref, gks_ref, gkw_ref, gcq_ref, gckv_ref,
                    wuq_ref, wuk_ref, wuv_ref, gmq_ref, gmk_ref, cos_ref, sa_ref, sb_ref,
                    qn_ref, kv_ref, kc_ref, vc_ref, gate_ref, qm_ref, km_ref, vm_ref, *, tm):
    j = pl.program_id(1)

    @pl.when(j == 0)
    def _():
        kv_ref[...] = jnp.zeros_like(kv_ref)
        qn_ref[...] = jnp.zeros_like(qn_ref)
        kc_ref[...] = jnp.zeros_like(kc_ref)
        vc_ref[...] = jnp.zeros_like(vc_ref)
        gate_ref[...] = jnp.zeros_like(gate_ref)
        qm_ref[...] = jnp.zeros_like(qm_ref)
        km_ref[...] = jnp.zeros_like(km_ref)
        vm_ref[...] = jnp.zeros_like(vm_ref)

    @pl.when(j > 0)
    def _():
        xn = _rms(x_ref[...], gmix_ref[...], x_ref.shape[-1]).astype(BF16)
        hq = _dot(xn, w_ref[:, _Q0:_Q0 + 1024])
        gq = gq_ref[...]
        for h in range(NSA_HEADS):
            seg = hq[:, LANE * h:LANE * (h + 1)]
            qn_ref[:, LANE * h:LANE * (h + 1)] = (_rms(seg, gq, NSA_DK) * NSA_DK ** -0.5).astype(BF16)
        hkv = _dot(xn, w_ref[:, _KV0:_KV0 + 1024])
        pos = (j - 1) * tm + lax.broadcasted_iota(jnp.int32, (tm, LANE), 0)
        lane = lax.broadcasted_iota(jnp.int32, (tm, LANE), 1)
        onehot = jnp.where(lane - NSA_DK == pos // SEL_BLOCK, 1.0, 0.0)
        ones = jnp.where(lane >= NSA_DK, 1.0, 0.0)
        for g in range(NSA_GROUPS):
            o = 4 * LANE * g
            kv_ref[:, o:o + LANE] = (_rms(hkv[:, o:o + LANE], gks_ref[...], NSA_DK) + onehot).astype(BF16)
            kv_ref[:, o + LANE:o + 2 * LANE] = (hkv[:, o + LANE:o + 2 * LANE] + ones).astype(BF16)
            kv_ref[:, o + 2 * LANE:o + 3 * LANE] = _rms(hkv[:, o + 2 * LANE:o + 3 * LANE], gkw_ref[...],
                                                        NSA_DK).astype(BF16)
            kv_ref[:, o + 3 * LANE:o + 4 * LANE] = (hkv[:, o + 3 * LANE:o + 4 * LANE] + ones).astype(BF16)
        hc = _dot(xn, w_ref[:, _KC0:_KC0 + 256])
        kc_ref[...] = hc[:, :LANE].astype(BF16)
        vc_ref[...] = hc[:, LANE:].astype(BF16)
        gate_ref[...] = jax.nn.sigmoid(_dot(xn, w_ref[:, _GT0:_GT0 + 256]))
        hm = _dot(xn, w_ref[:, _CQ0:_EVEN_W])
        cq = _rms(hm[:, :MLA_Q_RANK], gcq_ref[...], MLA_Q_RANK).astype(BF16)
        ckv = _rms(hm[:, MLA_Q_RANK:MLA_Q_RANK + MLA_KV_RANK], gckv_ref[...], MLA_KV_RANK).astype(BF16)
        k_rope = hm[:, MLA_Q_RANK + MLA_KV_RANK:]
        qm = _dot(cq, wuq_ref[...])
        kn = _dot(ckv, wuk_ref[...])
        vm_ref[...] = _dot(ckv, wuv_ref[...]).astype(BF16)
        cos, sa, sb = cos_ref[...], sa_ref[...], sb_ref[...]
        dqk = MLA_NOPE + MLA_ROPE
        for h in range(MLA_HEADS):
            sl = slice(LANE * h, LANE * (h + 1))
            qh = _rope(_rms(qm[:, sl], gmq_ref[...], dqk), cos, sa, sb) * dqk ** -0.5
            qm_ref[:, sl] = qh.astype(BF16)
            kh = _rope(_rms(kn[:, sl] + k_rope, gmk_ref[...], dqk), cos, sa, sb)
            km_ref[:, sl] = kh.astype(BF16)


def _even_in_weights(w_in):
    offs = np.concatenate([[0], np.cumsum(EVEN_IN_SIZES)])
    part = [w_in[:, offs[n]:offs[n + 1]] for n in range(len(EVEN_IN_SIZES))]
    q, k_c, v_c, k_s, v_s, k_w, v_w, gates, c_q, c_kv, k_rope = part
    d = w_in.shape[0]
    z = lambda n: jnp.zeros((d, n), w_in.dtype)
    cols = []
    for h in range(NSA_HEADS):
        cols += [q[:, 64 * h:64 * h + 64], z(64)]
    for g in range(NSA_GROUPS):
        s = slice(64 * g, 64 * g + 64)
        cols += [k_s[:, s], z(64), v_s[:, s], z(64), k_w[:, s], z(64), v_w[:, s], z(64)]
    cols += [k_c, v_c]
    for g in range(NSA_GROUPS):
        cols += [gates[:, 12 * g:12 * g + 12], z(LANE - 12)]
    cols += [c_q, c_kv, z(64), k_rope, z(32)]
    w = jnp.concatenate(cols, axis=1)
    assert w.shape[1] == _EVEN_W
    return w.astype(BF16)


def _pad_gain(g, width):
    return jnp.pad(g.astype(F32), (0, width - g.shape[0]))[None, :]


def _rope_tables(seq):
    half = MLA_ROPE // 2
    inv_freq = ROPE_THETA ** (-jnp.arange(half, dtype=F32) / half)
    ang = jnp.arange(seq).astype(F32)[:, None] * inv_freq[None, :]
    cos, sin = jnp.cos(ang), jnp.sin(ang)
    one = jnp.ones((seq, MLA_NOPE), F32)
    zn = jnp.zeros((seq, MLA_NOPE), F32)
    zt = jnp.zeros((seq, LANE - MLA_NOPE - MLA_ROPE), F32)
    zh = jnp.zeros((seq, half), F32)
    cos_t = jnp.concatenate([one, cos, cos, zt + 1.0], axis=1)
    sa = jnp.concatenate([zn, zh, sin, zt], axis=1)
    sb = jnp.concatenate([zn, -sin, zh, zt], axis=1)
    return cos_t, sa, sb


def _even_in_proj(x, gmix, w_in, q_norm, k_norm, cq_norm, ckv_norm, w_uq, w_ukv, mq_norm, mk_norm, tm=512):
    bsz, seq, d = x.shape
    nt = seq // tm
    w = _even_in_weights(w_in)
    dqk = MLA_NOPE + MLA_ROPE
    wuq = jnp.concatenate(
        [jnp.pad(w_uq[:, dqk * h:dqk * (h + 1)], ((0, 0), (0, LANE - dqk))) for h in range(MLA_HEADS)],
        axis=1).astype(BF16)
    kvw = MLA_NOPE + MLA_V
    wuk = jnp.concatenate(
        [jnp.pad(w_ukv[:, kvw * h:kvw * h + MLA_NOPE], ((0, 0), (0, LANE - MLA_NOPE))) for h in range(MLA_HEADS)],
        axis=1).astype(BF16)
    wuv = jnp.concatenate([w_ukv[:, kvw * h + MLA_NOPE:kvw * (h + 1)] for h in range(MLA_HEADS)], axis=1).astype(BF16)
    cos_t, sa, sb = _rope_tables(seq)
    tok = lambda width: pl.BlockSpec((None, tm, width), lambda b, j: (b, jnp.maximum(j - 1, 0), 0))
    postab = pl.BlockSpec((tm, LANE), lambda b, j: (jnp.maximum(j - 1, 0), 0))
    in_specs = [tok(d), _const_spec((1, d)), _const_spec((d, _EVEN_W)),
                _const_spec((1, LANE)), _const_spec((1, LANE)), _const_spec((1, LANE)),
                _const_spec((1, MLA_Q_RANK)), _const_spec((1, MLA_KV_RANK)),
                _const_spec(wuq.shape), _const_spec(wuk.shape), _const_spec(wuv.shape),
                _const_spec((1, LANE)), _const_spec((1, LANE)), postab, postab, postab]
    out_shape = [jax.ShapeDtypeStruct((bsz, seq, 1024), BF16),
                 jax.ShapeDtypeStruct((bsz, seq + tm, 1024), BF16),
                 jax.ShapeDtypeStruct((bsz, seq, LANE), BF16),
                 jax.ShapeDtypeStruct((bsz, seq, LANE), BF16),
                 jax.ShapeDtypeStruct((bsz, seq, 2 * LANE), F32),
                 jax.ShapeDtypeStruct((bsz, seq, 1024), BF16),
                 jax.ShapeDtypeStruct((bsz, seq, 1024), BF16),
                 jax.ShapeDtypeStruct((bsz, seq, 512), BF16)]
    out_specs = [tok(1024), pl.BlockSpec((None, tm, 1024), lambda b, j: (b, j, 0)), tok(LANE), tok(LANE),
                 tok(2 * LANE), tok(1024), tok(1024), tok(512)]
    return pl.pallas_call(
        functools.partial(_even_in_kernel, tm=tm), grid=(bsz, nt + 1), in_specs=in_specs, out_specs=out_specs,
        out_shape=out_shape, compiler_params=_params("parallel", "arbitrary"), name="even_in_proj",
    )(x, gmix[None, :], w, _pad_gain(q_norm, LANE), _pad_gain(k_norm[1], LANE), _pad_gain(k_norm[2], LANE),
      cq_norm[None, :], ckv_norm[None, :], wuq, wuk, wuv, _pad_gain(mq_norm, LANE), _pad_gain(mk_norm, LANE),
      cos_t, sa, sb)


def _cmp_kernel(x_ref, wa_ref, wb_ref, pa_ref, pb_ref, w2_ref, gain_ref, o_ref, *, normalize):
    x = x_ref[...]
    ua = _dot(x, wa_ref[...])
    ub = _dot(x, wb_ref[...])
    pt = _dot(pa_ref[...], wa_ref[...]) + _dot(pb_ref[...], wb_ref[...])
    n = ub.shape[0]
    pre = ua + pltpu.roll(ub, n - 1, 0) + pt[0:1]
    hid = jax.nn.gelu(pre).astype(BF16)
    out = _dot(hid, w2_ref[...])
    if normalize:
        gain = gain_ref[...]
        for g in range(NSA_GROUPS):
            sl = slice(LANE * g, LANE * (g + 1))
            o_ref[:, sl] = _rms(out[:, sl], gain, NSA_DK).astype(BF16)
    else:
        o_ref[...] = out.astype(BF16)


def _compress(kv, pos_emb, w1, w2, gain, normalize):
    bsz, seq, _ = kv.shape
    half = CMP_BLOCK // 2
    assert CMP_STRIDE == half
    nrow = seq // CMP_STRIDE
    x = kv.reshape(bsz, nrow, CMP_STRIDE * LANE)
    w1r = w1.reshape(CMP_BLOCK, NSA_DK, CMP_HIDDEN)
    eye = jnp.eye(NSA_GROUPS, dtype=w1.dtype)
    widen = lambda w: jnp.einsum("ldn,gh->lgdhn", w, eye).reshape(half * LANE, NSA_GROUPS * CMP_HIDDEN).astype(BF16)
    wa, wb = widen(w1r[:half]), widen(w1r[half:])
    prow = lambda p: jnp.pad(jnp.broadcast_to(p[:, None, :], (half, NSA_GROUPS, NSA_DK)).reshape(1, half * LANE),
                             ((0, 15), (0, 0))).astype(BF16)
    pa, pb = prow(pos_emb[:half]), prow(pos_emb[half:])
    second = jnp.zeros_like(w2) if normalize else w2
    w2w = jnp.einsum("nd,gh->gnhd", jnp.concatenate([w2, second], axis=1), eye)
    w2w = w2w.reshape(NSA_GROUPS * CMP_HIDDEN, NSA_GROUPS * LANE).astype(BF16)
    return pl.pallas_call(
        functools.partial(_cmp_kernel, normalize=normalize), grid=(bsz,),
        in_specs=[pl.BlockSpec((None, nrow, CMP_STRIDE * LANE), lambda b: (b, 0, 0)),
                  _const_spec(wa.shape), _const_spec(wb.shape), _const_spec(pa.shape), _const_spec(pb.shape),
                  _const_spec(w2w.shape), _const_spec((1, LANE))],
        out_specs=pl.BlockSpec((None, nrow, NSA_GROUPS * LANE), lambda b: (b, 0, 0)),
        out_shape=jax.ShapeDtypeStruct((bsz, nrow, NSA_GROUPS * LANE), BF16),
        compiler_params=_params("parallel"), name="nsa_compress",
    )(x, wa, wb, pa, pb, w2w, _pad_gain(gain, LANE))


def _bucket_table():
    dist = np.arange(FAR_DIST + 1)
    max_exact = REL_BUCKETS // 2
    nf = np.maximum(dist, max_exact).astype(np.float32)
    large = max_exact + (np.log(nf / max_exact) / math.log(REL_MAX_DIST / max_exact)
                         * (REL_BUCKETS - max_exact)).astype(np.int32)
    return np.where(dist < max_exact, dist, np.minimum(large, REL_BUCKETS - 1))


CMP_NEAR = 32


def _expand(tbl, idx):
    idx = np.asarray(idx)
    onehot = jnp.asarray(np.eye(tbl.shape[1], dtype=np.float32)[idx.reshape(-1)])
    out = lax.dot_general(tbl, onehot, (((1,), (1,)), ((), ())), precision=lax.Precision.HIGHEST)
    return out.reshape((tbl.shape[0],) + idx.shape)


def _toeplitz_tile(tbl, far, width, dist0, valid):
    period = width + SEL_BLOCK + 1
    j = np.arange(period)
    dist = dist0 - np.where(j < width, j, j - period)
    gen = jnp.where(jnp.asarray(valid(dist))[None], _expand(tbl, np.clip(dist, 0, FAR_DIST)) - far, NEG)
    flat = jnp.tile(gen, (1, SEL_BLOCK))[:, :SEL_BLOCK * (period - 1)]
    return flat.reshape(tbl.shape[0], SEL_BLOCK, period - 1)[:, :, :width]


def _bias_tables(rel_bias):
    tbl = _expand(rel_bias.astype(F32).T, _bucket_table())
    far = tbl[:, FAR_DIST:]
    bias_near = _toeplitz_tile(tbl, far, 4 * SEL_BLOCK, 3 * SEL_BLOCK, lambda d: d >= 0)
    bias_win = _toeplitz_tile(tbl, 0.0, WINDOW + SEL_BLOCK, WINDOW, lambda d: (d >= 0) & (d < WINDOW))
    d_c = (np.arange(SEL_BLOCK)[:, None] + (CMP_NEAR // 2) * CMP_STRIDE - (CMP_BLOCK - 1)
           - CMP_STRIDE * np.arange(CMP_NEAR)[None, :])
    assert d_c[:, 0].min() >= FAR_DIST and d_c[:, -1].max() < 0
    bias_cmp = jnp.where(jnp.asarray(d_c >= 0)[None], _expand(tbl, np.clip(d_c, 0, FAR_DIST)) - far[:, :, None], 0.0)
    return bias_cmp, bias_near, bias_win


def _cmp_mask_dist(seq):
    a = np.tile(np.arange(SEL_BLOCK), NSA_REP)[:, None]
    c = np.arange(seq // CMP_STRIDE)[None, :]
    return jnp.asarray(c * CMP_STRIDE + CMP_BLOCK - 1 - a, jnp.int32)


def _overlap_t(seq):
    ncp = seq // CMP_STRIDE
    cs = np.arange(ncp)[None, :] * CMP_STRIDE
    ss = np.arange(SEL_BLOCK)[:, None] * SEL_BLOCK
    ov = (cs < ss + SEL_BLOCK) & (cs + CMP_BLOCK - 1 >= ss) & (np.arange(ncp)[None, :] < (seq - CMP_BLOCK) // CMP_STRIDE + 1)
    return jnp.asarray(ov, BF16)


def _softmax_step(s, v, state):
    m, l, acc = state
    mn = jnp.maximum(m, jnp.max(s, axis=-1, keepdims=True))
    alpha = jnp.exp(m - mn)
    p = jnp.exp(s - mn)
    return mn, alpha * l + jnp.sum(p, axis=-1, keepdims=True), alpha * acc + _dot(p.astype(BF16), v)


FAR_CHUNK = 1024


def _lane_fold(x, op):
    out = x[:, 0:LANE]
    for c in range(1, x.shape[1] // LANE):
        out = op(out, x[:, LANE * c:LANE * (c + 1)])
    return out


SEL_TILE = 8


def _nsa_local_kernel(q_ref, kv_ref, kc_ref, vc_ref, g_ref, bc_ref, bw_ref, ovt_ref, dmask_ref, neg_ref, oc_ref, *, pad):
    t = pl.program_id(1)
    qb, rep, ng = SEL_BLOCK, NSA_REP, NSA_GROUPS
    rows = qb * rep
    gw = 4 * LANE
    ncp = kc_ref.shape[0]
    wk = WINDOW + qb
    m_i = lax.broadcasted_iota(jnp.int32, (CMP_NEAR, ncp), 0)
    c_i = lax.broadcasted_iota(jnp.int32, (CMP_NEAR, ncp), 1)
    w_lane = lax.broadcasted_iota(jnp.int32, (rows, wk), 1)
    dist = dmask_ref[...]
    ovt = ovt_ref[...]
    gates = g_ref[...]
    low = lax.broadcasted_iota(jnp.int32, (qb, LANE), 1) < NSA_DK
    jj = lax.broadcasted_iota(jnp.int32, (qb, ng * qb), 0)
    bases = []
    for g in range(ng):
        base = bc_ref[rep * g:rep * (g + 1)].reshape(rows, CMP_NEAR)
        b_hi = base.astype(BF16)
        bases.append((b_hi, (base - b_hi.astype(F32)).astype(BF16)))
    blocks = [(u, t * SEL_TILE + u, slice(qb * u, qb * (u + 1))) for u in range(SEL_TILE)]
    chains = [(u, i, tok, g) for u, i, tok in blocks for g in range(ng)]
    qq = [jnp.concatenate([q_ref[tok, gw * g + LANE * r:gw * g + LANE * (r + 1)] for r in range(rep)], axis=0)
          for _, _, tok, g in chains]
    shift = [jnp.where(c_i - m_i == (qb // CMP_STRIDE) * i - CMP_NEAR // 2, 1.0, 0.0).astype(BF16) for _, i, _ in blocks]
    s_c = [jnp.where(dist <= i * qb,
                     _dot_nt(qq[n], kc_ref[:, LANE * g:LANE * (g + 1)]) + _dot(bases[g][0], shift[u])
                     + _dot(bases[g][1], shift[u]), NEG) for n, (u, i, _, g) in enumerate(chains)]
    s_w = [jnp.where(w_lane + i * qb - WINDOW >= 0,
                     _dot_nt(qq[n], kv_ref[pl.ds(pl.multiple_of(pad - WINDOW + i * qb, qb), wk),
                                           gw * g + 2 * LANE:gw * g + 3 * LANE])
                     + bw_ref[rep * g:rep * (g + 1)].reshape(rows, wk), NEG) for n, (u, i, _, g) in enumerate(chains)]
    e_c = [jnp.exp(s - jnp.maximum(jnp.max(s, axis=-1, keepdims=True), -1e20)) for s in s_c]
    e_w = [jnp.exp(s - jnp.max(s, axis=-1, keepdims=True)) for s in s_w]
    inv = [1.0 / jnp.maximum(jnp.sum(e, axis=-1, keepdims=True), 1e-30) for e in e_c]
    o_c = [_dot(e_c[n].astype(BF16), vc_ref[:, LANE * g:LANE * (g + 1)]) * inv[n] for n, (_, _, _, g) in enumerate(chains)]
    pv_w = [_dot(e_w[n].astype(BF16), kv_ref[pl.ds(pl.multiple_of(pad - WINDOW + i * qb, qb), wk),
                                            gw * g + 3 * LANE:gw * g + 4 * LANE]) for n, (_, i, _, g) in enumerate(chains)]
    o_w = [pv / pv[:, NSA_DK:NSA_DK + 1] for pv in pv_w]
    for n, (_, _, tok, g) in enumerate(chains):
        part = []
        for r in range(rep):
            sl = slice(qb * r, qb * (r + 1))
            c0 = LANE * g + 3 * r
            part.append(gates[tok, c0:c0 + 1] * o_c[n][sl] + gates[tok, c0 + 2:c0 + 3] * o_w[n][sl])
        for pr in range(rep // 2):
            c0 = 2 * LANE * g + LANE * pr
            oc_ref[tok, c0:c0 + LANE] = jnp.where(low, part[2 * pr], pltpu.roll(part[2 * pr + 1], NSA_DK, 1))
    imps = []
    for u, i, _ in blocks:
        p = [e_c[ng * u + g] * inv[ng * u + g] for g in range(ng)]
        psum = jnp.concatenate([x[0:qb] + x[qb:2 * qb] + x[2 * qb:3 * qb] + x[3 * qb:4 * qb] for x in p], axis=0)
        hi = psum.astype(BF16)
        r1 = psum - hi.astype(F32)
        mid = r1.astype(BF16)
        lo = (r1 - mid.astype(F32)).astype(BF16)
        imp = _dot_nt(ovt, hi) + _dot_nt(ovt, mid) + _dot_nt(ovt, lo)
        imp = jnp.where((jj == i) | (jj == 0), FORCED_SCORE, imp)
        imps.append(jnp.where(jj > i, -1.0, imp))
    grp = [[imp[8 * a:8 * a + 8] for a in range(8)] for imp in imps]
    cnt = [[jnp.zeros((8, ng * qb), F32) for _ in range(8)] for _ in imps]
    sub = lax.broadcasted_iota(jnp.int32, (8, ng * qb), 0)
    for k in range(qb):
        for u in range(SEL_TILE):
            rk = imps[u][k:k + 1, :]
            for a in range(8):
                if 8 * a + 7 <= k:
                    cnt[u][a] = cnt[u][a] + jnp.where(rk > grp[u][a], 1.0, 0.0)
                elif 8 * a > k:
                    cnt[u][a] = cnt[u][a] + jnp.where(rk >= grp[u][a], 1.0, 0.0)
                else:
                    cnt[u][a] = cnt[u][a] + jnp.where(sub + 8 * a > k, jnp.where(rk >= grp[u][a], 1.0, 0.0),
                                                      jnp.where(rk > grp[u][a], 1.0, 0.0))
    for u in range(SEL_TILE):
        neg_ref[u] = jnp.where(jnp.concatenate(cnt[u], axis=0) < SEL_TOP_N, 0.0, SEL_MASK)


def _nsa_attend_kernel(q_ref, kv_ref, g_ref, neg_ref, oc_ref, bn_ref, o_ref, s_ref, *, pad):
    i = pl.program_id(1)
    qb, rep, ng = SEL_BLOCK, NSA_REP, NSA_GROUPS
    rows = qb * rep
    gw = 4 * LANE
    qs = [jnp.concatenate([q_ref[:, gw * g + LANE * r:gw * g + LANE * (r + 1)] for r in range(rep)], axis=0)
          for g in range(ng)]

    jj = lax.broadcasted_iota(jnp.int32, (qb, ng * qb), 0)
    neg = neg_ref[...]
    neg_far = jnp.where(jj >= i - 3, SEL_MASK, neg)
    zero = jnp.zeros((qb, ng * qb), BF16)
    ext_near = jnp.concatenate([zero, neg.astype(BF16)], axis=0)
    ext_far = jnp.concatenate([zero, neg_far.astype(BF16)], axis=0)
    q_lane = lax.broadcasted_iota(jnp.int32, (qb, ng * qb), 1)
    q_near, q_far = [], []
    for g in range(ng):
        pick = jnp.where(q_lane == jj + qb * g, 1.0, 0.0).astype(BF16)
        q_near.append(qs[g] + jnp.concatenate([_dot_nt(pick, ext_near).astype(BF16)] * rep, axis=0))
        q_far.append(qs[g] + jnp.concatenate([_dot_nt(pick, ext_far).astype(BF16)] * rep, axis=0))

    st_n = pl.multiple_of(pad + (i - 3) * qb, qb)
    in_seq = lax.broadcasted_iota(jnp.int32, (rows, 4 * qb), 1) + (i - 3) * qb >= 0
    s_near = [jnp.where(in_seq, _dot_nt(q_near[g], kv_ref[pl.ds(st_n, 4 * qb), gw * g:gw * g + LANE])
                        + bn_ref[rep * g:rep * (g + 1)].reshape(rows, 4 * qb), NEG) for g in range(ng)]

    nch = (jnp.maximum(i - 3, 0) * qb + FAR_CHUNK - 1) // FAR_CHUNK

    def pass1(c, mx):
        st = pl.multiple_of(pad + c * FAR_CHUNK, math.gcd(pad, FAR_CHUNK))
        col = pl.multiple_of(c * FAR_CHUNK, FAR_CHUNK)
        out = []
        for g in range(ng):
            s = _dot_nt(q_far[g], kv_ref[pl.ds(st, FAR_CHUNK), gw * g:gw * g + LANE])
            s_ref[g, :, pl.ds(col, FAR_CHUNK)] = s
            out.append(jnp.maximum(mx[g], _lane_fold(s, jnp.maximum)))
        return tuple(out)

    mx = lax.fori_loop(0, nch, pass1, tuple(jnp.full((rows, LANE), NEG, F32) for _ in range(ng)))
    ms = [jnp.max(jnp.maximum(mx[g], _lane_fold(s_near[g], jnp.maximum)), axis=-1, keepdims=True) for g in range(ng)]

    def pass2(c, state):
        st = pl.multiple_of(pad + c * FAR_CHUNK, math.gcd(pad, FAR_CHUNK))
        col = pl.multiple_of(c * FAR_CHUNK, FAR_CHUNK)
        out = []
        for g in range(ng):
            p = jnp.exp(s_ref[g, :, pl.ds(col, FAR_CHUNK)] - ms[g])
            out.append(state[g] + _dot(p.astype(BF16), kv_ref[pl.ds(st, FAR_CHUNK), gw * g + LANE:gw * g + 2 * LANE]))
        return tuple(out)

    far = lax.fori_loop(0, nch, pass2, tuple(jnp.zeros((rows, LANE), F32) for _ in range(ng)))
    o_s = []
    for g in range(ng):
        p = jnp.exp(s_near[g] - ms[g])
        pv = far[g] + _dot(p.astype(BF16), kv_ref[pl.ds(st_n, 4 * qb), gw * g + LANE:gw * g + 2 * LANE])
        o_s.append(pv / pv[:, NSA_DK:NSA_DK + 1])

    gates = g_ref[...]
    low = lax.broadcasted_iota(jnp.int32, (qb, LANE), 1) < NSA_DK
    for g in range(ng):
        outs = [gates[:, LANE * g + 3 * r + 1:LANE * g + 3 * r + 2] * o_s[g][qb * r:qb * (r + 1)] for r in range(rep)]
        for pr in range(rep // 2):
            c0 = 2 * LANE * g + LANE * pr
            pair = jnp.where(low, outs[2 * pr], pltpu.roll(outs[2 * pr + 1], NSA_DK, 1))
            o_ref[:, c0:c0 + LANE] = (oc_ref[:, c0:c0 + LANE] + pair).astype(o_ref.dtype)


def _nsa_attention(qn, kv, kc, vc, gates, rel_bias, pad):
    bsz, seq, _ = qn.shape
    qb, ng = SEL_BLOCK, NSA_GROUPS
    nq = seq // qb
    ncp = seq // CMP_STRIDE
    assert nq <= qb and nq % SEL_TILE == 0 and pad >= WINDOW and seq % FAR_CHUNK == 0
    bias_cmp, bias_near, bias_win = _bias_tables(rel_bias)
    ovt = _overlap_t(seq)
    dmask = _cmp_mask_dist(seq)
    width = NSA_HEADS * NSA_DK
    tile = lambda n, w: pl.BlockSpec((None, n, w), lambda b, i: (b, i, 0))
    per_b = lambda n, w: pl.BlockSpec((None, n, w), lambda b, i: (b, 0, 0))
    neg, o_local = pl.pallas_call(
        functools.partial(_nsa_local_kernel, pad=pad), grid=(bsz, nq // SEL_TILE),
        in_specs=[tile(SEL_TILE * qb, ng * 4 * LANE), per_b(seq + pad, ng * 4 * LANE), per_b(ncp, ng * LANE),
                  per_b(ncp, ng * LANE), tile(SEL_TILE * qb, ng * LANE), _const_spec(bias_cmp.shape),
                  _const_spec(bias_win.shape), _const_spec(ovt.shape), _const_spec(dmask.shape)],
        out_specs=[pl.BlockSpec((None, SEL_TILE, qb, ng * qb), lambda b, i: (b, i, 0, 0)), tile(SEL_TILE * qb, width)],
        out_shape=[jax.ShapeDtypeStruct((bsz, nq, qb, ng * qb), F32), jax.ShapeDtypeStruct((bsz, seq, width), F32)],
        compiler_params=_params("parallel", "arbitrary"), name="nsa_local",
    )(qn, kv, kc, vc, gates, bias_cmp, bias_win, ovt, dmask)
    return pl.pallas_call(
        functools.partial(_nsa_attend_kernel, pad=pad), grid=(bsz, nq),
        in_specs=[tile(qb, ng * 4 * LANE), per_b(seq + pad, ng * 4 * LANE), tile(qb, ng * LANE),
                  pl.BlockSpec((None, None, qb, ng * qb), lambda b, i: (b, i, 0, 0)), tile(qb, width),
                  _const_spec(bias_near.shape)],
        out_specs=tile(qb, width),
        out_shape=jax.ShapeDtypeStruct((bsz, seq, width), BF16),
        scratch_shapes=[pltpu.VMEM((ng, NSA_REP * qb, seq), F32)],
        compiler_params=_params("parallel", "arbitrary"), name="nsa_attention",
    )(qn, kv, gates, neg, o_local, bias_near)


def _mla_attn_kernel(q_ref, k_ref, v_ref, o_ref, *, tq, ck):
    i = pl.program_id(2)
    heads = [slice(LANE * hh, LANE * (hh + 1)) for hh in range(2)]
    qs = [q_ref[:, sl] for sl in heads]

    def body(c, states):
        st = pl.multiple_of(c * ck, ck)
        v = v_ref[pl.ds(st, ck), :]
        return tuple(_softmax_step(_dot_nt(q, k_ref[pl.ds(st, ck), sl]), v, state)
                     for q, sl, state in zip(qs, heads, states))

    init = (jnp.full((tq, 1), NEG, F32), jnp.zeros((tq, 1), F32), jnp.zeros((tq, LANE), F32))
    states = lax.fori_loop(0, i * (tq // ck), body, (init, init))
    st = pl.multiple_of(i * tq, tq)
    causal = lax.broadcasted_iota(jnp.int32, (tq, tq), 0) >= lax.broadcasted_iota(jnp.int32, (tq, tq), 1)
    outs = []
    for q, sl, state in zip(qs, heads, states):
        s = _dot_nt(q, k_ref[pl.ds(st, tq), sl])
        _, l, acc = _softmax_step(jnp.where(causal, s, NEG), v_ref[pl.ds(st, tq), :], state)
        outs.append(acc / l)
    low = lax.broadcasted_iota(jnp.int32, (tq, LANE), 1) < MLA_V
    o_ref[...] = jnp.where(low, outs[0], outs[1]).astype(o_ref.dtype)


def _mla_attention(qm, km, vm, tq=1024, ck=512):
    bsz, seq, _ = qm.shape
    tq = min(tq, seq)
    return pl.pallas_call(
        functools.partial(_mla_attn_kernel, tq=tq, ck=min(ck, tq)), grid=(bsz, MLA_HEADS // 2, seq // tq),
        in_specs=[pl.BlockSpec((None, tq, 2 * LANE), lambda b, h, i: (b, i, h)),
                  pl.BlockSpec((None, seq, 2 * LANE), lambda b, h, i: (b, 0, h)),
                  pl.BlockSpec((None, seq, LANE), lambda b, h, i: (b, 0, h))],
        out_specs=pl.BlockSpec((None, tq, LANE), lambda b, h, i: (b, i, h)),
        out_shape=jax.ShapeDtypeStruct((bsz, seq, MLA_HEADS * MLA_V), BF16),
        compiler_params=_params("parallel", "parallel", "arbitrary"), name="mla_attention",
    )(qm, km, vm)


def _even_out_kernel(x_ref, on_ref, om_ref, wn_ref, wm_ref, g_ref, wg_ref, wu_ref, wd_ref, o_ref, *, chunk):
    x1 = x_ref[...] + _dot(on_ref[...], wn_ref[...]) + _dot(om_ref[...], wm_ref[...])
    n = _rms(x1, g_ref[...], x1.shape[-1]).astype(BF16)
    ffn = None
    for f0 in range(0, wg_ref.shape[1], chunk):
        gate = _dot(n, wg_ref[:, f0:f0 + chunk])
        act = (gate * jax.nn.sigmoid(gate) * _dot(n, wu_ref[:, f0:f0 + chunk])).astype(BF16)
        part = _dot(act, wd_ref[f0:f0 + chunk, :])
        ffn = part if ffn is None else ffn + part
    o_ref[...] = x1 + ffn


def _even_out_ffn(x, o_nsa, o_mla, w_out, gain, w_gate, w_up, w_down, tm=512):
    t, d = x.shape
    dff = w_gate.shape[1]
    wn, wm = w_out[:o_nsa.shape[1]].astype(BF16), w_out[o_nsa.shape[1]:].astype(BF16)
    row = lambda width: pl.BlockSpec((tm, width), lambda i: (i, 0))
    return pl.pallas_call(
        functools.partial(_even_out_kernel, chunk=dff // 2), grid=(t // tm,),
        in_specs=[row(d), row(o_nsa.shape[1]), row(o_mla.shape[1]), _const_spec(wn.shape), _const_spec(wm.shape),
                  _const_spec((1, d)), _const_spec((d, dff)), _const_spec((d, dff)), _const_spec((dff, d))],
        out_specs=row(d), out_shape=jax.ShapeDtypeStruct((t, d), F32),
        compiler_params=_params("parallel"), name="even_out_ffn",
    )(x, o_nsa, o_mla, wn, wm, gain[None, :], w_gate.astype(BF16), w_up.astype(BF16), w_down.astype(BF16))


def _conv_kernel(x_ref, g_ref, win_ref, cw_ref, wout_ref, o_ref, vbuf_ref, *, tm):
    j = pl.program_id(1)
    x = x_ref[...]
    d = x.shape[-1]
    n = _rms(x, g_ref[...], d).astype(BF16)
    b_gate = _dot(n, win_ref[:, 0:d])
    v = _dot(n, win_ref[:, d:2 * d]) * _dot(n, win_ref[:, 2 * d:3 * d])

    @pl.when(j == 0)
    def _():
        vbuf_ref[0:8, :] = jnp.zeros((8, d), F32)

    vbuf_ref[8:8 + tm, :] = v
    cw = cw_ref[...]
    y = cw[2:3] * v + cw[1:2] * vbuf_ref[7:7 + tm, :] + cw[0:1] * vbuf_ref[6:6 + tm, :]
    vbuf_ref[0:8, :] = v[tm - 8:tm]
    o_ref[...] = x + _dot((b_gate * y).astype(BF16), wout_ref[...])


def _conv_mixer(x, gain, w_in, conv_w, w_out, tm=512):
    bsz, seq, d = x.shape
    cw = jnp.pad(conv_w.astype(F32), ((0, 8 - CONV_WIDTH), (0, 0)))
    tok = pl.BlockSpec((None, tm, d), lambda b, j: (b, j, 0))
    return pl.pallas_call(
        functools.partial(_conv_kernel, tm=tm), grid=(bsz, seq // tm),
        in_specs=[tok, _const_spec((1, d)), _const_spec((d, 3 * d)), _const_spec((8, d)), _const_spec((d, d))],
        out_specs=tok, out_shape=jax.ShapeDtypeStruct((bsz, seq, d), F32),
        scratch_shapes=[pltpu.VMEM((tm + 8, d), F32)],
        compiler_params=_params("parallel", "arbitrary"), name="conv_mixer",
    )(x, gain[None, :], w_in.astype(BF16), cw, w_out.astype(BF16))


def _router_kernel(x_ref, g_ref, wr_ref, br_ref, tri_ref, route_ref, cnt_ref, carry_ref, *, tm):
    t = pl.program_id(0)

    @pl.when(t == 0)
    def _():
        carry_ref[...] = jnp.zeros_like(carry_ref)

    n = _rms(x_ref[...], g_ref[...], x_ref.shape[-1])
    hi = n.astype(BF16)
    lo = (n - hi.astype(F32)).astype(BF16)
    whi, wlo = wr_ref[0], wr_ref[1]
    logits = _dot(hi, whi) + _dot(lo, whi) + _dot(hi, wlo) + br_ref[...]
    lane = lax.broadcasted_iota(jnp.int32, (tm, LANE), 1).astype(F32)
    big = float(LANE)
    m1 = jnp.max(logits, axis=-1, keepdims=True)
    e1 = jnp.min(jnp.where(logits == m1, lane, big), axis=-1, keepdims=True)
    rest = jnp.where(lane == e1, NEG, logits)
    m2 = jnp.max(rest, axis=-1, keepdims=True)
    e2 = jnp.min(jnp.where(rest == m2, lane, big), axis=-1, keepdims=True)
    z = jnp.exp(m2 - m1)
    w1 = 1.0 / (1.0 + z)
    w2 = z / (1.0 + z)
    oh1 = jnp.where(lane == e1, 1.0, 0.0)
    oh2 = jnp.where(lane == e2, 1.0, 0.0)
    both = oh1 + oh2
    before = _dot(tri_ref[...], both.astype(BF16)) + carry_ref[0:1, :]
    r1 = jnp.sum(oh1 * before, axis=-1, keepdims=True)
    r2 = jnp.sum(oh2 * before, axis=-1, keepdims=True)
    cols = [e1, e2, w1, w2, r1, r2]
    out = jnp.zeros((tm, LANE), F32)
    for c, val in enumerate(cols):
        out = jnp.where(lane == c, val, out)
    route_ref[...] = out
    carry_ref[0:1, :] = carry_ref[0:1, :] + jnp.sum(both, axis=0, keepdims=True)
    cnt_ref[...] = carry_ref[...]


def _moe_router(x, gain, w_router, b_router, tm=512):
    t, d = x.shape
    wr = jnp.pad(w_router.astype(F32), ((0, 0), (0, LANE - N_EXPERTS)))
    whi = wr.astype(BF16)
    wlo = (wr - whi.astype(F32)).astype(BF16)
    br = jnp.concatenate([b_router.astype(F32), jnp.full((LANE - N_EXPERTS,), NEG, F32)])[None, :]
    tri = jnp.asarray(np.tril(np.ones((tm, tm), np.float32), -1), BF16)
    route, cnt = pl.pallas_call(
        functools.partial(_router_kernel, tm=tm), grid=(t // tm,),
        in_specs=[pl.BlockSpec((tm, d), lambda i: (i, 0)), _const_spec((1, d)), _const_spec((2, d, LANE)),
                  _const_spec((1, LANE)), _const_spec((tm, tm))],
        out_specs=[pl.BlockSpec((tm, LANE), lambda i: (i, 0)), pl.BlockSpec((8, LANE), lambda i: (0, 0))],
        out_shape=[jax.ShapeDtypeStruct((t, LANE), F32), jax.ShapeDtypeStruct((8, LANE), F32)],
        scratch_shapes=[pltpu.VMEM((8, LANE), F32)],
        compiler_params=_params("arbitrary"), name="moe_router",
    )(x, gain[None, :], jnp.stack([whi, wlo]), br, tri)
    return route, cnt[0, :N_EXPERTS].astype(jnp.int32)


def _row_copy(src, i, dst, j, sem):
    return pltpu.make_async_copy(src.at[pl.ds(i, 1)], dst.at[pl.ds(j, 1)], sem)


def _scatter_kernel(fill_ref, dest_ref, x_ref, g_ref, xs_ref, xn_ref, zero_ref, sems, *, tm, rb):
    @pl.when(pl.program_id(0) == 0)
    def _():
        sem = sems.at[0]
        zero_ref[...] = jnp.zeros_like(zero_ref)
        sizes = [rb >> s for s in range(rb.bit_length() - 3)]
        for e in range(N_EXPERTS + 1):
            lo, n = fill_ref[e], fill_ref[N_EXPERTS + 1 + e] - fill_ref[e]
            whole = n // rb

            def copy(off, size):
                return pltpu.make_async_copy(zero_ref.at[pl.ds(0, size)],
                                             xs_ref.at[pl.ds(pl.multiple_of(off, 8), size)], sem)

            def blocks(k, c, lo=lo, copy=copy):
                copy(lo + k * rb, rb).start()
                copy(lo + k * rb, rb).wait()
                return c

            lax.fori_loop(0, whole, blocks, 0)
            off = lo + whole * rb
            for size in sizes[1:]:
                @pl.when((n & size) != 0)
                def _(off=off, size=size, copy=copy):
                    copy(off, size).start()
                    copy(off, size).wait()
                off = off + (n & size)

    t = pl.program_id(0)
    slot = t % 2
    rows, row_sem = xn_ref.at[slot], sems.at[slot]
    rows[...] = _rms(x_ref[...], g_ref[...], x_ref.shape[-1])

    def start(r, c):
        _row_copy(rows, r, xs_ref, dest_ref[0, 0, r], row_sem).start()
        _row_copy(rows, r, xs_ref, dest_ref[0, 0, tm + r], row_sem).start()
        return c

    lax.fori_loop(0, tm, start, 0)

    def drain(s):
        for _ in range(TOP_K):
            pltpu.make_async_copy(xn_ref.at[s], xs_ref.at[pl.ds(0, tm)], sems.at[s]).wait()

    @pl.when(t > 0)
    def _():
        drain(1 - slot)

    @pl.when(t == pl.num_programs(0) - 1)
    def _():
        drain(slot)


def _moe_scatter(x, gain, dest, fill, n_rows, tm, rb):
    t, d = x.shape
    nt = t // tm
    grid_spec = pltpu.PrefetchScalarGridSpec(
        num_scalar_prefetch=1, grid=(nt,),
        in_specs=[pl.BlockSpec((1, 1, 2 * tm), lambda i, f: (i, 0, 0), memory_space=pltpu.SMEM),
                  pl.BlockSpec((tm, d), lambda i, f: (i, 0)),
                  pl.BlockSpec((1, d), lambda i, f: (0, 0))],
        out_specs=pl.BlockSpec(memory_space=pl.ANY),
        scratch_shapes=[pltpu.VMEM((2, tm, d), F32), pltpu.VMEM((rb, d), F32), pltpu.SemaphoreType.DMA((2,))])
    return pl.pallas_call(
        functools.partial(_scatter_kernel, tm=tm, rb=rb), grid_spec=grid_spec,
        out_shape=jax.ShapeDtypeStruct((n_rows, d), F32),
        compiler_params=_params("arbitrary"), name="moe_scatter",
    )(fill, dest, x, gain[None, :])


def _expert_kernel(be_ref, nu_ref, x_ref, wg_ref, wu_ref, wd_ref, o_ref):
    used = pl.program_id(0) < nu_ref[0]

    @pl.when(used)
    def _():
        xb = x_ref[...].astype(BF16)
        gate = _dot(xb, wg_ref[...])
        act = (gate * jax.nn.sigmoid(gate) * _dot(xb, wu_ref[...])).astype(BF16)
        o_ref[...] = _dot(act, wd_ref[...])

    @pl.when(jnp.logical_not(used))
    def _():
        o_ref[...] = jnp.zeros_like(o_ref)


def _moe_experts(xs, blk_expert, n_used, w_gate, w_up, w_down, rb):
    n_rows, d = xs.shape
    dff = w_gate.shape[-1]
    rowblk = pl.BlockSpec((rb, d), lambda i, be, nu: (i, 0))
    wspec = lambda shape: pl.BlockSpec((None,) + shape, lambda i, be, nu: (be[i], 0, 0))
    grid_spec = pltpu.PrefetchScalarGridSpec(
        num_scalar_prefetch=2, grid=(n_rows // rb,),
        in_specs=[rowblk, wspec((d, dff)), wspec((d, dff)), wspec((dff, d))], out_specs=rowblk)
    return pl.pallas_call(
        _expert_kernel, grid_spec=grid_spec, out_shape=jax.ShapeDtypeStruct(xs.shape, F32),
        compiler_params=_params("arbitrary"), name="moe_experts",
    )(blk_expert, n_used, xs, w_gate.astype(BF16), w_up.astype(BF16), w_down.astype(BF16))


def _combine_kernel(dest_ref, next_ref, x_ref, route_ref, ys_ref, o_ref, got_ref, sems, *, tm):
    t = pl.program_id(0)
    slot = t % 2

    def gather(idx_ref, s):
        def start(r, c):
            for k in range(TOP_K):
                _row_copy(ys_ref, idx_ref[0, 0, k * tm + r], got_ref.at[s, k], r, sems.at[s]).start()
            return c

        lax.fori_loop(0, tm, start, 0)

    @pl.when(t == 0)
    def _():
        gather(dest_ref, 0)

    @pl.when(t + 1 < pl.num_programs(0))
    def _():
        gather(next_ref, 1 - slot)

    for k in range(TOP_K):
        pltpu.make_async_copy(ys_ref.at[pl.ds(0, tm)], got_ref.at[slot, k], sems.at[slot]).wait()
    route = route_ref[...]
    o_ref[...] = x_ref[...] + route[:, 2:3] * got_ref[slot, 0] + route[:, 3:4] * got_ref[slot, 1]


def _moe_combine(x, route, dest, ys, tm):
    t, d = x.shape
    nt = t // tm
    idx = lambda nxt: pl.BlockSpec((1, 1, TOP_K * tm), lambda i: (jnp.minimum(i + nxt, nt - 1), 0, 0),
                                   memory_space=pltpu.SMEM)
    return pl.pallas_call(
        functools.partial(_combine_kernel, tm=tm), grid=(nt,),
        in_specs=[idx(0), idx(1), pl.BlockSpec((tm, d), lambda i: (i, 0)), pl.BlockSpec((tm, LANE), lambda i: (i, 0)),
                  pl.BlockSpec(memory_space=pl.ANY)],
        out_specs=pl.BlockSpec((tm, d), lambda i: (i, 0)),
        out_shape=jax.ShapeDtypeStruct((t, d), F32),
        scratch_shapes=[pltpu.VMEM((2, TOP_K, tm, d), F32), pltpu.SemaphoreType.DMA((2,))],
        compiler_params=_params("arbitrary"), name="moe_combine",
    )(dest, dest, x, route, ys)


def _moe(x, gain, w_router, b_router, w_gate, w_up, w_down, rb=512, tm=256):
    t, d = x.shape
    route, counts = _moe_router(x, gain, w_router, b_router)
    n_blk = (t * TOP_K + rb - 1) // rb + N_EXPERTS
    padded = (counts + rb - 1) // rb * rb
    pad_end = jnp.cumsum(padded)
    pad_start = pad_end - padded
    dest = pad_start[route[:, 0:2].astype(jnp.int32)] + route[:, 4:6].astype(jnp.int32)
    dest_t = jnp.concatenate([dest[:, 0].reshape(t // tm, 1, tm), dest[:, 1].reshape(t // tm, 1, tm)], axis=2)
    fill = jnp.concatenate([(pad_start + counts) // 8 * 8, pad_end[-1:], pad_end, jnp.full((1,), n_blk * rb, jnp.int32)])
    n_used = pad_end[-1:] // rb
    blk = jnp.minimum(jnp.arange(n_blk, dtype=jnp.int32), n_used - 1) * rb
    blk_expert = jnp.sum(pad_end[None, :] <= blk[:, None], axis=1).astype(jnp.int32)
    xs = _moe_scatter(x, gain, dest_t, fill.astype(jnp.int32), n_blk * rb, tm, rb)
    ys = _moe_experts(xs, blk_expert, n_used.astype(jnp.int32), w_gate, w_up, w_down, rb)
    return _moe_combine(x, route, dest_t, ys, tm)


def kernel(x, rel_bias, ev_mix_norm, ev_w_in, nsa_q_norm, nsa_k_norm, nsa_cmp_pos, nsa_cmp_w1, nsa_cmp_w2, mla_cq_norm, mla_ckv_norm, mla_w_uq, mla_w_ukv, mla_q_norm, mla_k_norm, ev_w_out, ev_ffn_norm, ffn_w_gate, ffn_w_up, ffn_w_down, od_mix_norm, od_w_in, conv_w, od_w_out, od_ffn_norm, moe_w_router, moe_b_router, moe_w_gate, moe_w_up, moe_w_down):
    bsz, seq, d = x.shape
    depth = ev_mix_norm.shape[0] + od_mix_norm.shape[0]
    pad = 512
    for layer in range(depth):
        i = layer // 2
        if layer % 2 == 0:
            qn, kv, kc_raw, vc_raw, gates, qm, km, vm = _even_in_proj(
                x, ev_mix_norm[i], ev_w_in[i], nsa_q_norm[i], nsa_k_norm[i], mla_cq_norm[i], mla_ckv_norm[i],
                mla_w_uq[i], mla_w_ukv[i], mla_q_norm[i], mla_k_norm[i], tm=pad)
            kc = _compress(kc_raw, nsa_cmp_pos[i, 0], nsa_cmp_w1[i, 0], nsa_cmp_w2[i, 0], nsa_k_norm[i, 0], True)
            vc = _compress(vc_raw, nsa_cmp_pos[i, 1], nsa_cmp_w1[i, 1], nsa_cmp_w2[i, 1], nsa_k_norm[i, 0], False)
            o_nsa = _nsa_attention(qn, kv, kc, vc, gates, rel_bias, pad)
            o_mla = _mla_attention(qm, km, vm)
            x = _even_out_ffn(x.reshape(bsz * seq, d), o_nsa.reshape(bsz * seq, -1), o_mla.reshape(bsz * seq, -1),
                              ev_w_out[i], ev_ffn_norm[i], ffn_w_gate[i], ffn_w_up[i], ffn_w_down[i]).reshape(bsz, seq, d)
        else:
            x = _conv_mixer(x, od_mix_norm[i], od_w_in[i], conv_w[i], od_w_out[i])
            x = _moe(x.reshape(bsz * seq, d), od_ffn_norm[i], moe_w_router[i], moe_b_router[i],
                     moe_w_gate[i], moe_w_up[i], moe_w_down[i]).reshape(bsz, seq, d)
    return x
```

```python
import functools
import math

import jax
import jax.numpy as jnp
import numpy as np
from jax import lax
from jax.experimental import pallas as pl
from jax.experimental.pallas import tpu as pltpu

F32 = jnp.float32
BF16 = jnp.bfloat16

EPS = 1e-6
NEG = -1e30
FORCED_SCORE = 1e9
NSA_HEADS = 8
NSA_GROUPS = 2
NSA_REP = NSA_HEADS // NSA_GROUPS
NSA_DK = 64
CMP_BLOCK = 32
CMP_STRIDE = 16
CMP_HIDDEN = 256
SEL_BLOCK = 64
SEL_TOP_N = 16
WINDOW = 512
MLA_HEADS = 8
MLA_Q_RANK = 256
MLA_KV_RANK = 128
MLA_NOPE = 64
MLA_ROPE = 32
MLA_V = 64
ROPE_THETA = 10000.0
REL_BUCKETS = 32
REL_MAX_DIST = 128
CONV_WIDTH = 3
N_EXPERTS = 8
TOP_K = 2
EVEN_IN_SIZES = (512,) + (128,) * 6 + (24, 256, 128, 32)

LANE = 128
VMEM_LIMIT = 56 * 1024 * 1024
SEL_MASK = -30000.0
FAR_DIST = 128

_Q0, _KV0, _KC0, _VC0, _GT0, _CQ0, _CKV0, _KR0, _EVEN_W = 0, 1024, 2048, 2176, 2304, 2560, 2816, 2944, 3072


def _dot(a, b):
    return jnp.dot(a, b, preferred_element_type=F32)


def _dot_nt(a, b):
    return lax.dot_general(a, b, (((1,), (1,)), ((), ())), preferred_element_type=F32)


def _rms(x, gain, n):
    ss = jnp.sum(x * x, axis=-1, keepdims=True) * (1.0 / n)
    return x * lax.rsqrt(ss + EPS) * gain


def _params(*sem):
    return pltpu.CompilerParams(dimension_semantics=sem, vmem_limit_bytes=VMEM_LIMIT)


def _const_spec(shape):
    nd = len(shape)
    return pl.BlockSpec(shape, lambda *_: (0,) * nd, pipeline_mode=pl.Buffered(1))


def _even_in_kernel(x_ref, gmix_ref, w_ref, gq_ref, gks_ref, gkw_ref, gcq_ref, gckv_ref,
                    wuq_ref, wuk_ref, wuv_ref, gmq_ref, gmk_ref, cos_ref, sa_ref, sb_ref,
                    qn_ref, kv_ref, kc_ref, vc_ref, gate_ref, qm_ref, km_ref, vm_ref, *, tm):
    j = pl.program_id(1)

    @pl.when(j == 0)
    def _():
        kv_ref[...] = jnp.zeros_like(kv_ref)
        qn_ref[...] = jnp.zeros_like(qn_ref)
        kc_ref[...] = jnp.zeros_like(kc_ref)
        vc_ref[...] = jnp.zeros_like(vc_ref)
        gate_ref[...] = jnp.zeros_like(gate_ref)
        qm_ref[...] = jnp.zeros_like(qm_ref)
        km_ref[...] = jnp.zeros_like(km_ref)
        vm_ref[...] = jnp.zeros_like(vm_ref)

    @pl.when(j > 0)
    def _():
        xn = _rms(x_ref[...], gmix_ref[...], x_ref.shape[-1]).astype(BF16)
        hq = _dot(xn, w_ref[:, _Q0:_Q0 + 1024])
        hkv = _dot(xn, w_ref[:, _KV0:_KV0 + 1024])
        hc = _dot(xn, w_ref[:, _KC0:_KC0 + 256])
        hg = _dot(xn, w_ref[:, _GT0:_GT0 + 256])
        hm = _dot(xn, w_ref[:, _CQ0:_EVEN_W])
        cq = _rms(hm[:, :MLA_Q_RANK], gcq_ref[...], MLA_Q_RANK).astype(BF16)
        ckv = _rms(hm[:, MLA_Q_RANK:MLA_Q_RANK + MLA_KV_RANK], gckv_ref[...], MLA_KV_RANK).astype(BF16)
        k_rope = hm[:, MLA_Q_RANK + MLA_KV_RANK:]
        qm = _dot(cq, wuq_ref[...])
        kn = _dot(ckv, wuk_ref[...])
        vm_ref[...] = _dot(ckv, wuv_ref[...]).astype(BF16)
        dqk = MLA_NOPE + MLA_ROPE
        nsa_heads = [slice(LANE * h, LANE * (h + 1)) for h in range(NSA_HEADS)]
        mla_heads = [slice(LANE * h, LANE * (h + 1)) for h in range(MLA_HEADS)]
        segs = [(hq[:, sl], gq_ref[...], NSA_DK, NSA_DK ** -0.5) for sl in nsa_heads]
        for g in range(NSA_GROUPS):
            o = 4 * LANE * g
            segs.append((hkv[:, o:o + LANE], gks_ref[...], NSA_DK, 1.0))
            segs.append((hkv[:, o + 2 * LANE:o + 3 * LANE], gkw_ref[...], NSA_DK, 1.0))
        n_nsa = len(segs)
        segs += [(qm[:, sl], gmq_ref[...], dqk, dqk ** -0.5) for sl in mla_heads]
        segs += [(kn[:, sl] + k_rope, gmk_ref[...], dqk, 1.0) for sl in mla_heads]
        sums = [jnp.sum(x * x, axis=-1, keepdims=True) for x, _, _, _ in segs]
        scales = [lax.rsqrt(ss * (1.0 / n) + EPS) for ss, (_, _, n, _) in zip(sums, segs)]
        normed = [x * sc * gain for sc, (x, gain, _, _) in zip(scales, segs)]
        for h, sl in enumerate(nsa_heads):
            qn_ref[:, sl] = (normed[h] * segs[h][3]).astype(BF16)
        pos = (j - 1) * tm + lax.broadcasted_iota(jnp.int32, (tm, LANE), 0)
        lane = lax.broadcasted_iota(jnp.int32, (tm, LANE), 1)
        onehot = jnp.where(lane - NSA_DK == pos // SEL_BLOCK, 1.0, 0.0)
        ones = jnp.where(lane >= NSA_DK, 1.0, 0.0)
        for g in range(NSA_GROUPS):
            o = 4 * LANE * g
            kv_ref[:, o:o + LANE] = (normed[NSA_HEADS + 2 * g] + onehot).astype(BF16)
            kv_ref[:, o + LANE:o + 2 * LANE] = (hkv[:, o + LANE:o + 2 * LANE] + ones).astype(BF16)
            kv_ref[:, o + 2 * LANE:o + 3 * LANE] = normed[NSA_HEADS + 2 * g + 1].astype(BF16)
            kv_ref[:, o + 3 * LANE:o + 4 * LANE] = (hkv[:, o + 3 * LANE:o + 4 * LANE] + ones).astype(BF16)
        kc_ref[...] = hc[:, :LANE].astype(BF16)
        vc_ref[...] = hc[:, LANE:].astype(BF16)
        gate_ref[...] = jax.nn.sigmoid(hg)
        cos, sa, sb = cos_ref[...], sa_ref[...], sb_ref[...]
        mla = normed[n_nsa:]
        fwd = [pltpu.roll(x, 16, 1) for x in mla]
        bwd = [pltpu.roll(x, LANE - 16, 1) for x in mla]
        roped = [x * cos + f * sa + b * sb for x, f, b in zip(mla, fwd, bwd)]
        for h, sl in enumerate(mla_heads):
            qm_ref[:, sl] = (roped[h] * segs[n_nsa + h][3]).astype(BF16)
            km_ref[:, sl] = roped[MLA_HEADS + h].astype(BF16)


def _even_in_weights(w_in):
    offs = np.concatenate([[0], np.cumsum(EVEN_IN_SIZES)])
    part = [w_in[:, offs[n]:offs[n + 1]] for n in range(len(EVEN_IN_SIZES))]
    q, k_c, v_c, k_s, v_s, k_w, v_w, gates, c_q, c_kv, k_rope = part
    d = w_in.shape[0]
    z = lambda n: jnp.zeros((d, n), w_in.dtype)
    cols = []
    for h in range(NSA_HEADS):
        cols += [q[:, 64 * h:64 * h + 64], z(64)]
    for g in range(NSA_GROUPS):
        s = slice(64 * g, 64 * g + 64)
        cols += [k_s[:, s], z(64), v_s[:, s], z(64), k_w[:, s], z(64), v_w[:, s], z(64)]
    cols += [k_c, v_c]
    for g in range(NSA_GROUPS):
        cols += [gates[:, 12 * g:12 * g + 12], z(LANE - 12)]
    cols += [c_q, c_kv, z(64), k_rope, z(32)]
    w = jnp.concatenate(cols, axis=1)
    assert w.shape[1] == _EVEN_W
    return w.astype(BF16)


def _pad_gain(g, width):
    return jnp.pad(g.astype(F32), (0, width - g.shape[0]))[None, :]


def _rope_tables(seq):
    half = MLA_ROPE // 2
    inv_freq = ROPE_THETA ** (-jnp.arange(half, dtype=F32) / half)
    ang = jnp.arange(seq).astype(F32)[:, None] * inv_freq[None, :]
    cos, sin = jnp.cos(ang), jnp.sin(ang)
    one = jnp.ones((seq, MLA_NOPE), F32)
    zn = jnp.zeros((seq, MLA_NOPE), F32)
    zt = jnp.zeros((seq, LANE - MLA_NOPE - MLA_ROPE), F32)
    zh = jnp.zeros((seq, half), F32)
    cos_t = jnp.concatenate([one, cos, cos, zt + 1.0], axis=1)
    sa = jnp.concatenate([zn, zh, sin, zt], axis=1)
    sb = jnp.concatenate([zn, -sin, zh, zt], axis=1)
    return cos_t, sa, sb


def _even_in_proj(x, gmix, w_in, q_norm, k_norm, cq_norm, ckv_norm, w_uq, w_ukv, mq_norm, mk_norm, tm=512):
    bsz, seq, d = x.shape
    nt = seq // tm
    w = _even_in_weights(w_in)
    dqk = MLA_NOPE + MLA_ROPE
    wuq = jnp.concatenate(
        [jnp.pad(w_uq[:, dqk * h:dqk * (h + 1)], ((0, 0), (0, LANE - dqk))) for h in range(MLA_HEADS)],
        axis=1).astype(BF16)
    kvw = MLA_NOPE + MLA_V
    wuk = jnp.concatenate(
        [jnp.pad(w_ukv[:, kvw * h:kvw * h + MLA_NOPE], ((0, 0), (0, LANE - MLA_NOPE))) for h in range(MLA_HEADS)],
        axis=1).astype(BF16)
    wuv = jnp.concatenate([w_ukv[:, kvw * h + MLA_NOPE:kvw * (h + 1)] for h in range(MLA_HEADS)], axis=1).astype(BF16)
    cos_t, sa, sb = _rope_tables(seq)
    tok = lambda width: pl.BlockSpec((None, tm, width), lambda b, j: (b, jnp.maximum(j - 1, 0), 0))
    postab = pl.BlockSpec((tm, LANE), lambda b, j: (jnp.maximum(j - 1, 0), 0))
    in_specs = [tok(d), _const_spec((1, d)), _const_spec((d, _EVEN_W)),
                _const_spec((1, LANE)), _const_spec((1, LANE)), _const_spec((1, LANE)),
                _const_spec((1, MLA_Q_RANK)), _const_spec((1, MLA_KV_RANK)),
                _const_spec(wuq.shape), _const_spec(wuk.shape), _const_spec(wuv.shape),
                _const_spec((1, LANE)), _const_spec((1, LANE)), postab, postab, postab]
    out_shape = [jax.ShapeDtypeStruct((bsz, seq, 1024), BF16),
                 jax.ShapeDtypeStruct((bsz, seq + tm, 1024), BF16),
                 jax.ShapeDtypeStruct((bsz, seq, LANE), BF16),
                 jax.ShapeDtypeStruct((bsz, seq, LANE), BF16),
                 jax.ShapeDtypeStruct((bsz, seq, 2 * LANE), F32),
                 jax.ShapeDtypeStruct((bsz, seq, 1024), BF16),
                 jax.ShapeDtypeStruct((bsz, seq, 1024), BF16),
                 jax.ShapeDtypeStruct((bsz, seq, 512), BF16)]
    out_specs = [tok(1024), pl.BlockSpec((None, tm, 1024), lambda b, j: (b, j, 0)), tok(LANE), tok(LANE),
                 tok(2 * LANE), tok(1024), tok(1024), tok(512)]
    return pl.pallas_call(
        functools.partial(_even_in_kernel, tm=tm), grid=(bsz, nt + 1), in_specs=in_specs, out_specs=out_specs,
        out_shape=out_shape, compiler_params=_params("parallel", "arbitrary"), name="even_in_proj",
    )(x, gmix[None, :], w, _pad_gain(q_norm, LANE), _pad_gain(k_norm[1], LANE), _pad_gain(k_norm[2], LANE),
      cq_norm[None, :], ckv_norm[None, :], wuq, wuk, wuv, _pad_gain(mq_norm, LANE), _pad_gain(mk_norm, LANE),
      cos_t, sa, sb)


def _cmp_kernel(x_ref, wa_ref, wb_ref, pa_ref, pb_ref, w2_ref, gain_ref, o_ref, *, normalize):
    x = x_ref[...]
    ua = _dot(x, wa_ref[...])
    ub = _dot(x, wb_ref[...])
    pt = _dot(pa_ref[...], wa_ref[...]) + _dot(pb_ref[...], wb_ref[...])
    n = ub.shape[0]
    pre = ua + pltpu.roll(ub, n - 1, 0) + pt[0:1]
    hid = jax.nn.gelu(pre).astype(BF16)
    out = _dot(hid, w2_ref[...])
    if normalize:
        gain = gain_ref[...]
        for g in range(NSA_GROUPS):
            sl = slice(LANE * g, LANE * (g + 1))
            o_ref[:, sl] = _rms(out[:, sl], gain, NSA_DK).astype(BF16)
    else:
        o_ref[...] = out.astype(BF16)


def _compress(kv, pos_emb, w1, w2, gain, normalize):
    bsz, seq, _ = kv.shape
    half = CMP_BLOCK // 2
    assert CMP_STRIDE == half
    nrow = seq // CMP_STRIDE
    x = kv.reshape(bsz, nrow, CMP_STRIDE * LANE)
    w1r = w1.reshape(CMP_BLOCK, NSA_DK, CMP_HIDDEN)
    eye = jnp.eye(NSA_GROUPS, dtype=w1.dtype)
    widen = lambda w: jnp.einsum("ldn,gh->lgdhn", w, eye).reshape(half * LANE, NSA_GROUPS * CMP_HIDDEN).astype(BF16)
    wa, wb = widen(w1r[:half]), widen(w1r[half:])
    prow = lambda p: jnp.pad(jnp.broadcast_to(p[:, None, :], (half, NSA_GROUPS, NSA_DK)).reshape(1, half * LANE),
                             ((0, 15), (0, 0))).astype(BF16)
    pa, pb = prow(pos_emb[:half]), prow(pos_emb[half:])
    second = jnp.zeros_like(w2) if normalize else w2
    w2w = jnp.einsum("nd,gh->gnhd", jnp.concatenate([w2, second], axis=1), eye)
    w2w = w2w.reshape(NSA_GROUPS * CMP_HIDDEN, NSA_GROUPS * LANE).astype(BF16)
    return pl.pallas_call(
        functools.partial(_cmp_kernel, normalize=normalize), grid=(bsz,),
        in_specs=[pl.BlockSpec((None, nrow, CMP_STRIDE * LANE), lambda b: (b, 0, 0)),
                  _const_spec(wa.shape), _const_spec(wb.shape), _const_spec(pa.shape), _const_spec(pb.shape),
                  _const_spec(w2w.shape), _const_spec((1, LANE))],
        out_specs=pl.BlockSpec((None, nrow, NSA_GROUPS * LANE), lambda b: (b, 0, 0)),
        out_shape=jax.ShapeDtypeStruct((bsz, nrow, NSA_GROUPS * LANE), BF16),
        compiler_params=_params("parallel"), name="nsa_compress",
    )(x, wa, wb, pa, pb, w2w, _pad_gain(gain, LANE))


def _bucket_table():
    dist = np.arange(FAR_DIST + 1)
    max_exact = REL_BUCKETS // 2
    nf = np.maximum(dist, max_exact).astype(np.float32)
    large = max_exact + (np.log(nf / max_exact) / math.log(REL_MAX_DIST / max_exact)
                         * (REL_BUCKETS - max_exact)).astype(np.int32)
    return np.where(dist < max_exact, dist, np.minimum(large, REL_BUCKETS - 1))


CMP_NEAR = 32


def _expand(tbl, idx):
    idx = np.asarray(idx)
    onehot = jnp.asarray(np.eye(tbl.shape[1], dtype=np.float32)[idx.reshape(-1)])
    out = lax.dot_general(tbl, onehot, (((1,), (1,)), ((), ())), precision=lax.Precision.HIGHEST)
    return out.reshape((tbl.shape[0],) + idx.shape)


def _toeplitz_tile(tbl, far, width, dist0, valid):
    period = width + SEL_BLOCK + 1
    j = np.arange(period)
    dist = dist0 - np.where(j < width, j, j - period)
    gen = jnp.where(jnp.asarray(valid(dist))[None], _expand(tbl, np.clip(dist, 0, FAR_DIST)) - far, NEG)
    flat = jnp.tile(gen, (1, SEL_BLOCK))[:, :SEL_BLOCK * (period - 1)]
    return flat.reshape(tbl.shape[0], SEL_BLOCK, period - 1)[:, :, :width]


def _bias_tables(rel_bias):
    tbl = _expand(rel_bias.astype(F32).T, _bucket_table())
    far = tbl[:, FAR_DIST:]
    bias_near = _toeplitz_tile(tbl, far, 4 * SEL_BLOCK, 3 * SEL_BLOCK, lambda d: d >= 0)
    bias_win = _toeplitz_tile(tbl, 0.0, WINDOW + SEL_BLOCK, WINDOW, lambda d: (d >= 0) & (d < WINDOW))
    d_c = (np.arange(SEL_BLOCK)[:, None] + (CMP_NEAR // 2) * CMP_STRIDE - (CMP_BLOCK - 1)
           - CMP_STRIDE * np.arange(CMP_NEAR)[None, :])
    assert d_c[:, 0].min() >= FAR_DIST and d_c[:, -1].max() < 0
    bias_cmp = jnp.where(jnp.asarray(d_c >= 0)[None], _expand(tbl, np.clip(d_c, 0, FAR_DIST)) - far[:, :, None], 0.0)
    return bias_cmp, bias_near, bias_win


def _cmp_mask_dist(seq):
    a = np.tile(np.arange(SEL_BLOCK), NSA_REP)[:, None]
    c = np.arange(seq // CMP_STRIDE)[None, :]
    return jnp.asarray(c * CMP_STRIDE + CMP_BLOCK - 1 - a, jnp.int32)


def _overlap_t(seq):
    ncp = seq // CMP_STRIDE
    cs = np.arange(ncp)[None, :] * CMP_STRIDE
    ss = np.arange(SEL_BLOCK)[:, None] * SEL_BLOCK
    ov = (cs < ss + SEL_BLOCK) & (cs + CMP_BLOCK - 1 >= ss) & (np.arange(ncp)[None, :] < (seq - CMP_BLOCK) // CMP_STRIDE + 1)
    return jnp.asarray(ov, BF16)


def _softmax_step(s, v, state):
    m, l, acc = state
    mn = jnp.maximum(m, jnp.max(s, axis=-1, keepdims=True))
    alpha = jnp.exp(m - mn)
    p = jnp.exp(s - mn)
    return mn, alpha * l + jnp.sum(p, axis=-1, keepdims=True), alpha * acc + _dot(p.astype(BF16), v)


FAR_CHUNK = 1024


def _lane_fold(x, op):
    out = x[:, 0:LANE]
    for c in range(1, x.shape[1] // LANE):
        out = op(out, x[:, LANE * c:LANE * (c + 1)])
    return out


SEL_TILE = 8


def _nsa_local_kernel(q_ref, kv_ref, kc_ref, vc_ref, g_ref, bc_ref, bw_ref, ovt_ref, dmask_ref, neg_ref, oc_ref, *, pad):
    t = pl.program_id(1)
    qb, rep, ng = SEL_BLOCK, NSA_REP, NSA_GROUPS
    rows = qb * rep
    gw = 4 * LANE
    ncp = kc_ref.shape[0]
    wk = WINDOW + qb
    m_i = lax.broadcasted_iota(jnp.int32, (CMP_NEAR, ncp), 0)
    c_i = lax.broadcasted_iota(jnp.int32, (CMP_NEAR, ncp), 1)
    w_lane = lax.broadcasted_iota(jnp.int32, (rows, wk), 1)
    dist = dmask_ref[...]
    ovt = ovt_ref[...]
    gates = g_ref[...]
    low = lax.broadcasted_iota(jnp.int32, (qb, LANE), 1) < NSA_DK
    jj = lax.broadcasted_iota(jnp.int32, (qb, ng * qb), 0)
    bases = []
    for g in range(ng):
        base = bc_ref[rep * g:rep * (g + 1)].reshape(rows, CMP_NEAR)
        b_hi = base.astype(BF16)
        bases.append((b_hi, (base - b_hi.astype(F32)).astype(BF16)))
    blocks = [(u, t * SEL_TILE + u, slice(qb * u, qb * (u + 1))) for u in range(SEL_TILE)]
    chains = [(u, i, tok, g) for u, i, tok in blocks for g in range(ng)]
    qq = [jnp.concatenate([q_ref[tok, gw * g + LANE * r:gw * g + LANE * (r + 1)] for r in range(rep)], axis=0)
          for _, _, tok, g in chains]
    shift = [jnp.where(c_i - m_i == (qb // CMP_STRIDE) * i - CMP_NEAR // 2, 1.0, 0.0).astype(BF16) for _, i, _ in blocks]
    s_c = [jnp.where(dist <= i * qb,
                     _dot_nt(qq[n], kc_ref[:, LANE * g:LANE * (g + 1)]) + _dot(bases[g][0], shift[u])
                     + _dot(bases[g][1], shift[u]), NEG) for n, (u, i, _, g) in enumerate(chains)]
    s_w = [jnp.where(w_lane + i * qb - WINDOW >= 0,
                     _dot_nt(qq[n], kv_ref[pl.ds(pl.multiple_of(pad - WINDOW + i * qb, qb), wk),
                                           gw * g + 2 * LANE:gw * g + 3 * LANE])
                     + bw_ref[rep * g:rep * (g + 1)].reshape(rows, wk), NEG) for n, (u, i, _, g) in enumerate(chains)]
    e_c = [jnp.exp(s - jnp.maximum(jnp.max(s, axis=-1, keepdims=True), -1e20)) for s in s_c]
    e_w = [jnp.exp(s - jnp.max(s, axis=-1, keepdims=True)) for s in s_w]
    inv = [1.0 / jnp.maximum(jnp.sum(e, axis=-1, keepdims=True), 1e-30) for e in e_c]
    o_c = [_dot(e_c[n].astype(BF16), vc_ref[:, LANE * g:LANE * (g + 1)]) * inv[n] for n, (_, _, _, g) in enumerate(chains)]
    pv_w = [_dot(e_w[n].astype(BF16), kv_ref[pl.ds(pl.multiple_of(pad - WINDOW + i * qb, qb), wk),
                                            gw * g + 3 * LANE:gw * g + 4 * LANE]) for n, (_, i, _, g) in enumerate(chains)]
    o_w = [pv / pv[:, NSA_DK:NSA_DK + 1] for pv in pv_w]
    for n, (_, _, tok, g) in enumerate(chains):
        part = []
        for r in range(rep):
            sl = slice(qb * r, qb * (r + 1))
            c0 = LANE * g + 3 * r
            part.append(gates[tok, c0:c0 + 1] * o_c[n][sl] + gates[tok, c0 + 2:c0 + 3] * o_w[n][sl])
        for pr in range(rep // 2):
            c0 = 2 * LANE * g + LANE * pr
            oc_ref[tok, c0:c0 + LANE] = jnp.where(low, part[2 * pr], pltpu.roll(part[2 * pr + 1], NSA_DK, 1))
    imps = []
    for u, i, _ in blocks:
        p = [e_c[ng * u + g] * inv[ng * u + g] for g in range(ng)]
        psum = jnp.concatenate([x[0:qb] + x[qb:2 * qb] + x[2 * qb:3 * qb] + x[3 * qb:4 * qb] for x in p], axis=0)
        hi = psum.astype(BF16)
        r1 = psum - hi.astype(F32)
        mid = r1.astype(BF16)
        lo = (r1 - mid.astype(F32)).astype(BF16)
        imp = _dot_nt(ovt, hi) + _dot_nt(ovt, mid) + _dot_nt(ovt, lo)
        imp = jnp.where((jj == i) | (jj == 0), FORCED_SCORE, imp)
        imps.append(jnp.where(jj > i, -1.0, imp))
    grp = [[imp[8 * a:8 * a + 8] for a in range(8)] for imp in imps]
    cnt = [[jnp.zeros((8, ng * qb), F32) for _ in range(8)] for _ in imps]
    sub = lax.broadcasted_iota(jnp.int32, (8, ng * qb), 0)
    for k in range(qb):
        for u in range(SEL_TILE):
            rk = imps[u][k:k + 1, :]
            for a in range(8):
                if 8 * a + 7 <= k:
                    cnt[u][a] = cnt[u][a] + jnp.where(rk > grp[u][a], 1.0, 0.0)
                elif 8 * a > k:
                    cnt[u][a] = cnt[u][a] + jnp.where(rk >= grp[u][a], 1.0, 0.0)
                else:
                    cnt[u][a] = cnt[u][a] + jnp.where(sub + 8 * a > k, jnp.where(rk >= grp[u][a], 1.0, 0.0),
                                                      jnp.where(rk > grp[u][a], 1.0, 0.0))
    for u in range(SEL_TILE):
        neg_ref[u] = jnp.where(jnp.concatenate(cnt[u], axis=0) < SEL_TOP_N, 0.0, SEL_MASK)


ATT_TILE = 2


def _nsa_attend_kernel(q_ref, kv_ref, g_ref, neg_ref, oc_ref, bn_ref, o_ref, s_ref, *, pad):
    t = pl.program_id(1)
    qb, rep, ng = SEL_BLOCK, NSA_REP, NSA_GROUPS
    rows = qb * rep
    gw = 4 * LANE
    blocks = [(u, t * ATT_TILE + u, slice(qb * u, qb * (u + 1))) for u in range(ATT_TILE)]
    chains = [(u, i, tok, g) for u, i, tok in blocks for g in range(ng)]
    nc = len(chains)
    qs = [jnp.concatenate([q_ref[tok, gw * g + LANE * r:gw * g + LANE * (r + 1)] for r in range(rep)], axis=0)
          for _, _, tok, g in chains]
    jj = lax.broadcasted_iota(jnp.int32, (qb, ng * qb), 0)
    q_lane = lax.broadcasted_iota(jnp.int32, (qb, ng * qb), 1)
    zero = jnp.zeros((qb, ng * qb), BF16)
    ext_near = [jnp.concatenate([zero, neg_ref[u].astype(BF16)], axis=0) for u, _, _ in blocks]
    ext_far = [jnp.concatenate([zero, jnp.where(jj >= i - 3, SEL_MASK, neg_ref[u]).astype(BF16)], axis=0)
               for u, i, _ in blocks]
    pick = [jnp.where(q_lane == jj + qb * g, 1.0, 0.0).astype(BF16) for g in range(ng)]
    q_near = [qs[n] + jnp.concatenate([_dot_nt(pick[g], ext_near[u]).astype(BF16)] * rep, axis=0)
              for n, (u, _, _, g) in enumerate(chains)]
    q_far = [qs[n] + jnp.concatenate([_dot_nt(pick[g], ext_far[u]).astype(BF16)] * rep, axis=0)
             for n, (u, _, _, g) in enumerate(chains)]

    n_lane = lax.broadcasted_iota(jnp.int32, (rows, 4 * qb), 1)
    near_at = [pl.multiple_of(pad + (i - 3) * qb, qb) for _, i, _ in blocks]
    s_near = [jnp.where(n_lane + (i - 3) * qb >= 0,
                        _dot_nt(q_near[n], kv_ref[pl.ds(near_at[u], 4 * qb), gw * g:gw * g + LANE])
                        + bn_ref[rep * g:rep * (g + 1)].reshape(rows, 4 * qb), NEG) for n, (u, i, _, g) in enumerate(chains)]

    nch = (jnp.maximum(blocks[-1][1] - 3, 0) * qb + FAR_CHUNK - 1) // FAR_CHUNK

    def pass1(c, mx):
        st = pl.multiple_of(pad + c * FAR_CHUNK, math.gcd(pad, FAR_CHUNK))
        col = pl.multiple_of(c * FAR_CHUNK, FAR_CHUNK)
        out = []
        for n, (_, _, _, g) in enumerate(chains):
            s = _dot_nt(q_far[n], kv_ref[pl.ds(st, FAR_CHUNK), gw * g:gw * g + LANE])
            s_ref[n, :, pl.ds(col, FAR_CHUNK)] = s
            out.append(jnp.maximum(mx[n], _lane_fold(s, jnp.maximum)))
        return tuple(out)

    mx = lax.fori_loop(0, nch, pass1, tuple(jnp.full((rows, LANE), NEG, F32) for _ in range(nc)))
    ms = [jnp.max(jnp.maximum(mx[n], _lane_fold(s_near[n], jnp.maximum)), axis=-1, keepdims=True) for n in range(nc)]

    def pass2(c, state):
        st = pl.multiple_of(pad + c * FAR_CHUNK, math.gcd(pad, FAR_CHUNK))
        col = pl.multiple_of(c * FAR_CHUNK, FAR_CHUNK)
        out = []
        for n, (_, _, _, g) in enumerate(chains):
            p = jnp.exp(s_ref[n, :, pl.ds(col, FAR_CHUNK)] - ms[n])
            out.append(state[n] + _dot(p.astype(BF16), kv_ref[pl.ds(st, FAR_CHUNK), gw * g + LANE:gw * g + 2 * LANE]))
        return tuple(out)

    far = lax.fori_loop(0, nch, pass2, tuple(jnp.zeros((rows, LANE), F32) for _ in range(nc)))
    p_near = [jnp.exp(s_near[n] - ms[n]).astype(BF16) for n in range(nc)]
    pv = [far[n] + _dot(p_near[n], kv_ref[pl.ds(near_at[u], 4 * qb), gw * g + LANE:gw * g + 2 * LANE])
          for n, (u, _, _, g) in enumerate(chains)]
    o_s = [x / x[:, NSA_DK:NSA_DK + 1] for x in pv]

    gates = g_ref[...]
    low = lax.broadcasted_iota(jnp.int32, (qb, LANE), 1) < NSA_DK
    for n, (_, _, tok, g) in enumerate(chains):
        outs = [gates[tok, LANE * g + 3 * r + 1:LANE * g + 3 * r + 2] * o_s[n][qb * r:qb * (r + 1)] for r in range(rep)]
        for pr in range(rep // 2):
            c0 = 2 * LANE * g + LANE * pr
            pair = jnp.where(low, outs[2 * pr], pltpu.roll(outs[2 * pr + 1], NSA_DK, 1))
            o_ref[tok, c0:c0 + LANE] = (oc_ref[tok, c0:c0 + LANE] + pair).astype(o_ref.dtype)


def _nsa_attention(qn, kv, kc, vc, gates, rel_bias, pad):
    bsz, seq, _ = qn.shape
    qb, ng = SEL_BLOCK, NSA_GROUPS
    nq = seq // qb
    ncp = seq // CMP_STRIDE
    assert nq <= qb and nq % SEL_TILE == 0 and nq % ATT_TILE == 0 and pad >= WINDOW and seq % FAR_CHUNK == 0
    bias_cmp, bias_near, bias_win = _bias_tables(rel_bias)
    ovt = _overlap_t(seq)
    dmask = _cmp_mask_dist(seq)
    width = NSA_HEADS * NSA_DK
    tile = lambda n, w: pl.BlockSpec((None, n, w), lambda b, i: (b, i, 0))
    per_b = lambda n, w: pl.BlockSpec((None, n, w), lambda b, i: (b, 0, 0))
    neg, o_local = pl.pallas_call(
        functools.partial(_nsa_local_kernel, pad=pad), grid=(bsz, nq // SEL_TILE),
        in_specs=[tile(SEL_TILE * qb, ng * 4 * LANE), per_b(seq + pad, ng * 4 * LANE), per_b(ncp, ng * LANE),
                  per_b(ncp, ng * LANE), tile(SEL_TILE * qb, ng * LANE), _const_spec(bias_cmp.shape),
                  _const_spec(bias_win.shape), _const_spec(ovt.shape), _const_spec(dmask.shape)],
        out_specs=[pl.BlockSpec((None, SEL_TILE, qb, ng * qb), lambda b, i: (b, i, 0, 0)), tile(SEL_TILE * qb, width)],
        out_shape=[jax.ShapeDtypeStruct((bsz, nq, qb, ng * qb), F32), jax.ShapeDtypeStruct((bsz, seq, width), F32)],
        compiler_params=_params("parallel", "arbitrary"), name="nsa_local",
    )(qn, kv, kc, vc, gates, bias_cmp, bias_win, ovt, dmask)
    return pl.pallas_call(
        functools.partial(_nsa_attend_kernel, pad=pad), grid=(bsz, nq // ATT_TILE),
        in_specs=[tile(ATT_TILE * qb, ng * 4 * LANE), per_b(seq + pad, ng * 4 * LANE), tile(ATT_TILE * qb, ng * LANE),
                  pl.BlockSpec((None, ATT_TILE, qb, ng * qb), lambda b, i: (b, i, 0, 0)), tile(ATT_TILE * qb, width),
                  _const_spec(bias_near.shape)],
        out_specs=tile(ATT_TILE * qb, width),
        out_shape=jax.ShapeDtypeStruct((bsz, seq, width), BF16),
        scratch_shapes=[pltpu.VMEM((ATT_TILE * ng, NSA_REP * qb, seq), F32)],
        compiler_params=_params("parallel", "arbitrary"), name="nsa_attention",
    )(qn, kv, gates, neg, o_local, bias_near)


def _mla_attn_kernel(q_ref, k_ref, v_ref, o_ref, *, tq, ck):
    i = pl.program_id(2)
    heads = [slice(LANE * hh, LANE * (hh + 1)) for hh in range(2)]
    qs = [q_ref[:, sl] for sl in heads]

    def body(c, states):
        st = pl.multiple_of(c * ck, ck)
        v = v_ref[pl.ds(st, ck), :]
        return tuple(_softmax_step(_dot_nt(q, k_ref[pl.ds(st, ck), sl]), v, state)
                     for q, sl, state in zip(qs, heads, states))

    init = (jnp.full((tq, 1), NEG, F32), jnp.zeros((tq, 1), F32), jnp.zeros((tq, LANE), F32))
    states = lax.fori_loop(0, i * (tq // ck), body, (init, init))
    st = pl.multiple_of(i * tq, tq)
    causal = lax.broadcasted_iota(jnp.int32, (tq, tq), 0) >= lax.broadcasted_iota(jnp.int32, (tq, tq), 1)
    outs = []
    for q, sl, state in zip(qs, heads, states):
        s = _dot_nt(q, k_ref[pl.ds(st, tq), sl])
        _, l, acc = _softmax_step(jnp.where(causal, s, NEG), v_ref[pl.ds(st, tq), :], state)
        outs.append(acc / l)
    low = lax.broadcasted_iota(jnp.int32, (tq, LANE), 1) < MLA_V
    o_ref[...] = jnp.where(low, outs[0], outs[1]).astype(o_ref.dtype)


def _mla_attention(qm, km, vm, tq=1024, ck=1024):
    bsz, seq, _ = qm.shape
    tq = min(tq, seq)
    return pl.pallas_call(
        functools.partial(_mla_attn_kernel, tq=tq, ck=min(ck, tq)), grid=(bsz, MLA_HEADS // 2, seq // tq),
        in_specs=[pl.BlockSpec((None, tq, 2 * LANE), lambda b, h, i: (b, i, h)),
                  pl.BlockSpec((None, seq, 2 * LANE), lambda b, h, i: (b, 0, h)),
                  pl.BlockSpec((None, seq, LANE), lambda b, h, i: (b, 0, h))],
        out_specs=pl.BlockSpec((None, tq, LANE), lambda b, h, i: (b, i, h)),
        out_shape=jax.ShapeDtypeStruct((bsz, seq, MLA_HEADS * MLA_V), BF16),
        compiler_params=_params("parallel", "parallel", "arbitrary"), name="mla_attention",
    )(qm, km, vm)


def _even_out_kernel(x_ref, on_ref, om_ref, wn_ref, wm_ref, g_ref, wg_ref, wu_ref, wd_ref, o_ref, *, chunk):
    x1 = x_ref[...] + _dot(on_ref[...], wn_ref[...]) + _dot(om_ref[...], wm_ref[...])
    n = _rms(x1, g_ref[...], x1.shape[-1]).astype(BF16)
    ffn = None
    for f0 in range(0, wg_ref.shape[1], chunk):
        gate = _dot(n, wg_ref[:, f0:f0 + chunk])
        act = (gate * jax.nn.sigmoid(gate) * _dot(n, wu_ref[:, f0:f0 + chunk])).astype(BF16)
        part = _dot(act, wd_ref[f0:f0 + chunk, :])
        ffn = part if ffn is None else ffn + part
    o_ref[...] = x1 + ffn


def _even_out_ffn(x, o_nsa, o_mla, w_out, gain, w_gate, w_up, w_down, tm=512):
    t, d = x.shape
    dff = w_gate.shape[1]
    wn, wm = w_out[:o_nsa.shape[1]].astype(BF16), w_out[o_nsa.shape[1]:].astype(BF16)
    row = lambda width: pl.BlockSpec((tm, width), lambda i: (i, 0))
    return pl.pallas_call(
        functools.partial(_even_out_kernel, chunk=dff // 2), grid=(t // tm,),
        in_specs=[row(d), row(o_nsa.shape[1]), row(o_mla.shape[1]), _const_spec(wn.shape), _const_spec(wm.shape),
                  _const_spec((1, d)), _const_spec((d, dff)), _const_spec((d, dff)), _const_spec((dff, d))],
        out_specs=row(d), out_shape=jax.ShapeDtypeStruct((t, d), F32),
        compiler_params=_params("parallel"), name="even_out_ffn",
    )(x, o_nsa, o_mla, wn, wm, gain[None, :], w_gate.astype(BF16), w_up.astype(BF16), w_down.astype(BF16))


def _conv_kernel(x_ref, g_ref, win_ref, cw_ref, wout_ref, o_ref, vbuf_ref, *, tm):
    j = pl.program_id(1)
    x = x_ref[...]
    d = x.shape[-1]
    n = _rms(x, g_ref[...], d).astype(BF16)
    b_gate = _dot(n, win_ref[:, 0:d])
    v = _dot(n, win_ref[:, d:2 * d]) * _dot(n, win_ref[:, 2 * d:3 * d])

    @pl.when(j == 0)
    def _():
        vbuf_ref[0:8, :] = jnp.zeros((8, d), F32)

    vbuf_ref[8:8 + tm, :] = v
    cw = cw_ref[...]
    y = cw[2:3] * v + cw[1:2] * vbuf_ref[7:7 + tm, :] + cw[0:1] * vbuf_ref[6:6 + tm, :]
    vbuf_ref[0:8, :] = v[tm - 8:tm]
    o_ref[...] = x + _dot((b_gate * y).astype(BF16), wout_ref[...])


def _conv_mixer(x, gain, w_in, conv_w, w_out, tm=512):
    bsz, seq, d = x.shape
    cw = jnp.pad(conv_w.astype(F32), ((0, 8 - CONV_WIDTH), (0, 0)))
    tok = pl.BlockSpec((None, tm, d), lambda b, j: (b, j, 0))
    return pl.pallas_call(
        functools.partial(_conv_kernel, tm=tm), grid=(bsz, seq // tm),
        in_specs=[tok, _const_spec((1, d)), _const_spec((d, 3 * d)), _const_spec((8, d)), _const_spec((d, d))],
        out_specs=tok, out_shape=jax.ShapeDtypeStruct((bsz, seq, d), F32),
        scratch_shapes=[pltpu.VMEM((tm + 8, d), F32)],
        compiler_params=_params("parallel", "arbitrary"), name="conv_mixer",
    )(x, gain[None, :], w_in.astype(BF16), cw, w_out.astype(BF16))


def _router_kernel(x_ref, g_ref, wr_ref, br_ref, tri_ref, route_ref, cnt_ref, carry_ref, *, tm):
    t = pl.program_id(0)

    @pl.when(t == 0)
    def _():
        carry_ref[...] = jnp.zeros_like(carry_ref)

    n = _rms(x_ref[...], g_ref[...], x_ref.shape[-1])
    hi = n.astype(BF16)
    lo = (n - hi.astype(F32)).astype(BF16)
    whi, wlo = wr_ref[0], wr_ref[1]
    logits = _dot(hi, whi) + _dot(lo, whi) + _dot(hi, wlo) + br_ref[...]
    lane = lax.broadcasted_iota(jnp.int32, (tm, LANE), 1).astype(F32)
    big = float(LANE)
    m1 = jnp.max(logits, axis=-1, keepdims=True)
    e1 = jnp.min(jnp.where(logits == m1, lane, big), axis=-1, keepdims=True)
    rest = jnp.where(lane == e1, NEG, logits)
    m2 = jnp.max(rest, axis=-1, keepdims=True)
    e2 = jnp.min(jnp.where(rest == m2, lane, big), axis=-1, keepdims=True)
    z = jnp.exp(m2 - m1)
    w1 = 1.0 / (1.0 + z)
    w2 = z / (1.0 + z)
    oh1 = jnp.where(lane == e1, 1.0, 0.0)
    oh2 = jnp.where(lane == e2, 1.0, 0.0)
    both = oh1 + oh2
    before = _dot(tri_ref[...], both.astype(BF16)) + carry_ref[0:1, :]
    r1 = jnp.sum(oh1 * before, axis=-1, keepdims=True)
    r2 = jnp.sum(oh2 * before, axis=-1, keepdims=True)
    cols = [e1, e2, w1, w2, r1, r2]
    out = jnp.zeros((tm, LANE), F32)
    for c, val in enumerate(cols):
        out = jnp.where(lane == c, val, out)
    route_ref[...] = out
    carry_ref[0:1, :] = carry_ref[0:1, :] + jnp.sum(both, axis=0, keepdims=True)
    cnt_ref[...] = carry_ref[...]


def _moe_router(x, gain, w_router, b_router, tm=512):
    t, d = x.shape
    wr = jnp.pad(w_router.astype(F32), ((0, 0), (0, LANE - N_EXPERTS)))
    whi = wr.astype(BF16)
    wlo = (wr - whi.astype(F32)).astype(BF16)
    br = jnp.concatenate([b_router.astype(F32), jnp.full((LANE - N_EXPERTS,), NEG, F32)])[None, :]
    tri = jnp.asarray(np.tril(np.ones((tm, tm), np.float32), -1), BF16)
    route, cnt = pl.pallas_call(
        functools.partial(_router_kernel, tm=tm), grid=(t // tm,),
        in_specs=[pl.BlockSpec((tm, d), lambda i: (i, 0)), _const_spec((1, d)), _const_spec((2, d, LANE)),
                  _const_spec((1, LANE)), _const_spec((tm, tm))],
        out_specs=[pl.BlockSpec((tm, LANE), lambda i: (i, 0)), pl.BlockSpec((8, LANE), lambda i: (0, 0))],
        out_shape=[jax.ShapeDtypeStruct((t, LANE), F32), jax.ShapeDtypeStruct((8, LANE), F32)],
        scratch_shapes=[pltpu.VMEM((8, LANE), F32)],
        compiler_params=_params("arbitrary"), name="moe_router",
    )(x, gain[None, :], jnp.stack([whi, wlo]), br, tri)
    return route, cnt[0, :N_EXPERTS].astype(jnp.int32)


def _row_copy(src, i, dst, j, sem):
    return pltpu.make_async_copy(src.at[pl.ds(i, 1)], dst.at[pl.ds(j, 1)], sem)


def _scatter_kernel(fill_ref, dest_ref, x_ref, g_ref, xs_ref, xn_ref, zero_ref, sems, *, tm, rb):
    @pl.when(pl.program_id(0) == 0)
    def _():
        sem = sems.at[0]
        zero_ref[...] = jnp.zeros_like(zero_ref)
        sizes = [rb >> s for s in range(rb.bit_length() - 3)]
        for e in range(N_EXPERTS + 1):
            lo, n = fill_ref[e], fill_ref[N_EXPERTS + 1 + e] - fill_ref[e]
            whole = n // rb

            def copy(off, size):
                return pltpu.make_async_copy(zero_ref.at[pl.ds(0, size)],
                                             xs_ref.at[pl.ds(pl.multiple_of(off, 8), size)], sem)

            def blocks(k, c, lo=lo, copy=copy):
                copy(lo + k * rb, rb).start()
                copy(lo + k * rb, rb).wait()
                return c

            lax.fori_loop(0, whole, blocks, 0)
            off = lo + whole * rb
            for size in sizes[1:]:
                @pl.when((n & size) != 0)
                def _(off=off, size=size, copy=copy):
                    copy(off, size).start()
                    copy(off, size).wait()
                off = off + (n & size)

    t = pl.program_id(0)
    slot = t % 2
    rows, row_sem = xn_ref.at[slot], sems.at[slot]
    rows[...] = _rms(x_ref[...], g_ref[...], x_ref.shape[-1])

    def start(r, c):
        _row_copy(rows, r, xs_ref, dest_ref[0, 0, r], row_sem).start()
        _row_copy(rows, r, xs_ref, dest_ref[0, 0, tm + r], row_sem).start()
        return c

    lax.fori_loop(0, tm, start, 0)

    def drain(s):
        for _ in range(TOP_K):
            pltpu.make_async_copy(xn_ref.at[s], xs_ref.at[pl.ds(0, tm)], sems.at[s]).wait()

    @pl.when(t > 0)
    def _():
        drain(1 - slot)

    @pl.when(t == pl.num_programs(0) - 1)
    def _():
        drain(slot)


def _moe_scatter(x, gain, dest, fill, n_rows, tm, rb):
    t, d = x.shape
    nt = t // tm
    grid_spec = pltpu.PrefetchScalarGridSpec(
        num_scalar_prefetch=1, grid=(nt,),
        in_specs=[pl.BlockSpec((1, 1, 2 * tm), lambda i, f: (i, 0, 0), memory_space=pltpu.SMEM),
                  pl.BlockSpec((tm, d), lambda i, f: (i, 0)),
                  pl.BlockSpec((1, d), lambda i, f: (0, 0))],
        out_specs=pl.BlockSpec(memory_space=pl.ANY),
        scratch_shapes=[pltpu.VMEM((2, tm, d), F32), pltpu.VMEM((rb, d), F32), pltpu.SemaphoreType.DMA((2,))])
    return pl.pallas_call(
        functools.partial(_scatter_kernel, tm=tm, rb=rb), grid_spec=grid_spec,
        out_shape=jax.ShapeDtypeStruct((n_rows, d), F32),
        compiler_params=_params("arbitrary"), name="moe_scatter",
    )(fill, dest, x, gain[None, :])


def _expert_kernel(be_ref, nu_ref, x_ref, wg_ref, wu_ref, wd_ref, o_ref):
    used = pl.program_id(0) < nu_ref[0]

    @pl.when(used)
    def _():
        xb = x_ref[...].astype(BF16)
        gate = _dot(xb, wg_ref[...])
        act = (gate * jax.nn.sigmoid(gate) * _dot(xb, wu_ref[...])).astype(BF16)
        o_ref[...] = _dot(act, wd_ref[...])

    @pl.when(jnp.logical_not(used))
    def _():
        o_ref[...] = jnp.zeros_like(o_ref)


def _moe_experts(xs, blk_expert, n_used, w_gate, w_up, w_down, rb):
    n_rows, d = xs.shape
    dff = w_gate.shape[-1]
    rowblk = pl.BlockSpec((rb, d), lambda i, be, nu: (i, 0))
    wspec = lambda shape: pl.BlockSpec((None,) + shape, lambda i, be, nu: (be[i], 0, 0))
    grid_spec = pltpu.PrefetchScalarGridSpec(
        num_scalar_prefetch=2, grid=(n_rows // rb,),
        in_specs=[rowblk, wspec((d, dff)), wspec((d, dff)), wspec((dff, d))], out_specs=rowblk)
    return pl.pallas_call(
        _expert_kernel, grid_spec=grid_spec, out_shape=jax.ShapeDtypeStruct(xs.shape, F32),
        compiler_params=_params("arbitrary"), name="moe_experts",
    )(blk_expert, n_used, xs, w_gate.astype(BF16), w_up.astype(BF16), w_down.astype(BF16))


def _combine_kernel(dest_ref, next_ref, x_ref, route_ref, ys_ref, o_ref, got_ref, sems, *, tm):
    t = pl.program_id(0)
    slot = t % 2

    def gather(idx_ref, s):
        def start(r, c):
            for k in range(TOP_K):
                _row_copy(ys_ref, idx_ref[0, 0, k * tm + r], got_ref.at[s, k], r, sems.at[s]).start()
            return c

        lax.fori_loop(0, tm, start, 0)

    @pl.when(t == 0)
    def _():
        gather(dest_ref, 0)

    @pl.when(t + 1 < pl.num_programs(0))
    def _():
        gather(next_ref, 1 - slot)

    for k in range(TOP_K):
        pltpu.make_async_copy(ys_ref.at[pl.ds(0, tm)], got_ref.at[slot, k], sems.at[slot]).wait()
    route = route_ref[...]
    o_ref[...] = x_ref[...] + route[:, 2:3] * got_ref[slot, 0] + route[:, 3:4] * got_ref[slot, 1]


def _moe_combine(x, route, dest, ys, tm):
    t, d = x.shape
    nt = t // tm
    idx = lambda nxt: pl.BlockSpec((1, 1, TOP_K * tm), lambda i: (jnp.minimum(i + nxt, nt - 1), 0, 0),
                                   memory_space=pltpu.SMEM)
    return pl.pallas_call(
        functools.partial(_combine_kernel, tm=tm), grid=(nt,),
        in_specs=[idx(0), idx(1), pl.BlockSpec((tm, d), lambda i: (i, 0)), pl.BlockSpec((tm, LANE), lambda i: (i, 0)),
                  pl.BlockSpec(memory_space=pl.ANY)],
        out_specs=pl.BlockSpec((tm, d), lambda i: (i, 0)),
        out_shape=jax.ShapeDtypeStruct((t, d), F32),
        scratch_shapes=[pltpu.VMEM((2, TOP_K, tm, d), F32), pltpu.SemaphoreType.DMA((2,))],
        compiler_params=_params("arbitrary"), name="moe_combine",
    )(dest, dest, x, route, ys)


def _moe(x, gain, w_router, b_router, w_gate, w_up, w_down, rb=512, tm=256):
    t, d = x.shape
    route, counts = _moe_router(x, gain, w_router, b_router)
    n_blk = (t * TOP_K + rb - 1) // rb + N_EXPERTS
    padded = (counts + rb - 1) // rb * rb
    pad_end = jnp.cumsum(padded)
    pad_start = pad_end - padded
    dest = pad_start[route[:, 0:2].astype(jnp.int32)] + route[:, 4:6].astype(jnp.int32)
    dest_t = jnp.concatenate([dest[:, 0].reshape(t // tm, 1, tm), dest[:, 1].reshape(t // tm, 1, tm)], axis=2)
    fill = jnp.concatenate([(pad_start + counts) // 8 * 8, pad_end[-1:], pad_end, jnp.full((1,), n_blk * rb, jnp.int32)])
    n_used = pad_end[-1:] // rb
    blk = jnp.minimum(jnp.arange(n_blk, dtype=jnp.int32), n_used - 1) * rb
    blk_expert = jnp.sum(pad_end[None, :] <= blk[:, None], axis=1).astype(jnp.int32)
    xs = _moe_scatter(x, gain, dest_t, fill.astype(jnp.int32), n_blk * rb, tm, rb)
    ys = _moe_experts(xs, blk_expert, n_used.astype(jnp.int32), w_gate, w_up, w_down, rb)
    return _moe_combine(x, route, dest_t, ys, tm)


def kernel(x, rel_bias, ev_mix_norm, ev_w_in, nsa_q_norm, nsa_k_norm, nsa_cmp_pos, nsa_cmp_w1, nsa_cmp_w2, mla_cq_norm, mla_ckv_norm, mla_w_uq, mla_w_ukv, mla_q_norm, mla_k_norm, ev_w_out, ev_ffn_norm, ffn_w_gate, ffn_w_up, ffn_w_down, od_mix_norm, od_w_in, conv_w, od_w_out, od_ffn_norm, moe_w_router, moe_b_router, moe_w_gate, moe_w_up, moe_w_down):
    bsz, seq, d = x.shape
    depth = ev_mix_norm.shape[0] + od_mix_norm.shape[0]
    pad = 512
    for layer in range(depth):
        i = layer // 2
        if layer % 2 == 0:
            qn, kv, kc_raw, vc_raw, gates, qm, km, vm = _even_in_proj(
                x, ev_mix_norm[i], ev_w_in[i], nsa_q_norm[i], nsa_k_norm[i], mla_cq_norm[i], mla_ckv_norm[i],
                mla_w_uq[i], mla_w_ukv[i], mla_q_norm[i], mla_k_norm[i], tm=pad)
            kc = _compress(kc_raw, nsa_cmp_pos[i, 0], nsa_cmp_w1[i, 0], nsa_cmp_w2[i, 0], nsa_k_norm[i, 0], True)
            vc = _compress(vc_raw, nsa_cmp_pos[i, 1], nsa_cmp_w1[i, 1], nsa_cmp_w2[i, 1], nsa_k_norm[i, 0], False)
            o_nsa = _nsa_attention(qn, kv, kc, vc, gates, rel_bias, pad)
            o_mla = _mla_attention(qm, km, vm)
            x = _even_out_ffn(x.reshape(bsz * seq, d), o_nsa.reshape(bsz * seq, -1), o_mla.reshape(bsz * seq, -1),
                              ev_w_out[i], ev_ffn_norm[i], ffn_w_gate[i], ffn_w_up[i], ffn_w_down[i]).reshape(bsz, seq, d)
        else:
            x = _conv_mixer(x, od_mix_norm[i], od_w_in[i], conv_w[i], od_w_out[i])
            x = _moe(x.reshape(bsz * seq, d), od_ffn_norm[i], moe_w_router[i], moe_b_router[i],
                     moe_w_gate[i], moe_w_up[i], moe_w_down[i]).reshape(bsz, seq, d)
    return x
```

```python
import functools
import math

import jax
import jax.numpy as jnp
import numpy as np
from jax import lax
from jax.experimental import pallas as pl
from jax.experimental.pallas import tpu as pltpu

F32 = jnp.float32
BF16 = jnp.bfloat16

EPS = 1e-6
NEG = -1e30
FORCED_SCORE = 1e9
NSA_HEADS = 8
NSA_GROUPS = 2
NSA_REP = NSA_HEADS // NSA_GROUPS
NSA_DK = 64
CMP_BLOCK = 32
CMP_STRIDE = 16
CMP_HIDDEN = 256
SEL_BLOCK = 64
SEL_TOP_N = 16
WINDOW = 512
MLA_HEADS = 8
MLA_Q_RANK = 256
MLA_KV_RANK = 128
MLA_NOPE = 64
MLA_ROPE = 32
MLA_V = 64
ROPE_THETA = 10000.0
REL_BUCKETS = 32
REL_MAX_DIST = 128
CONV_WIDTH = 3
N_EXPERTS = 8
TOP_K = 2
EVEN_IN_SIZES = (512,) + (128,) * 6 + (24, 256, 128, 32)

LANE = 128
VMEM_LIMIT = 56 * 1024 * 1024
SEL_MASK = -30000.0
FAR_DIST = 128

ROW_TILE = 512
KV_PAD = ROW_TILE
MOE_ROW_BLOCK = 512
MOE_DMA_TILE = 512

_Q0, _KV0, _KC0, _VC0, _GT0, _CQ0, _CKV0, _KR0, _EVEN_W = 0, 1024, 2048, 2176, 2304, 2560, 2816, 2944, 3072


def _dot(a, b):
    return jnp.dot(a, b, preferred_element_type=F32)


def _dot_nt(a, b):
    return lax.dot_general(a, b, (((1,), (1,)), ((), ())), preferred_element_type=F32)


def _rms(x, gain, n):
    ss = jnp.sum(x * x, axis=-1, keepdims=True) * (1.0 / n)
    return x * lax.rsqrt(ss + EPS) * gain


def _params(*sem):
    return pltpu.CompilerParams(dimension_semantics=sem, vmem_limit_bytes=VMEM_LIMIT)


def _const_spec(shape):
    nd = len(shape)
    return pl.BlockSpec(shape, lambda *_: (0,) * nd, pipeline_mode=pl.Buffered(1))


def _even_in_kernel(x_ref, gmix_ref, w_ref, gq_ref, gks_ref, gkw_ref, gcq_ref, gckv_ref,
                    wuq_ref, wuk_ref, wuv_ref, gmq_ref, gmk_ref, cos_ref, sa_ref, sb_ref,
                    qn_ref, kv_ref, kc_ref, vc_ref, gate_ref, qm_ref, km_ref, vm_ref, *, tm):
    j = pl.program_id(1)

    @pl.when(j == 0)
    def _():
        kv_ref[...] = jnp.zeros_like(kv_ref)
        qn_ref[...] = jnp.zeros_like(qn_ref)
        kc_ref[...] = jnp.zeros_like(kc_ref)
        vc_ref[...] = jnp.zeros_like(vc_ref)
        gate_ref[...] = jnp.zeros_like(gate_ref)
        qm_ref[...] = jnp.zeros_like(qm_ref)
        km_ref[...] = jnp.zeros_like(km_ref)
        vm_ref[...] = jnp.zeros_like(vm_ref)

    @pl.when(j > 0)
    def _():
        xn = _rms(x_ref[...], gmix_ref[...], x_ref.shape[-1]).astype(BF16)
        hq = _dot(xn, w_ref[:, _Q0:_Q0 + 1024])
        hkv = _dot(xn, w_ref[:, _KV0:_KV0 + 1024])
        hc = _dot(xn, w_ref[:, _KC0:_KC0 + 256])
        hg = _dot(xn, w_ref[:, _GT0:_GT0 + 256])
        hm = _dot(xn, w_ref[:, _CQ0:_EVEN_W])
        cq = _rms(hm[:, :MLA_Q_RANK], gcq_ref[...], MLA_Q_RANK).astype(BF16)
        ckv = _rms(hm[:, MLA_Q_RANK:MLA_Q_RANK + MLA_KV_RANK], gckv_ref[...], MLA_KV_RANK).astype(BF16)
        k_rope = hm[:, MLA_Q_RANK + MLA_KV_RANK:]
        qm = _dot(cq, wuq_ref[...])
        kn = _dot(ckv, wuk_ref[...])
        vm_ref[...] = _dot(ckv, wuv_ref[...]).astype(BF16)
        dqk = MLA_NOPE + MLA_ROPE
        nsa_heads = [slice(LANE * h, LANE * (h + 1)) for h in range(NSA_HEADS)]
        mla_heads = [slice(LANE * h, LANE * (h + 1)) for h in range(MLA_HEADS)]
        segs = [(hq[:, sl], gq_ref[...], NSA_DK, NSA_DK ** -0.5) for sl in nsa_heads]
        for g in range(NSA_GROUPS):
            o = 4 * LANE * g
            segs.append((hkv[:, o:o + LANE], gks_ref[...], NSA_DK, 1.0))
            segs.append((hkv[:, o + 2 * LANE:o + 3 * LANE], gkw_ref[...], NSA_DK, 1.0))
        n_nsa = len(segs)
        segs += [(qm[:, sl], gmq_ref[...], dqk, dqk ** -0.5) for sl in mla_heads]
        segs += [(kn[:, sl] + k_rope, gmk_ref[...], dqk, 1.0) for sl in mla_heads]
        sums = [jnp.sum(x * x, axis=-1, keepdims=True) for x, _, _, _ in segs]
        scales = [lax.rsqrt(ss * (1.0 / n) + EPS) for ss, (_, _, n, _) in zip(sums, segs)]
        normed = [x * sc * gain for sc, (x, gain, _, _) in zip(scales, segs)]
        for h, sl in enumerate(nsa_heads):
            qn_ref[:, sl] = (normed[h] * segs[h][3]).astype(BF16)
        pos = (j - 1) * tm + lax.broadcasted_iota(jnp.int32, (tm, LANE), 0)
        lane = lax.broadcasted_iota(jnp.int32, (tm, LANE), 1)
        onehot = jnp.where(lane - NSA_DK == pos // SEL_BLOCK, 1.0, 0.0)
        ones = jnp.where(lane >= NSA_DK, 1.0, 0.0)
        for g in range(NSA_GROUPS):
            o = 4 * LANE * g
            kv_ref[:, o:o + LANE] = (normed[NSA_HEADS + 2 * g] + onehot).astype(BF16)
            kv_ref[:, o + LANE:o + 2 * LANE] = (hkv[:, o + LANE:o + 2 * LANE] + ones).astype(BF16)
            kv_ref[:, o + 2 * LANE:o + 3 * LANE] = normed[NSA_HEADS + 2 * g + 1].astype(BF16)
            kv_ref[:, o + 3 * LANE:o + 4 * LANE] = (hkv[:, o + 3 * LANE:o + 4 * LANE] + ones).astype(BF16)
        kc_ref[...] = hc[:, :LANE].astype(BF16)
        vc_ref[...] = hc[:, LANE:].astype(BF16)
        gate_ref[...] = jax.nn.sigmoid(hg)
        cos, sa, sb = cos_ref[...], sa_ref[...], sb_ref[...]
        mla = normed[n_nsa:]
        fwd = [pltpu.roll(x, 16, 1) for x in mla]
        bwd = [pltpu.roll(x, LANE - 16, 1) for x in mla]
        roped = [x * cos + f * sa + b * sb for x, f, b in zip(mla, fwd, bwd)]
        for h, sl in enumerate(mla_heads):
            qm_ref[:, sl] = (roped[h] * segs[n_nsa + h][3]).astype(BF16)
            km_ref[:, sl] = roped[MLA_HEADS + h].astype(BF16)


def _even_in_weights(w_in):
    offs = np.concatenate([[0], np.cumsum(EVEN_IN_SIZES)])
    part = [w_in[:, offs[n]:offs[n + 1]] for n in range(len(EVEN_IN_SIZES))]
    q, k_c, v_c, k_s, v_s, k_w, v_w, gates, c_q, c_kv, k_rope = part
    d = w_in.shape[0]
    z = lambda n: jnp.zeros((d, n), w_in.dtype)
    cols = []
    for h in range(NSA_HEADS):
        cols += [q[:, 64 * h:64 * h + 64], z(64)]
    for g in range(NSA_GROUPS):
        s = slice(64 * g, 64 * g + 64)
        cols += [k_s[:, s], z(64), v_s[:, s], z(64), k_w[:, s], z(64), v_w[:, s], z(64)]
    cols += [k_c, v_c]
    for g in range(NSA_GROUPS):
        cols += [gates[:, 12 * g:12 * g + 12], z(LANE - 12)]
    cols += [c_q, c_kv, z(64), k_rope, z(32)]
    w = jnp.concatenate(cols, axis=1)
    assert w.shape[1] == _EVEN_W
    return w.astype(BF16)


def _pad_gain(g, width):
    return jnp.pad(g.astype(F32), (0, width - g.shape[0]))[None, :]


def _rope_tables(seq):
    half = MLA_ROPE // 2
    inv_freq = ROPE_THETA ** (-jnp.arange(half, dtype=F32) / half)
    ang = jnp.arange(seq).astype(F32)[:, None] * inv_freq[None, :]
    cos, sin = jnp.cos(ang), jnp.sin(ang)
    one = jnp.ones((seq, MLA_NOPE), F32)
    zn = jnp.zeros((seq, MLA_NOPE), F32)
    zt = jnp.zeros((seq, LANE - MLA_NOPE - MLA_ROPE), F32)
    zh = jnp.zeros((seq, half), F32)
    cos_t = jnp.concatenate([one, cos, cos, zt + 1.0], axis=1)
    sa = jnp.concatenate([zn, zh, sin, zt], axis=1)
    sb = jnp.concatenate([zn, -sin, zh, zt], axis=1)
    return cos_t, sa, sb


def _even_in_proj(x, gmix, w_in, q_norm, k_norm, cq_norm, ckv_norm, w_uq, w_ukv, mq_norm, mk_norm, tm=ROW_TILE):
    bsz, seq, d = x.shape
    nt = seq // tm
    w = _even_in_weights(w_in)
    dqk = MLA_NOPE + MLA_ROPE
    wuq = jnp.concatenate(
        [jnp.pad(w_uq[:, dqk * h:dqk * (h + 1)], ((0, 0), (0, LANE - dqk))) for h in range(MLA_HEADS)],
        axis=1).astype(BF16)
    kvw = MLA_NOPE + MLA_V
    wuk = jnp.concatenate(
        [jnp.pad(w_ukv[:, kvw * h:kvw * h + MLA_NOPE], ((0, 0), (0, LANE - MLA_NOPE))) for h in range(MLA_HEADS)],
        axis=1).astype(BF16)
    wuv = jnp.concatenate([w_ukv[:, kvw * h + MLA_NOPE:kvw * (h + 1)] for h in range(MLA_HEADS)], axis=1).astype(BF16)
    cos_t, sa, sb = _rope_tables(seq)
    tok = lambda width: pl.BlockSpec((None, tm, width), lambda b, j: (b, jnp.maximum(j - 1, 0), 0))
    postab = pl.BlockSpec((tm, LANE), lambda b, j: (jnp.maximum(j - 1, 0), 0))
    in_specs = [tok(d), _const_spec((1, d)), _const_spec((d, _EVEN_W)),
                _const_spec((1, LANE)), _const_spec((1, LANE)), _const_spec((1, LANE)),
                _const_spec((1, MLA_Q_RANK)), _const_spec((1, MLA_KV_RANK)),
                _const_spec(wuq.shape), _const_spec(wuk.shape), _const_spec(wuv.shape),
                _const_spec((1, LANE)), _const_spec((1, LANE)), postab, postab, postab]
    out_shape = [jax.ShapeDtypeStruct((bsz, seq, 1024), BF16),
                 jax.ShapeDtypeStruct((bsz, seq + tm, 1024), BF16),
                 jax.ShapeDtypeStruct((bsz, seq, LANE), BF16),
                 jax.ShapeDtypeStruct((bsz, seq, LANE), BF16),
                 jax.ShapeDtypeStruct((bsz, seq, 2 * LANE), F32),
                 jax.ShapeDtypeStruct((bsz, seq, 1024), BF16),
                 jax.ShapeDtypeStruct((bsz, seq, 1024), BF16),
                 jax.ShapeDtypeStruct((bsz, seq, 512), BF16)]
    out_specs = [tok(1024), pl.BlockSpec((None, tm, 1024), lambda b, j: (b, j, 0)), tok(LANE), tok(LANE),
                 tok(2 * LANE), tok(1024), tok(1024), tok(512)]
    return pl.pallas_call(
        functools.partial(_even_in_kernel, tm=tm), grid=(bsz, nt + 1), in_specs=in_specs, out_specs=out_specs,
        out_shape=out_shape, compiler_params=_params("parallel", "arbitrary"), name="even_in_proj",
    )(x, gmix[None, :], w, _pad_gain(q_norm, LANE), _pad_gain(k_norm[1], LANE), _pad_gain(k_norm[2], LANE),
      cq_norm[None, :], ckv_norm[None, :], wuq, wuk, wuv, _pad_gain(mq_norm, LANE), _pad_gain(mk_norm, LANE),
      cos_t, sa, sb)


def _cmp_kernel(x_ref, wa_ref, wb_ref, pa_ref, pb_ref, w2_ref, gain_ref, o_ref, *, normalize):
    x = x_ref[...]
    ua = _dot(x, wa_ref[...])
    ub = _dot(x, wb_ref[...])
    pt = _dot(pa_ref[...], wa_ref[...]) + _dot(pb_ref[...], wb_ref[...])
    n = ub.shape[0]
    pre = ua + pltpu.roll(ub, n - 1, 0) + pt[0:1]
    hid = jax.nn.gelu(pre).astype(BF16)
    out = _dot(hid, w2_ref[...])
    if normalize:
        gain = gain_ref[...]
        for g in range(NSA_GROUPS):
            sl = slice(LANE * g, LANE * (g + 1))
            o_ref[:, sl] = _rms(out[:, sl], gain, NSA_DK).astype(BF16)
    else:
        o_ref[...] = out.astype(BF16)


def _compress(kv, pos_emb, w1, w2, gain, normalize):
    bsz, seq, _ = kv.shape
    half = CMP_BLOCK // 2
    assert CMP_STRIDE == half
    nrow = seq // CMP_STRIDE
    x = kv.reshape(bsz, nrow, CMP_STRIDE * LANE)
    w1r = w1.reshape(CMP_BLOCK, NSA_DK, CMP_HIDDEN)
    eye = jnp.eye(NSA_GROUPS, dtype=w1.dtype)
    widen = lambda w: jnp.einsum("ldn,gh->lgdhn", w, eye).reshape(half * LANE, NSA_GROUPS * CMP_HIDDEN).astype(BF16)
    wa, wb = widen(w1r[:half]), widen(w1r[half:])
    prow = lambda p: jnp.pad(jnp.broadcast_to(p[:, None, :], (half, NSA_GROUPS, NSA_DK)).reshape(1, half * LANE),
                             ((0, 15), (0, 0))).astype(BF16)
    pa, pb = prow(pos_emb[:half]), prow(pos_emb[half:])
    second = jnp.zeros_like(w2) if normalize else w2
    w2w = jnp.einsum("nd,gh->gnhd", jnp.concatenate([w2, second], axis=1), eye)
    w2w = w2w.reshape(NSA_GROUPS * CMP_HIDDEN, NSA_GROUPS * LANE).astype(BF16)
    return pl.pallas_call(
        functools.partial(_cmp_kernel, normalize=normalize), grid=(bsz,),
        in_specs=[pl.BlockSpec((None, nrow, CMP_STRIDE * LANE), lambda b: (b, 0, 0)),
                  _const_spec(wa.shape), _const_spec(wb.shape), _const_spec(pa.shape), _const_spec(pb.shape),
                  _const_spec(w2w.shape), _const_spec((1, LANE))],
        out_specs=pl.BlockSpec((None, nrow, NSA_GROUPS * LANE), lambda b: (b, 0, 0)),
        out_shape=jax.ShapeDtypeStruct((bsz, nrow, NSA_GROUPS * LANE), BF16),
        compiler_params=_params("parallel"), name="nsa_compress",
    )(x, wa, wb, pa, pb, w2w, _pad_gain(gain, LANE))


def _bucket_table():
    dist = np.arange(FAR_DIST + 1)
    max_exact = REL_BUCKETS // 2
    nf = np.maximum(dist, max_exact).astype(np.float32)
    large = max_exact + (np.log(nf / max_exact) / math.log(REL_MAX_DIST / max_exact)
                         * (REL_BUCKETS - max_exact)).astype(np.int32)
    return np.where(dist < max_exact, dist, np.minimum(large, REL_BUCKETS - 1))


CMP_NEAR = 32


def _expand(tbl, idx):
    idx = np.asarray(idx)
    onehot = jnp.asarray(np.eye(tbl.shape[1], dtype=np.float32)[idx.reshape(-1)])
    out = lax.dot_general(tbl, onehot, (((1,), (1,)), ((), ())), precision=lax.Precision.HIGHEST)
    return out.reshape((tbl.shape[0],) + idx.shape)


def _toeplitz_tile(tbl, far, width, dist0, valid):
    period = width + SEL_BLOCK + 1
    j = np.arange(period)
    dist = dist0 - np.where(j < width, j, j - period)
    gen = jnp.where(jnp.asarray(valid(dist))[None], _expand(tbl, np.clip(dist, 0, FAR_DIST)) - far, NEG)
    flat = jnp.tile(gen, (1, SEL_BLOCK))[:, :SEL_BLOCK * (period - 1)]
    return flat.reshape(tbl.shape[0], SEL_BLOCK, period - 1)[:, :, :width]


def _bias_tables(rel_bias):
    tbl = _expand(rel_bias.astype(F32).T, _bucket_table())
    far = tbl[:, FAR_DIST:]
    bias_near = _toeplitz_tile(tbl, far, 4 * SEL_BLOCK, 3 * SEL_BLOCK, lambda d: d >= 0)
    bias_win = _toeplitz_tile(tbl, 0.0, WINDOW + SEL_BLOCK, WINDOW, lambda d: (d >= 0) & (d < WINDOW))
    d_c = (np.arange(SEL_BLOCK)[:, None] + (CMP_NEAR // 2) * CMP_STRIDE - (CMP_BLOCK - 1)
           - CMP_STRIDE * np.arange(CMP_NEAR)[None, :])
    assert d_c[:, 0].min() >= FAR_DIST and d_c[:, -1].max() < 0
    bias_cmp = jnp.where(jnp.asarray(d_c >= 0)[None], _expand(tbl, np.clip(d_c, 0, FAR_DIST)) - far[:, :, None], 0.0)
    return bias_cmp, bias_near, bias_win


def _cmp_mask_dist(seq):
    a = np.tile(np.arange(SEL_BLOCK), NSA_REP)[:, None]
    c = np.arange(seq // CMP_STRIDE)[None, :]
    return jnp.asarray(c * CMP_STRIDE + CMP_BLOCK - 1 - a, jnp.int32)


def _overlap_t(seq):
    ncp = seq // CMP_STRIDE
    cs = np.arange(ncp)[None, :] * CMP_STRIDE
    ss = np.arange(SEL_BLOCK)[:, None] * SEL_BLOCK
    ov = (cs < ss + SEL_BLOCK) & (cs + CMP_BLOCK - 1 >= ss) & (np.arange(ncp)[None, :] < (seq - CMP_BLOCK) // CMP_STRIDE + 1)
    return jnp.asarray(ov, BF16)


def _softmax_step(s, v, state):
    m, l, acc = state
    mn = jnp.maximum(m, jnp.max(s, axis=-1, keepdims=True))
    alpha = jnp.exp(m - mn)
    p = jnp.exp(s - mn)
    return mn, alpha * l + jnp.sum(p, axis=-1, keepdims=True), alpha * acc + _dot(p.astype(BF16), v)


FAR_CHUNK = 1024


def _lane_fold(x, op):
    out = x[:, 0:LANE]
    for c in range(1, x.shape[1] // LANE):
        out = op(out, x[:, LANE * c:LANE * (c + 1)])
    return out


SEL_TILE = 8


def _nsa_local_kernel(q_ref, kv_ref, kc_ref, vc_ref, g_ref, bc_ref, bw_ref, ovt_ref, dmask_ref, neg_ref, oc_ref, *, pad):
    t = pl.program_id(1)
    qb, rep, ng = SEL_BLOCK, NSA_REP, NSA_GROUPS
    rows = qb * rep
    gw = 4 * LANE
    ncp = kc_ref.shape[0]
    wk = WINDOW + qb
    m_i = lax.broadcasted_iota(jnp.int32, (CMP_NEAR, ncp), 0)
    c_i = lax.broadcasted_iota(jnp.int32, (CMP_NEAR, ncp), 1)
    w_lane = lax.broadcasted_iota(jnp.int32, (rows, wk), 1)
    dist = dmask_ref[...]
    ovt = ovt_ref[...]
    gates = g_ref[...]
    low = lax.broadcasted_iota(jnp.int32, (qb, LANE), 1) < NSA_DK
    jj = lax.broadcasted_iota(jnp.int32, (qb, ng * qb), 0)
    bases = []
    for g in range(ng):
        base = bc_ref[rep * g:rep * (g + 1)].reshape(rows, CMP_NEAR)
        b_hi = base.astype(BF16)
        bases.append((b_hi, (base - b_hi.astype(F32)).astype(BF16)))
    blocks = [(u, t * SEL_TILE + u, slice(qb * u, qb * (u + 1))) for u in range(SEL_TILE)]
    chains = [(u, i, tok, g) for u, i, tok in blocks for g in range(ng)]
    qq = [jnp.concatenate([q_ref[tok, gw * g + LANE * r:gw * g + LANE * (r + 1)] for r in range(rep)], axis=0)
          for _, _, tok, g in chains]
    shift = [jnp.where(c_i - m_i == (qb // CMP_STRIDE) * i - CMP_NEAR // 2, 1.0, 0.0).astype(BF16) for _, i, _ in blocks]
    s_c = [jnp.where(dist <= i * qb,
                     _dot_nt(qq[n], kc_ref[:, LANE * g:LANE * (g + 1)]) + _dot(bases[g][0], shift[u])
                     + _dot(bases[g][1], shift[u]), NEG) for n, (u, i, _, g) in enumerate(chains)]
    s_w = [jnp.where(w_lane + i * qb - WINDOW >= 0,
                     _dot_nt(qq[n], kv_ref[pl.ds(pl.multiple_of(pad - WINDOW + i * qb, qb), wk),
                                           gw * g + 2 * LANE:gw * g + 3 * LANE])
                     + bw_ref[rep * g:rep * (g + 1)].reshape(rows, wk), NEG) for n, (u, i, _, g) in enumerate(chains)]
    e_c = [jnp.exp(s - jnp.maximum(jnp.max(s, axis=-1, keepdims=True), -1e20)) for s in s_c]
    e_w = [jnp.exp(s - jnp.max(s, axis=-1, keepdims=True)) for s in s_w]
    inv = [1.0 / jnp.maximum(jnp.sum(e, axis=-1, keepdims=True), 1e-30) for e in e_c]
    o_c = [_dot(e_c[n].astype(BF16), vc_ref[:, LANE * g:LANE * (g + 1)]) * inv[n] for n, (_, _, _, g) in enumerate(chains)]
    pv_w = [_dot(e_w[n].astype(BF16), kv_ref[pl.ds(pl.multiple_of(pad - WINDOW + i * qb, qb), wk),
                                            gw * g + 3 * LANE:gw * g + 4 * LANE]) for n, (_, i, _, g) in enumerate(chains)]
    o_w = [pv / pv[:, NSA_DK:NSA_DK + 1] for pv in pv_w]
    for n, (_, _, tok, g) in enumerate(chains):
        part = []
        for r in range(rep):
            sl = slice(qb * r, qb * (r + 1))
            c0 = LANE * g + 3 * r
            part.append(gates[tok, c0:c0 + 1] * o_c[n][sl] + gates[tok, c0 + 2:c0 + 3] * o_w[n][sl])
        for pr in range(rep // 2):
            c0 = 2 * LANE * g + LANE * pr
            oc_ref[tok, c0:c0 + LANE] = jnp.where(low, part[2 * pr], pltpu.roll(part[2 * pr + 1], NSA_DK, 1))
    imps = []
    for u, i, _ in blocks:
        p = [e_c[ng * u + g] * inv[ng * u + g] for g in range(ng)]
        psum = jnp.concatenate([x[0:qb] + x[qb:2 * qb] + x[2 * qb:3 * qb] + x[3 * qb:4 * qb] for x in p], axis=0)
        hi = psum.astype(BF16)
        r1 = psum - hi.astype(F32)
        mid = r1.astype(BF16)
        lo = (r1 - mid.astype(F32)).astype(BF16)
        imp = _dot_nt(ovt, hi) + _dot_nt(ovt, mid) + _dot_nt(ovt, lo)
        imp = jnp.where((jj == i) | (jj == 0), FORCED_SCORE, imp)
        imps.append(jnp.where(jj > i, -1.0, imp))
    grp = [[imp[8 * a:8 * a + 8] for a in range(8)] for imp in imps]
    cnt = [[jnp.zeros((8, ng * qb), F32) for _ in range(8)] for _ in imps]
    sub = lax.broadcasted_iota(jnp.int32, (8, ng * qb), 0)
    for k in range(qb):
        for u in range(SEL_TILE):
            rk = imps[u][k:k + 1, :]
            for a in range(8):
                if 8 * a + 7 <= k:
                    cnt[u][a] = cnt[u][a] + jnp.where(rk > grp[u][a], 1.0, 0.0)
                elif 8 * a > k:
                    cnt[u][a] = cnt[u][a] + jnp.where(rk >= grp[u][a], 1.0, 0.0)
                else:
                    cnt[u][a] = cnt[u][a] + jnp.where(sub + 8 * a > k, jnp.where(rk >= grp[u][a], 1.0, 0.0),
                                                      jnp.where(rk > grp[u][a], 1.0, 0.0))
    for u in range(SEL_TILE):
        neg_ref[u] = jnp.where(jnp.concatenate(cnt[u], axis=0) < SEL_TOP_N, 0.0, SEL_MASK)


ATT_TILE = 2


def _nsa_attend_kernel(q_ref, kv_ref, g_ref, neg_ref, oc_ref, bn_ref, o_ref, s_ref, *, pad):
    t = pl.program_id(1)
    qb, rep, ng = SEL_BLOCK, NSA_REP, NSA_GROUPS
    rows = qb * rep
    gw = 4 * LANE
    blocks = [(u, t * ATT_TILE + u, slice(qb * u, qb * (u + 1))) for u in range(ATT_TILE)]
    chains = [(u, i, tok, g) for u, i, tok in blocks for g in range(ng)]
    nc = len(chains)
    qs = [jnp.concatenate([q_ref[tok, gw * g + LANE * r:gw * g + LANE * (r + 1)] for r in range(rep)], axis=0)
          for _, _, tok, g in chains]
    jj = lax.broadcasted_iota(jnp.int32, (qb, ng * qb), 0)
    q_lane = lax.broadcasted_iota(jnp.int32, (qb, ng * qb), 1)
    zero = jnp.zeros((qb, ng * qb), BF16)
    ext_near = [jnp.concatenate([zero, neg_ref[u].astype(BF16)], axis=0) for u, _, _ in blocks]
    ext_far = [jnp.concatenate([zero, jnp.where(jj >= i - 3, SEL_MASK, neg_ref[u]).astype(BF16)], axis=0)
               for u, i, _ in blocks]
    pick = [jnp.where(q_lane == jj + qb * g, 1.0, 0.0).astype(BF16) for g in range(ng)]
    q_near = [qs[n] + jnp.concatenate([_dot_nt(pick[g], ext_near[u]).astype(BF16)] * rep, axis=0)
              for n, (u, _, _, g) in enumerate(chains)]
    q_far = [qs[n] + jnp.concatenate([_dot_nt(pick[g], ext_far[u]).astype(BF16)] * rep, axis=0)
             for n, (u, _, _, g) in enumerate(chains)]

    n_lane = lax.broadcasted_iota(jnp.int32, (rows, 4 * qb), 1)
    near_at = [pl.multiple_of(pad + (i - 3) * qb, qb) for _, i, _ in blocks]
    s_near = [jnp.where(n_lane + (i - 3) * qb >= 0,
                        _dot_nt(q_near[n], kv_ref[pl.ds(near_at[u], 4 * qb), gw * g:gw * g + LANE])
                        + bn_ref[rep * g:rep * (g + 1)].reshape(rows, 4 * qb), NEG) for n, (u, i, _, g) in enumerate(chains)]

    nch = (jnp.maximum(blocks[-1][1] - 3, 0) * qb + FAR_CHUNK - 1) // FAR_CHUNK

    def pass1(c, mx):
        st = pl.multiple_of(pad + c * FAR_CHUNK, math.gcd(pad, FAR_CHUNK))
        col = pl.multiple_of(c * FAR_CHUNK, FAR_CHUNK)
        out = []
        for n, (_, _, _, g) in enumerate(chains):
            s = _dot_nt(q_far[n], kv_ref[pl.ds(st, FAR_CHUNK), gw * g:gw * g + LANE])
            s_ref[n, :, pl.ds(col, FAR_CHUNK)] = s
            out.append(jnp.maximum(mx[n], _lane_fold(s, jnp.maximum)))
        return tuple(out)

    mx = lax.fori_loop(0, nch, pass1, tuple(jnp.full((rows, LANE), NEG, F32) for _ in range(nc)))
    ms = [jnp.max(jnp.maximum(mx[n], _lane_fold(s_near[n], jnp.maximum)), axis=-1, keepdims=True) for n in range(nc)]

    def pass2(c, state):
        st = pl.multiple_of(pad + c * FAR_CHUNK, math.gcd(pad, FAR_CHUNK))
        col = pl.multiple_of(c * FAR_CHUNK, FAR_CHUNK)
        out = []
        for n, (_, _, _, g) in enumerate(chains):
            p = jnp.exp(s_ref[n, :, pl.ds(col, FAR_CHUNK)] - ms[n])
            out.append(state[n] + _dot(p.astype(BF16), kv_ref[pl.ds(st, FAR_CHUNK), gw * g + LANE:gw * g + 2 * LANE]))
        return tuple(out)

    far = lax.fori_loop(0, nch, pass2, tuple(jnp.zeros((rows, LANE), F32) for _ in range(nc)))
    p_near = [jnp.exp(s_near[n] - ms[n]).astype(BF16) for n in range(nc)]
    pv = [far[n] + _dot(p_near[n], kv_ref[pl.ds(near_at[u], 4 * qb), gw * g + LANE:gw * g + 2 * LANE])
          for n, (u, _, _, g) in enumerate(chains)]
    o_s = [x / x[:, NSA_DK:NSA_DK + 1] for x in pv]

    gates = g_ref[...]
    low = lax.broadcasted_iota(jnp.int32, (qb, LANE), 1) < NSA_DK
    for n, (_, _, tok, g) in enumerate(chains):
        outs = [gates[tok, LANE * g + 3 * r + 1:LANE * g + 3 * r + 2] * o_s[n][qb * r:qb * (r + 1)] for r in range(rep)]
        for pr in range(rep // 2):
            c0 = 2 * LANE * g + LANE * pr
            pair = jnp.where(low, outs[2 * pr], pltpu.roll(outs[2 * pr + 1], NSA_DK, 1))
            o_ref[tok, c0:c0 + LANE] = (oc_ref[tok, c0:c0 + LANE] + pair).astype(o_ref.dtype)


def _nsa_attention(qn, kv, kc, vc, gates, rel_bias, pad):
    bsz, seq, _ = qn.shape
    qb, ng = SEL_BLOCK, NSA_GROUPS
    nq = seq // qb
    ncp = seq // CMP_STRIDE
    assert nq <= qb and nq % SEL_TILE == 0 and nq % ATT_TILE == 0 and pad >= WINDOW and seq % FAR_CHUNK == 0
    bias_cmp, bias_near, bias_win = _bias_tables(rel_bias)
    ovt = _overlap_t(seq)
    dmask = _cmp_mask_dist(seq)
    width = NSA_HEADS * NSA_DK
    tile = lambda n, w: pl.BlockSpec((None, n, w), lambda b, i: (b, i, 0))
    per_b = lambda n, w: pl.BlockSpec((None, n, w), lambda b, i: (b, 0, 0))
    neg, o_local = pl.pallas_call(
        functools.partial(_nsa_local_kernel, pad=pad), grid=(bsz, nq // SEL_TILE),
        in_specs=[tile(SEL_TILE * qb, ng * 4 * LANE), per_b(seq + pad, ng * 4 * LANE), per_b(ncp, ng * LANE),
                  per_b(ncp, ng * LANE), tile(SEL_TILE * qb, ng * LANE), _const_spec(bias_cmp.shape),
                  _const_spec(bias_win.shape), _const_spec(ovt.shape), _const_spec(dmask.shape)],
        out_specs=[pl.BlockSpec((None, SEL_TILE, qb, ng * qb), lambda b, i: (b, i, 0, 0)), tile(SEL_TILE * qb, width)],
        out_shape=[jax.ShapeDtypeStruct((bsz, nq, qb, ng * qb), F32), jax.ShapeDtypeStruct((bsz, seq, width), F32)],
        compiler_params=_params("parallel", "arbitrary"), name="nsa_local",
    )(qn, kv, kc, vc, gates, bias_cmp, bias_win, ovt, dmask)
    return pl.pallas_call(
        functools.partial(_nsa_attend_kernel, pad=pad), grid=(bsz, nq // ATT_TILE),
        in_specs=[tile(ATT_TILE * qb, ng * 4 * LANE), per_b(seq + pad, ng * 4 * LANE), tile(ATT_TILE * qb, ng * LANE),
                  pl.BlockSpec((None, ATT_TILE, qb, ng * qb), lambda b, i: (b, i, 0, 0)), tile(ATT_TILE * qb, width),
                  _const_spec(bias_near.shape)],
        out_specs=tile(ATT_TILE * qb, width),
        out_shape=jax.ShapeDtypeStruct((bsz, seq, width), BF16),
        scratch_shapes=[pltpu.VMEM((ATT_TILE * ng, NSA_REP * qb, seq), F32)],
        compiler_params=_params("parallel", "arbitrary"), name="nsa_attention",
    )(qn, kv, gates, neg, o_local, bias_near)


def _mla_attn_kernel(q_ref, k_ref, v_ref, o_ref, *, tq, ck):
    i = pl.program_id(2)
    heads = [slice(LANE * hh, LANE * (hh + 1)) for hh in range(2)]
    qs = [q_ref[:, sl] for sl in heads]

    def body(c, states):
        st = pl.multiple_of(c * ck, ck)
        v = v_ref[pl.ds(st, ck), :]
        return tuple(_softmax_step(_dot_nt(q, k_ref[pl.ds(st, ck), sl]), v, state)
                     for q, sl, state in zip(qs, heads, states))

    init = (jnp.full((tq, 1), NEG, F32), jnp.zeros((tq, 1), F32), jnp.zeros((tq, LANE), F32))
    states = lax.fori_loop(0, i * (tq // ck), body, (init, init))
    st = pl.multiple_of(i * tq, tq)
    causal = lax.broadcasted_iota(jnp.int32, (tq, tq), 0) >= lax.broadcasted_iota(jnp.int32, (tq, tq), 1)
    outs = []
    for q, sl, state in zip(qs, heads, states):
        s = _dot_nt(q, k_ref[pl.ds(st, tq), sl])
        _, l, acc = _softmax_step(jnp.where(causal, s, NEG), v_ref[pl.ds(st, tq), :], state)
        outs.append(acc / l)
    low = lax.broadcasted_iota(jnp.int32, (tq, LANE), 1) < MLA_V
    o_ref[...] = jnp.where(low, outs[0], outs[1]).astype(o_ref.dtype)


def _mla_attention(qm, km, vm, tq=1024, ck=1024):
    bsz, seq, _ = qm.shape
    tq = min(tq, seq)
    return pl.pallas_call(
        functools.partial(_mla_attn_kernel, tq=tq, ck=min(ck, tq)), grid=(bsz, MLA_HEADS // 2, seq // tq),
        in_specs=[pl.BlockSpec((None, tq, 2 * LANE), lambda b, h, i: (b, i, h)),
                  pl.BlockSpec((None, seq, 2 * LANE), lambda b, h, i: (b, 0, h)),
                  pl.BlockSpec((None, seq, LANE), lambda b, h, i: (b, 0, h))],
        out_specs=pl.BlockSpec((None, tq, LANE), lambda b, h, i: (b, i, h)),
        out_shape=jax.ShapeDtypeStruct((bsz, seq, MLA_HEADS * MLA_V), BF16),
        compiler_params=_params("parallel", "parallel", "arbitrary"), name="mla_attention",
    )(qm, km, vm)


def _even_out_kernel(x_ref, on_ref, om_ref, wn_ref, wm_ref, g_ref, wg_ref, wu_ref, wd_ref, o_ref, *, chunk):
    x1 = x_ref[...] + _dot(on_ref[...], wn_ref[...]) + _dot(om_ref[...], wm_ref[...])
    n = _rms(x1, g_ref[...], x1.shape[-1]).astype(BF16)
    ffn = None
    for f0 in range(0, wg_ref.shape[1], chunk):
        gate = _dot(n, wg_ref[:, f0:f0 + chunk])
        act = (gate * jax.nn.sigmoid(gate) * _dot(n, wu_ref[:, f0:f0 + chunk])).astype(BF16)
        part = _dot(act, wd_ref[f0:f0 + chunk, :])
        ffn = part if ffn is None else ffn + part
    o_ref[...] = x1 + ffn


def _even_out_ffn(x, o_nsa, o_mla, w_out, gain, w_gate, w_up, w_down, tm=ROW_TILE):
    t, d = x.shape
    dff = w_gate.shape[1]
    wn, wm = w_out[:o_nsa.shape[1]].astype(BF16), w_out[o_nsa.shape[1]:].astype(BF16)
    row = lambda width: pl.BlockSpec((tm, width), lambda i: (i, 0))
    return pl.pallas_call(
        functools.partial(_even_out_kernel, chunk=dff // 2), grid=(t // tm,),
        in_specs=[row(d), row(o_nsa.shape[1]), row(o_mla.shape[1]), _const_spec(wn.shape), _const_spec(wm.shape),
                  _const_spec((1, d)), _const_spec((d, dff)), _const_spec((d, dff)), _const_spec((dff, d))],
        out_specs=row(d), out_shape=jax.ShapeDtypeStruct((t, d), F32),
        compiler_params=_params("parallel"), name="even_out_ffn",
    )(x, o_nsa, o_mla, wn, wm, gain[None, :], w_gate.astype(BF16), w_up.astype(BF16), w_down.astype(BF16))


def _conv_kernel(x_ref, g_ref, win_ref, cw_ref, wout_ref, o_ref, vbuf_ref, *, tm):
    j = pl.program_id(1)
    x = x_ref[...]
    d = x.shape[-1]
    n = _rms(x, g_ref[...], d).astype(BF16)
    b_gate = _dot(n, win_ref[:, 0:d])
    v = _dot(n, win_ref[:, d:2 * d]) * _dot(n, win_ref[:, 2 * d:3 * d])

    @pl.when(j == 0)
    def _():
        vbuf_ref[0:8, :] = jnp.zeros((8, d), F32)

    vbuf_ref[8:8 + tm, :] = v
    cw = cw_ref[...]
    y = cw[2:3] * v + cw[1:2] * vbuf_ref[7:7 + tm, :] + cw[0:1] * vbuf_ref[6:6 + tm, :]
    vbuf_ref[0:8, :] = v[tm - 8:tm]
    o_ref[...] = x + _dot((b_gate * y).astype(BF16), wout_ref[...])


def _conv_mixer(x, gain, w_in, conv_w, w_out, tm=ROW_TILE):
    bsz, seq, d = x.shape
    cw = jnp.pad(conv_w.astype(F32), ((0, 8 - CONV_WIDTH), (0, 0)))
    tok = pl.BlockSpec((None, tm, d), lambda b, j: (b, j, 0))
    return pl.pallas_call(
        functools.partial(_conv_kernel, tm=tm), grid=(bsz, seq // tm),
        in_specs=[tok, _const_spec((1, d)), _const_spec((d, 3 * d)), _const_spec((8, d)), _const_spec((d, d))],
        out_specs=tok, out_shape=jax.ShapeDtypeStruct((bsz, seq, d), F32),
        scratch_shapes=[pltpu.VMEM((tm + 8, d), F32)],
        compiler_params=_params("parallel", "arbitrary"), name="conv_mixer",
    )(x, gain[None, :], w_in.astype(BF16), cw, w_out.astype(BF16))


def _router_kernel(x_ref, g_ref, wr_ref, br_ref, tri_ref, route_ref, cnt_ref, carry_ref, *, tm):
    t = pl.program_id(0)

    @pl.when(t == 0)
    def _():
        carry_ref[...] = jnp.zeros_like(carry_ref)

    n = _rms(x_ref[...], g_ref[...], x_ref.shape[-1])
    hi = n.astype(BF16)
    lo = (n - hi.astype(F32)).astype(BF16)
    whi, wlo = wr_ref[0], wr_ref[1]
    logits = _dot(hi, whi) + _dot(lo, whi) + _dot(hi, wlo) + br_ref[...]
    lane = lax.broadcasted_iota(jnp.int32, (tm, LANE), 1).astype(F32)
    big = float(LANE)
    m1 = jnp.max(logits, axis=-1, keepdims=True)
    e1 = jnp.min(jnp.where(logits == m1, lane, big), axis=-1, keepdims=True)
    rest = jnp.where(lane == e1, NEG, logits)
    m2 = jnp.max(rest, axis=-1, keepdims=True)
    e2 = jnp.min(jnp.where(rest == m2, lane, big), axis=-1, keepdims=True)
    z = jnp.exp(m2 - m1)
    w1 = 1.0 / (1.0 + z)
    w2 = z / (1.0 + z)
    oh1 = jnp.where(lane == e1, 1.0, 0.0)
    oh2 = jnp.where(lane == e2, 1.0, 0.0)
    both = oh1 + oh2
    before = _dot(tri_ref[...], both.astype(BF16)) + carry_ref[0:1, :]
    r1 = jnp.sum(oh1 * before, axis=-1, keepdims=True)
    r2 = jnp.sum(oh2 * before, axis=-1, keepdims=True)
    cols = [e1, e2, w1, w2, r1, r2]
    out = jnp.zeros((tm, LANE), F32)
    for c, val in enumerate(cols):
        out = jnp.where(lane == c, val, out)
    route_ref[...] = out
    carry_ref[0:1, :] = carry_ref[0:1, :] + jnp.sum(both, axis=0, keepdims=True)
    cnt_ref[...] = carry_ref[...]


def _moe_router(x, gain, w_router, b_router, tm=ROW_TILE):
    t, d = x.shape
    wr = jnp.pad(w_router.astype(F32), ((0, 0), (0, LANE - N_EXPERTS)))
    whi = wr.astype(BF16)
    wlo = (wr - whi.astype(F32)).astype(BF16)
    br = jnp.concatenate([b_router.astype(F32), jnp.full((LANE - N_EXPERTS,), NEG, F32)])[None, :]
    tri = jnp.asarray(np.tril(np.ones((tm, tm), np.float32), -1), BF16)
    route, cnt = pl.pallas_call(
        functools.partial(_router_kernel, tm=tm), grid=(t // tm,),
        in_specs=[pl.BlockSpec((tm, d), lambda i: (i, 0)), _const_spec((1, d)), _const_spec((2, d, LANE)),
                  _const_spec((1, LANE)), _const_spec((tm, tm))],
        out_specs=[pl.BlockSpec((tm, LANE), lambda i: (i, 0)), pl.BlockSpec((8, LANE), lambda i: (0, 0))],
        out_shape=[jax.ShapeDtypeStruct((t, LANE), F32), jax.ShapeDtypeStruct((8, LANE), F32)],
        scratch_shapes=[pltpu.VMEM((8, LANE), F32)],
        compiler_params=_params("arbitrary"), name="moe_router",
    )(x, gain[None, :], jnp.stack([whi, wlo]), br, tri)
    return route, cnt[0, :N_EXPERTS].astype(jnp.int32)


def _row_copy(src, i, dst, j, sem):
    return pltpu.make_async_copy(src.at[pl.ds(i, 1)], dst.at[pl.ds(j, 1)], sem)


def _scatter_kernel(fill_ref, dest_ref, x_ref, g_ref, xs_ref, xn_ref, zero_ref, sems, *, tm, rb):
    @pl.when(pl.program_id(0) == 0)
    def _():
        sem = sems.at[0]
        zero_ref[...] = jnp.zeros_like(zero_ref)
        sizes = [rb >> s for s in range(rb.bit_length() - 3)]
        for e in range(N_EXPERTS + 1):
            lo, n = fill_ref[e], fill_ref[N_EXPERTS + 1 + e] - fill_ref[e]
            whole = n // rb

            def copy(off, size):
                return pltpu.make_async_copy(zero_ref.at[pl.ds(0, size)],
                                             xs_ref.at[pl.ds(pl.multiple_of(off, 8), size)], sem)

            def blocks(k, c, lo=lo, copy=copy):
                copy(lo + k * rb, rb).start()
                copy(lo + k * rb, rb).wait()
                return c

            lax.fori_loop(0, whole, blocks, 0)
            off = lo + whole * rb
            for size in sizes[1:]:
                @pl.when((n & size) != 0)
                def _(off=off, size=size, copy=copy):
                    copy(off, size).start()
                    copy(off, size).wait()
                off = off + (n & size)

    t = pl.program_id(0)
    slot = t % 2
    rows, row_sem = xn_ref.at[slot], sems.at[slot]
    rows[...] = _rms(x_ref[...], g_ref[...], x_ref.shape[-1])

    def start(r, c):
        _row_copy(rows, r, xs_ref, dest_ref[0, 0, r], row_sem).start()
        _row_copy(rows, r, xs_ref, dest_ref[0, 0, tm + r], row_sem).start()
        return c

    lax.fori_loop(0, tm, start, 0)

    def drain(s):
        for _ in range(TOP_K):
            pltpu.make_async_copy(xn_ref.at[s], xs_ref.at[pl.ds(0, tm)], sems.at[s]).wait()

    @pl.when(t > 0)
    def _():
        drain(1 - slot)

    @pl.when(t == pl.num_programs(0) - 1)
    def _():
        drain(slot)


def _moe_scatter(x, gain, dest, fill, n_rows, tm, rb):
    t, d = x.shape
    nt = t // tm
    grid_spec = pltpu.PrefetchScalarGridSpec(
        num_scalar_prefetch=1, grid=(nt,),
        in_specs=[pl.BlockSpec((1, 1, 2 * tm), lambda i, f: (i, 0, 0), memory_space=pltpu.SMEM),
                  pl.BlockSpec((tm, d), lambda i, f: (i, 0)),
                  pl.BlockSpec((1, d), lambda i, f: (0, 0))],
        out_specs=pl.BlockSpec(memory_space=pl.ANY),
        scratch_shapes=[pltpu.VMEM((2, tm, d), F32), pltpu.VMEM((rb, d), F32), pltpu.SemaphoreType.DMA((2,))])
    return pl.pallas_call(
        functools.partial(_scatter_kernel, tm=tm, rb=rb), grid_spec=grid_spec,
        out_shape=jax.ShapeDtypeStruct((n_rows, d), F32),
        compiler_params=_params("arbitrary"), name="moe_scatter",
    )(fill, dest, x, gain[None, :])


def _expert_kernel(be_ref, nu_ref, x_ref, wg_ref, wu_ref, wd_ref, o_ref):
    used = pl.program_id(0) < nu_ref[0]

    @pl.when(used)
    def _():
        xb = x_ref[...].astype(BF16)
        gate = _dot(xb, wg_ref[...])
        act = (gate * jax.nn.sigmoid(gate) * _dot(xb, wu_ref[...])).astype(BF16)
        o_ref[...] = _dot(act, wd_ref[...])

    @pl.when(jnp.logical_not(used))
    def _():
        o_ref[...] = jnp.zeros_like(o_ref)


def _moe_experts(xs, blk_expert, n_used, w_gate, w_up, w_down, rb):
    n_rows, d = xs.shape
    dff = w_gate.shape[-1]
    rowblk = pl.BlockSpec((rb, d), lambda i, be, nu: (i, 0))
    wspec = lambda shape: pl.BlockSpec((None,) + shape, lambda i, be, nu: (be[i], 0, 0))
    grid_spec = pltpu.PrefetchScalarGridSpec(
        num_scalar_prefetch=2, grid=(n_rows // rb,),
        in_specs=[rowblk, wspec((d, dff)), wspec((d, dff)), wspec((dff, d))], out_specs=rowblk)
    return pl.pallas_call(
        _expert_kernel, grid_spec=grid_spec, out_shape=jax.ShapeDtypeStruct(xs.shape, F32),
        compiler_params=_params("arbitrary"), name="moe_experts",
    )(blk_expert, n_used, xs, w_gate.astype(BF16), w_up.astype(BF16), w_down.astype(BF16))


def _combine_kernel(dest_ref, next_ref, x_ref, route_ref, ys_ref, o_ref, got_ref, sems, *, tm):
    t = pl.program_id(0)
    slot = t % 2

    def gather(idx_ref, s):
        def start(r, c):
            for k in range(TOP_K):
                _row_copy(ys_ref, idx_ref[0, 0, k * tm + r], got_ref.at[s, k], r, sems.at[s]).start()
            return c

        lax.fori_loop(0, tm, start, 0)

    @pl.when(t == 0)
    def _():
        gather(dest_ref, 0)

    @pl.when(t + 1 < pl.num_programs(0))
    def _():
        gather(next_ref, 1 - slot)

    for k in range(TOP_K):
        pltpu.make_async_copy(ys_ref.at[pl.ds(0, tm)], got_ref.at[slot, k], sems.at[slot]).wait()
    route = route_ref[...]
    o_ref[...] = x_ref[...] + route[:, 2:3] * got_ref[slot, 0] + route[:, 3:4] * got_ref[slot, 1]


def _moe_combine(x, route, dest, ys, tm):
    t, d = x.shape
    nt = t // tm
    idx = lambda nxt: pl.BlockSpec((1, 1, TOP_K * tm), lambda i: (jnp.minimum(i + nxt, nt - 1), 0, 0),
                                   memory_space=pltpu.SMEM)
    return pl.pallas_call(
        functools.partial(_combine_kernel, tm=tm), grid=(nt,),
        in_specs=[idx(0), idx(1), pl.BlockSpec((tm, d), lambda i: (i, 0)), pl.BlockSpec((tm, LANE), lambda i: (i, 0)),
                  pl.BlockSpec(memory_space=pl.ANY)],
        out_specs=pl.BlockSpec((tm, d), lambda i: (i, 0)),
        out_shape=jax.ShapeDtypeStruct((t, d), F32),
        scratch_shapes=[pltpu.VMEM((2, TOP_K, tm, d), F32), pltpu.SemaphoreType.DMA((2,))],
        compiler_params=_params("arbitrary"), name="moe_combine",
    )(dest, dest, x, route, ys)


def _moe(x, gain, w_router, b_router, w_gate, w_up, w_down, rb=MOE_ROW_BLOCK, tm=MOE_DMA_TILE):
    t, d = x.shape
    route, counts = _moe_router(x, gain, w_router, b_router)
    n_blk = (t * TOP_K + rb - 1) // rb + N_EXPERTS
    padded = (counts + rb - 1) // rb * rb
    pad_end = jnp.cumsum(padded)
    pad_start = pad_end - padded
    dest = pad_start[route[:, 0:2].astype(jnp.int32)] + route[:, 4:6].astype(jnp.int32)
    dest_t = jnp.concatenate([dest[:, 0].reshape(t // tm, 1, tm), dest[:, 1].reshape(t // tm, 1, tm)], axis=2)
    fill = jnp.concatenate([(pad_start + counts) // 8 * 8, pad_end[-1:], pad_end, jnp.full((1,), n_blk * rb, jnp.int32)])
    n_used = pad_end[-1:] // rb
    blk = jnp.minimum(jnp.arange(n_blk, dtype=jnp.int32), n_used - 1) * rb
    blk_expert = jnp.sum(pad_end[None, :] <= blk[:, None], axis=1).astype(jnp.int32)
    xs = _moe_scatter(x, gain, dest_t, fill.astype(jnp.int32), n_blk * rb, tm, rb)
    ys = _moe_experts(xs, blk_expert, n_used.astype(jnp.int32), w_gate, w_up, w_down, rb)
    return _moe_combine(x, route, dest_t, ys, tm)


def kernel(x, rel_bias, ev_mix_norm, ev_w_in, nsa_q_norm, nsa_k_norm, nsa_cmp_pos, nsa_cmp_w1, nsa_cmp_w2, mla_cq_norm, mla_ckv_norm, mla_w_uq, mla_w_ukv, mla_q_norm, mla_k_norm, ev_w_out, ev_ffn_norm, ffn_w_gate, ffn_w_up, ffn_w_down, od_mix_norm, od_w_in, conv_w, od_w_out, od_ffn_norm, moe_w_router, moe_b_router, moe_w_gate, moe_w_up, moe_w_down):
    bsz, seq, d = x.shape
    depth = ev_mix_norm.shape[0] + od_mix_norm.shape[0]
    pad = KV_PAD
    for layer in range(depth):
        i = layer // 2
        if layer % 2 == 0:
            qn, kv, kc_raw, vc_raw, gates, qm, km, vm = _even_in_proj(
                x, ev_mix_norm[i], ev_w_in[i], nsa_q_norm[i], nsa_k_norm[i], mla_cq_norm[i], mla_ckv_norm[i],
                mla_w_uq[i], mla_w_ukv[i], mla_q_norm[i], mla_k_norm[i], tm=pad)
            kc = _compress(kc_raw, nsa_cmp_pos[i, 0], nsa_cmp_w1[i, 0], nsa_cmp_w2[i, 0], nsa_k_norm[i, 0], True)
            vc = _compress(vc_raw, nsa_cmp_pos[i, 1], nsa_cmp_w1[i, 1], nsa_cmp_w2[i, 1], nsa_k_norm[i, 0], False)
            o_nsa = _nsa_attention(qn, kv, kc, vc, gates, rel_bias, pad)
            o_mla = _mla_attention(qm, km, vm)
            x = _even_out_ffn(x.reshape(bsz * seq, d), o_nsa.reshape(bsz * seq, -1), o_mla.reshape(bsz * seq, -1),
                              ev_w_out[i], ev_ffn_norm[i], ffn_w_gate[i], ffn_w_up[i], ffn_w_down[i]).reshape(bsz, seq, d)
        else:
            x = _conv_mixer(x, od_mix_norm[i], od_w_in[i], conv_w[i], od_w_out[i])
            x = _moe(x.reshape(bsz * seq, d), od_ffn_norm[i], moe_w_router[i], moe_b_router[i],
                     moe_w_gate[i], moe_w_up[i], moe_w_down[i]).reshape(bsz, seq, d)
    return x
```

```python
import functools
import math

import jax
import jax.numpy as jnp
import numpy as np
from jax import lax
from jax.experimental import pallas as pl
from jax.experimental.pallas import tpu as pltpu

F32 = jnp.float32
BF16 = jnp.bfloat16

EPS = 1e-6
NEG = -1e30
FORCED_SCORE = 1e9
NSA_HEADS = 8
NSA_GROUPS = 2
NSA_REP = NSA_HEADS // NSA_GROUPS
NSA_DK = 64
CMP_BLOCK = 32
CMP_STRIDE = 16
CMP_HIDDEN = 256
SEL_BLOCK = 64
SEL_TOP_N = 16
WINDOW = 512
MLA_HEADS = 8
MLA_Q_RANK = 256
MLA_KV_RANK = 128
MLA_NOPE = 64
MLA_ROPE = 32
MLA_V = 64
ROPE_THETA = 10000.0
REL_BUCKETS = 32
REL_MAX_DIST = 128
CONV_WIDTH = 3
N_EXPERTS = 8
TOP_K = 2
EVEN_IN_SIZES = (512,) + (128,) * 6 + (24, 256, 128, 32)

LANE = 128
VMEM_LIMIT = 56 * 1024 * 1024
SEL_MASK = -30000.0
FAR_DIST = 128

ROW_TILE = 512
KV_PAD = ROW_TILE
MOE_ROW_BLOCK = 512
MOE_DMA_TILE = 512
DMA_ISSUE_UNROLL = 8

_Q0, _KV0, _KC0, _VC0, _GT0, _CQ0, _CKV0, _KR0, _EVEN_W = 0, 1024, 2048, 2176, 2304, 2560, 2816, 2944, 3072


def _dot(a, b):
    return jnp.dot(a, b, preferred_element_type=F32)


def _dot_nt(a, b):
    return lax.dot_general(a, b, (((1,), (1,)), ((), ())), preferred_element_type=F32)


def _rms(x, gain, n):
    ss = jnp.sum(x * x, axis=-1, keepdims=True) * (1.0 / n)
    return x * lax.rsqrt(ss + EPS) * gain


def _params(*sem):
    return pltpu.CompilerParams(dimension_semantics=sem, vmem_limit_bytes=VMEM_LIMIT)


def _const_spec(shape):
    nd = len(shape)
    return pl.BlockSpec(shape, lambda *_: (0,) * nd, pipeline_mode=pl.Buffered(1))


def _even_in_kernel(x_ref, gmix_ref, w_ref, gq_ref, gks_ref, gkw_ref, gcq_ref, gckv_ref,
                    wuq_ref, wuk_ref, wuv_ref, gmq_ref, gmk_ref, cos_ref, sa_ref, sb_ref,
                    qn_ref, kv_ref, kc_ref, vc_ref, gate_ref, qm_ref, km_ref, vm_ref, *, tm):
    j = pl.program_id(1)

    @pl.when(j == 0)
    def _():
        kv_ref[...] = jnp.zeros_like(kv_ref)
        qn_ref[...] = jnp.zeros_like(qn_ref)
        kc_ref[...] = jnp.zeros_like(kc_ref)
        vc_ref[...] = jnp.zeros_like(vc_ref)
        gate_ref[...] = jnp.zeros_like(gate_ref)
        qm_ref[...] = jnp.zeros_like(qm_ref)
        km_ref[...] = jnp.zeros_like(km_ref)
        vm_ref[...] = jnp.zeros_like(vm_ref)

    @pl.when(j > 0)
    def _():
        xn = _rms(x_ref[...], gmix_ref[...], x_ref.shape[-1]).astype(BF16)
        hq = _dot(xn, w_ref[:, _Q0:_Q0 + 1024])
        hkv = _dot(xn, w_ref[:, _KV0:_KV0 + 1024])
        hc = _dot(xn, w_ref[:, _KC0:_KC0 + 256])
        hg = _dot(xn, w_ref[:, _GT0:_GT0 + 256])
        hm = _dot(xn, w_ref[:, _CQ0:_EVEN_W])
        cq = _rms(hm[:, :MLA_Q_RANK], gcq_ref[...], MLA_Q_RANK).astype(BF16)
        ckv = _rms(hm[:, MLA_Q_RANK:MLA_Q_RANK + MLA_KV_RANK], gckv_ref[...], MLA_KV_RANK).astype(BF16)
        k_rope = hm[:, MLA_Q_RANK + MLA_KV_RANK:]
        qm = _dot(cq, wuq_ref[...])
        kn = _dot(ckv, wuk_ref[...])
        vm_ref[...] = _dot(ckv, wuv_ref[...]).astype(BF16)
        dqk = MLA_NOPE + MLA_ROPE
        nsa_heads = [slice(LANE * h, LANE * (h + 1)) for h in range(NSA_HEADS)]
        mla_heads = [slice(LANE * h, LANE * (h + 1)) for h in range(MLA_HEADS)]
        segs = [(hq[:, sl], gq_ref[...], NSA_DK, NSA_DK ** -0.5) for sl in nsa_heads]
        for g in range(NSA_GROUPS):
            o = 4 * LANE * g
            segs.append((hkv[:, o:o + LANE], gks_ref[...], NSA_DK, 1.0))
            segs.append((hkv[:, o + 2 * LANE:o + 3 * LANE], gkw_ref[...], NSA_DK, 1.0))
        n_nsa = len(segs)
        segs += [(qm[:, sl], gmq_ref[...], dqk, dqk ** -0.5) for sl in mla_heads]
        segs += [(kn[:, sl] + k_rope, gmk_ref[...], dqk, 1.0) for sl in mla_heads]
        sums = [jnp.sum(x * x, axis=-1, keepdims=True) for x, _, _, _ in segs]
        scales = [lax.rsqrt(ss * (1.0 / n) + EPS) for ss, (_, _, n, _) in zip(sums, segs)]
        normed = [x * sc * gain for sc, (x, gain, _, _) in zip(scales, segs)]
        for h, sl in enumerate(nsa_heads):
            qn_ref[:, sl] = (normed[h] * segs[h][3]).astype(BF16)
        pos = (j - 1) * tm + lax.broadcasted_iota(jnp.int32, (tm, LANE), 0)
        lane = lax.broadcasted_iota(jnp.int32, (tm, LANE), 1)
        onehot = jnp.where(lane - NSA_DK == pos // SEL_BLOCK, 1.0, 0.0)
        ones = jnp.where(lane >= NSA_DK, 1.0, 0.0)
        for g in range(NSA_GROUPS):
            o = 4 * LANE * g
            kv_ref[:, o:o + LANE] = (normed[NSA_HEADS + 2 * g] + onehot).astype(BF16)
            kv_ref[:, o + LANE:o + 2 * LANE] = (hkv[:, o + LANE:o + 2 * LANE] + ones).astype(BF16)
            kv_ref[:, o + 2 * LANE:o + 3 * LANE] = normed[NSA_HEADS + 2 * g + 1].astype(BF16)
            kv_ref[:, o + 3 * LANE:o + 4 * LANE] = (hkv[:, o + 3 * LANE:o + 4 * LANE] + ones).astype(BF16)
        kc_ref[...] = hc[:, :LANE].astype(BF16)
        vc_ref[...] = hc[:, LANE:].astype(BF16)
        gate_ref[...] = jax.nn.sigmoid(hg)
        cos, sa, sb = cos_ref[...], sa_ref[...], sb_ref[...]
        mla = normed[n_nsa:]
        fwd = [pltpu.roll(x, 16, 1) for x in mla]
        bwd = [pltpu.roll(x, LANE - 16, 1) for x in mla]
        roped = [x * cos + f * sa + b * sb for x, f, b in zip(mla, fwd, bwd)]
        for h, sl in enumerate(mla_heads):
            qm_ref[:, sl] = (roped[h] * segs[n_nsa + h][3]).astype(BF16)
            km_ref[:, sl] = roped[MLA_HEADS + h].astype(BF16)


def _even_in_weights(w_in):
    offs = np.concatenate([[0], np.cumsum(EVEN_IN_SIZES)])
    part = [w_in[:, offs[n]:offs[n + 1]] for n in range(len(EVEN_IN_SIZES))]
    q, k_c, v_c, k_s, v_s, k_w, v_w, gates, c_q, c_kv, k_rope = part
    d = w_in.shape[0]
    z = lambda n: jnp.zeros((d, n), w_in.dtype)
    cols = []
    for h in range(NSA_HEADS):
        cols += [q[:, 64 * h:64 * h + 64], z(64)]
    for g in range(NSA_GROUPS):
        s = slice(64 * g, 64 * g + 64)
        cols += [k_s[:, s], z(64), v_s[:, s], z(64), k_w[:, s], z(64), v_w[:, s], z(64)]
    cols += [k_c, v_c]
    for g in range(NSA_GROUPS):
        cols += [gates[:, 12 * g:12 * g + 12], z(LANE - 12)]
    cols += [c_q, c_kv, z(64), k_rope, z(32)]
    w = jnp.concatenate(cols, axis=1)
    assert w.shape[1] == _EVEN_W
    return w.astype(BF16)


def _pad_gain(g, width):
    return jnp.pad(g.astype(F32), (0, width - g.shape[0]))[None, :]


def _rope_tables(seq):
    half = MLA_ROPE // 2
    inv_freq = ROPE_THETA ** (-jnp.arange(half, dtype=F32) / half)
    ang = jnp.arange(seq).astype(F32)[:, None] * inv_freq[None, :]
    cos, sin = jnp.cos(ang), jnp.sin(ang)
    one = jnp.ones((seq, MLA_NOPE), F32)
    zn = jnp.zeros((seq, MLA_NOPE), F32)
    zt = jnp.zeros((seq, LANE - MLA_NOPE - MLA_ROPE), F32)
    zh = jnp.zeros((seq, half), F32)
    cos_t = jnp.concatenate([one, cos, cos, zt + 1.0], axis=1)
    sa = jnp.concatenate([zn, zh, sin, zt], axis=1)
    sb = jnp.concatenate([zn, -sin, zh, zt], axis=1)
    return cos_t, sa, sb


def _even_in_proj(x, gmix, w_in, q_norm, k_norm, cq_norm, ckv_norm, w_uq, w_ukv, mq_norm, mk_norm, tm=ROW_TILE):
    bsz, seq, d = x.shape
    nt = seq // tm
    w = _even_in_weights(w_in)
    dqk = MLA_NOPE + MLA_ROPE
    wuq = jnp.concatenate(
        [jnp.pad(w_uq[:, dqk * h:dqk * (h + 1)], ((0, 0), (0, LANE - dqk))) for h in range(MLA_HEADS)],
        axis=1).astype(BF16)
    kvw = MLA_NOPE + MLA_V
    wuk = jnp.concatenate(
        [jnp.pad(w_ukv[:, kvw * h:kvw * h + MLA_NOPE], ((0, 0), (0, LANE - MLA_NOPE))) for h in range(MLA_HEADS)],
        axis=1).astype(BF16)
    wuv = jnp.concatenate([w_ukv[:, kvw * h + MLA_NOPE:kvw * (h + 1)] for h in range(MLA_HEADS)], axis=1).astype(BF16)
    cos_t, sa, sb = _rope_tables(seq)
    tok = lambda width: pl.BlockSpec((None, tm, width), lambda b, j: (b, jnp.maximum(j - 1, 0), 0))
    postab = pl.BlockSpec((tm, LANE), lambda b, j: (jnp.maximum(j - 1, 0), 0))
    in_specs = [tok(d), _const_spec((1, d)), _const_spec((d, _EVEN_W)),
                _const_spec((1, LANE)), _const_spec((1, LANE)), _const_spec((1, LANE)),
                _const_spec((1, MLA_Q_RANK)), _const_spec((1, MLA_KV_RANK)),
                _const_spec(wuq.shape), _const_spec(wuk.shape), _const_spec(wuv.shape),
                _const_spec((1, LANE)), _const_spec((1, LANE)), postab, postab, postab]
    out_shape = [jax.ShapeDtypeStruct((bsz, seq, 1024), BF16),
                 jax.ShapeDtypeStruct((bsz, seq + tm, 1024), BF16),
                 jax.ShapeDtypeStruct((bsz, seq, LANE), BF16),
                 jax.ShapeDtypeStruct((bsz, seq, LANE), BF16),
                 jax.ShapeDtypeStruct((bsz, seq, 2 * LANE), F32),
                 jax.ShapeDtypeStruct((bsz, seq, 1024), BF16),
                 jax.ShapeDtypeStruct((bsz, seq, 1024), BF16),
                 jax.ShapeDtypeStruct((bsz, seq, 512), BF16)]
    out_specs = [tok(1024), pl.BlockSpec((None, tm, 1024), lambda b, j: (b, j, 0)), tok(LANE), tok(LANE),
                 tok(2 * LANE), tok(1024), tok(1024), tok(512)]
    return pl.pallas_call(
        functools.partial(_even_in_kernel, tm=tm), grid=(bsz, nt + 1), in_specs=in_specs, out_specs=out_specs,
        out_shape=out_shape, compiler_params=_params("parallel", "arbitrary"), name="even_in_proj",
    )(x, gmix[None, :], w, _pad_gain(q_norm, LANE), _pad_gain(k_norm[1], LANE), _pad_gain(k_norm[2], LANE),
      cq_norm[None, :], ckv_norm[None, :], wuq, wuk, wuv, _pad_gain(mq_norm, LANE), _pad_gain(mk_norm, LANE),
      cos_t, sa, sb)


def _cmp_kernel(x_ref, wa_ref, wb_ref, pa_ref, pb_ref, w2_ref, gain_ref, o_ref, *, normalize):
    x = x_ref[...]
    ua = _dot(x, wa_ref[...])
    ub = _dot(x, wb_ref[...])
    pt = _dot(pa_ref[...], wa_ref[...]) + _dot(pb_ref[...], wb_ref[...])
    n = ub.shape[0]
    pre = ua + pltpu.roll(ub, n - 1, 0) + pt[0:1]
    hid = jax.nn.gelu(pre).astype(BF16)
    out = _dot(hid, w2_ref[...])
    if normalize:
        gain = gain_ref[...]
        for g in range(NSA_GROUPS):
            sl = slice(LANE * g, LANE * (g + 1))
            o_ref[:, sl] = _rms(out[:, sl], gain, NSA_DK).astype(BF16)
    else:
        o_ref[...] = out.astype(BF16)


def _compress(kv, pos_emb, w1, w2, gain, normalize):
    bsz, seq, _ = kv.shape
    half = CMP_BLOCK // 2
    assert CMP_STRIDE == half
    nrow = seq // CMP_STRIDE
    x = kv.reshape(bsz, nrow, CMP_STRIDE * LANE)
    w1r = w1.reshape(CMP_BLOCK, NSA_DK, CMP_HIDDEN)
    eye = jnp.eye(NSA_GROUPS, dtype=w1.dtype)
    widen = lambda w: jnp.einsum("ldn,gh->lgdhn", w, eye).reshape(half * LANE, NSA_GROUPS * CMP_HIDDEN).astype(BF16)
    wa, wb = widen(w1r[:half]), widen(w1r[half:])
    prow = lambda p: jnp.pad(jnp.broadcast_to(p[:, None, :], (half, NSA_GROUPS, NSA_DK)).reshape(1, half * LANE),
                             ((0, 15), (0, 0))).astype(BF16)
    pa, pb = prow(pos_emb[:half]), prow(pos_emb[half:])
    second = jnp.zeros_like(w2) if normalize else w2
    w2w = jnp.einsum("nd,gh->gnhd", jnp.concatenate([w2, second], axis=1), eye)
    w2w = w2w.reshape(NSA_GROUPS * CMP_HIDDEN, NSA_GROUPS * LANE).astype(BF16)
    return pl.pallas_call(
        functools.partial(_cmp_kernel, normalize=normalize), grid=(bsz,),
        in_specs=[pl.BlockSpec((None, nrow, CMP_STRIDE * LANE), lambda b: (b, 0, 0)),
                  _const_spec(wa.shape), _const_spec(wb.shape), _const_spec(pa.shape), _const_spec(pb.shape),
                  _const_spec(w2w.shape), _const_spec((1, LANE))],
        out_specs=pl.BlockSpec((None, nrow, NSA_GROUPS * LANE), lambda b: (b, 0, 0)),
        out_shape=jax.ShapeDtypeStruct((bsz, nrow, NSA_GROUPS * LANE), BF16),
        compiler_params=_params("parallel"), name="nsa_compress",
    )(x, wa, wb, pa, pb, w2w, _pad_gain(gain, LANE))


def _bucket_table():
    dist = np.arange(FAR_DIST + 1)
    max_exact = REL_BUCKETS // 2
    nf = np.maximum(dist, max_exact).astype(np.float32)
    large = max_exact + (np.log(nf / max_exact) / math.log(REL_MAX_DIST / max_exact)
                         * (REL_BUCKETS - max_exact)).astype(np.int32)
    return np.where(dist < max_exact, dist, np.minimum(large, REL_BUCKETS - 1))


CMP_NEAR = 32


def _expand(tbl, idx):
    idx = np.asarray(idx)
    onehot = jnp.asarray(np.eye(tbl.shape[1], dtype=np.float32)[idx.reshape(-1)])
    out = lax.dot_general(tbl, onehot, (((1,), (1,)), ((), ())), precision=lax.Precision.HIGHEST)
    return out.reshape((tbl.shape[0],) + idx.shape)


def _toeplitz_tile(tbl, far, width, dist0, valid):
    period = width + SEL_BLOCK + 1
    j = np.arange(period)
    dist = dist0 - np.where(j < width, j, j - period)
    gen = jnp.where(jnp.asarray(valid(dist))[None], _expand(tbl, np.clip(dist, 0, FAR_DIST)) - far, NEG)
    flat = jnp.tile(gen, (1, SEL_BLOCK))[:, :SEL_BLOCK * (period - 1)]
    return flat.reshape(tbl.shape[0], SEL_BLOCK, period - 1)[:, :, :width]


def _bias_tables(rel_bias):
    tbl = _expand(rel_bias.astype(F32).T, _bucket_table())
    far = tbl[:, FAR_DIST:]
    bias_near = _toeplitz_tile(tbl, far, 4 * SEL_BLOCK, 3 * SEL_BLOCK, lambda d: d >= 0)
    bias_win = _toeplitz_tile(tbl, 0.0, WINDOW + SEL_BLOCK, WINDOW, lambda d: (d >= 0) & (d < WINDOW))
    d_c = (np.arange(SEL_BLOCK)[:, None] + (CMP_NEAR // 2) * CMP_STRIDE - (CMP_BLOCK - 1)
           - CMP_STRIDE * np.arange(CMP_NEAR)[None, :])
    assert d_c[:, 0].min() >= FAR_DIST and d_c[:, -1].max() < 0
    bias_cmp = jnp.where(jnp.asarray(d_c >= 0)[None], _expand(tbl, np.clip(d_c, 0, FAR_DIST)) - far[:, :, None], 0.0)
    return bias_cmp, bias_near, bias_win


def _cmp_mask_dist(seq):
    a = np.tile(np.arange(SEL_BLOCK), NSA_REP)[:, None]
    c = np.arange(seq // CMP_STRIDE)[None, :]
    return jnp.asarray(c * CMP_STRIDE + CMP_BLOCK - 1 - a, jnp.int32)


def _overlap_t(seq):
    ncp = seq // CMP_STRIDE
    cs = np.arange(ncp)[None, :] * CMP_STRIDE
    ss = np.arange(SEL_BLOCK)[:, None] * SEL_BLOCK
    ov = (cs < ss + SEL_BLOCK) & (cs + CMP_BLOCK - 1 >= ss) & (np.arange(ncp)[None, :] < (seq - CMP_BLOCK) // CMP_STRIDE + 1)
    return jnp.asarray(ov, BF16)


def _softmax_step(s, v, state):
    m, l, acc = state
    mn = jnp.maximum(m, jnp.max(s, axis=-1, keepdims=True))
    alpha = jnp.exp(m - mn)
    p = jnp.exp(s - mn)
    return mn, alpha * l + jnp.sum(p, axis=-1, keepdims=True), alpha * acc + _dot(p.astype(BF16), v)


FAR_CHUNK = 1024


def _lane_fold(x, op):
    out = x[:, 0:LANE]
    for c in range(1, x.shape[1] // LANE):
        out = op(out, x[:, LANE * c:LANE * (c + 1)])
    return out


SEL_TILE = 8


def _nsa_local_kernel(q_ref, kv_ref, kc_ref, vc_ref, g_ref, bc_ref, bw_ref, ovt_ref, dmask_ref, neg_ref, oc_ref, *, pad):
    t = pl.program_id(1)
    qb, rep, ng = SEL_BLOCK, NSA_REP, NSA_GROUPS
    rows = qb * rep
    gw = 4 * LANE
    ncp = kc_ref.shape[0]
    wk = WINDOW + qb
    m_i = lax.broadcasted_iota(jnp.int32, (CMP_NEAR, ncp), 0)
    c_i = lax.broadcasted_iota(jnp.int32, (CMP_NEAR, ncp), 1)
    w_lane = lax.broadcasted_iota(jnp.int32, (rows, wk), 1)
    dist = dmask_ref[...]
    ovt = ovt_ref[...]
    gates = g_ref[...]
    low = lax.broadcasted_iota(jnp.int32, (qb, LANE), 1) < NSA_DK
    jj = lax.broadcasted_iota(jnp.int32, (qb, ng * qb), 0)
    bases = []
    for g in range(ng):
        base = bc_ref[rep * g:rep * (g + 1)].reshape(rows, CMP_NEAR)
        b_hi = base.astype(BF16)
        bases.append((b_hi, (base - b_hi.astype(F32)).astype(BF16)))
    blocks = [(u, t * SEL_TILE + u, slice(qb * u, qb * (u + 1))) for u in range(SEL_TILE)]
    chains = [(u, i, tok, g) for u, i, tok in blocks for g in range(ng)]
    qq = [jnp.concatenate([q_ref[tok, gw * g + LANE * r:gw * g + LANE * (r + 1)] for r in range(rep)], axis=0)
          for _, _, tok, g in chains]
    shift = [jnp.where(c_i - m_i == (qb // CMP_STRIDE) * i - CMP_NEAR // 2, 1.0, 0.0).astype(BF16) for _, i, _ in blocks]
    s_c = [jnp.where(dist <= i * qb,
                     _dot_nt(qq[n], kc_ref[:, LANE * g:LANE * (g + 1)]) + _dot(bases[g][0], shift[u])
                     + _dot(bases[g][1], shift[u]), NEG) for n, (u, i, _, g) in enumerate(chains)]
    s_w = [jnp.where(w_lane + i * qb - WINDOW >= 0,
                     _dot_nt(qq[n], kv_ref[pl.ds(pl.multiple_of(pad - WINDOW + i * qb, qb), wk),
                                           gw * g + 2 * LANE:gw * g + 3 * LANE])
                     + bw_ref[rep * g:rep * (g + 1)].reshape(rows, wk), NEG) for n, (u, i, _, g) in enumerate(chains)]
    e_c = [jnp.exp(s - jnp.maximum(jnp.max(s, axis=-1, keepdims=True), -1e20)) for s in s_c]
    e_w = [jnp.exp(s - jnp.max(s, axis=-1, keepdims=True)) for s in s_w]
    inv = [1.0 / jnp.maximum(jnp.sum(e, axis=-1, keepdims=True), 1e-30) for e in e_c]
    o_c = [_dot(e_c[n].astype(BF16), vc_ref[:, LANE * g:LANE * (g + 1)]) * inv[n] for n, (_, _, _, g) in enumerate(chains)]
    pv_w = [_dot(e_w[n].astype(BF16), kv_ref[pl.ds(pl.multiple_of(pad - WINDOW + i * qb, qb), wk),
                                            gw * g + 3 * LANE:gw * g + 4 * LANE]) for n, (_, i, _, g) in enumerate(chains)]
    o_w = [pv / pv[:, NSA_DK:NSA_DK + 1] for pv in pv_w]
    for n, (_, _, tok, g) in enumerate(chains):
        part = []
        for r in range(rep):
            sl = slice(qb * r, qb * (r + 1))
            c0 = LANE * g + 3 * r
            part.append(gates[tok, c0:c0 + 1] * o_c[n][sl] + gates[tok, c0 + 2:c0 + 3] * o_w[n][sl])
        for pr in range(rep // 2):
            c0 = 2 * LANE * g + LANE * pr
            oc_ref[tok, c0:c0 + LANE] = jnp.where(low, part[2 * pr], pltpu.roll(part[2 * pr + 1], NSA_DK, 1))
    imps = []
    for u, i, _ in blocks:
        p = [e_c[ng * u + g] * inv[ng * u + g] for g in range(ng)]
        psum = jnp.concatenate([x[0:qb] + x[qb:2 * qb] + x[2 * qb:3 * qb] + x[3 * qb:4 * qb] for x in p], axis=0)
        hi = psum.astype(BF16)
        r1 = psum - hi.astype(F32)
        mid = r1.astype(BF16)
        lo = (r1 - mid.astype(F32)).astype(BF16)
        imp = _dot_nt(ovt, hi) + _dot_nt(ovt, mid) + _dot_nt(ovt, lo)
        imp = jnp.where((jj == i) | (jj == 0), FORCED_SCORE, imp)
        imps.append(jnp.where(jj > i, -1.0, imp))
    grp = [[imp[8 * a:8 * a + 8] for a in range(8)] for imp in imps]
    cnt = [[jnp.zeros((8, ng * qb), F32) for _ in range(8)] for _ in imps]
    sub = lax.broadcasted_iota(jnp.int32, (8, ng * qb), 0)
    for k in range(qb):
        for u in range(SEL_TILE):
            rk = imps[u][k:k + 1, :]
            for a in range(8):
                if 8 * a + 7 <= k:
                    cnt[u][a] = cnt[u][a] + jnp.where(rk > grp[u][a], 1.0, 0.0)
                elif 8 * a > k:
                    cnt[u][a] = cnt[u][a] + jnp.where(rk >= grp[u][a], 1.0, 0.0)
                else:
                    cnt[u][a] = cnt[u][a] + jnp.where(sub + 8 * a > k, jnp.where(rk >= grp[u][a], 1.0, 0.0),
                                                      jnp.where(rk > grp[u][a], 1.0, 0.0))
    for u in range(SEL_TILE):
        neg_ref[u] = jnp.where(jnp.concatenate(cnt[u], axis=0) < SEL_TOP_N, 0.0, SEL_MASK)


ATT_TILE = 2


def _nsa_attend_kernel(q_ref, kv_ref, g_ref, neg_ref, oc_ref, bn_ref, o_ref, s_ref, *, pad):
    t = pl.program_id(1)
    qb, rep, ng = SEL_BLOCK, NSA_REP, NSA_GROUPS
    rows = qb * rep
    gw = 4 * LANE
    blocks = [(u, t * ATT_TILE + u, slice(qb * u, qb * (u + 1))) for u in range(ATT_TILE)]
    chains = [(u, i, tok, g) for u, i, tok in blocks for g in range(ng)]
    nc = len(chains)
    qs = [jnp.concatenate([q_ref[tok, gw * g + LANE * r:gw * g + LANE * (r + 1)] for r in range(rep)], axis=0)
          for _, _, tok, g in chains]
    jj = lax.broadcasted_iota(jnp.int32, (qb, ng * qb), 0)
    q_lane = lax.broadcasted_iota(jnp.int32, (qb, ng * qb), 1)
    zero = jnp.zeros((qb, ng * qb), BF16)
    ext_near = [jnp.concatenate([zero, neg_ref[u].astype(BF16)], axis=0) for u, _, _ in blocks]
    ext_far = [jnp.concatenate([zero, jnp.where(jj >= i - 3, SEL_MASK, neg_ref[u]).astype(BF16)], axis=0)
               for u, i, _ in blocks]
    pick = [jnp.where(q_lane == jj + qb * g, 1.0, 0.0).astype(BF16) for g in range(ng)]
    q_near = [qs[n] + jnp.concatenate([_dot_nt(pick[g], ext_near[u]).astype(BF16)] * rep, axis=0)
              for n, (u, _, _, g) in enumerate(chains)]
    q_far = [qs[n] + jnp.concatenate([_dot_nt(pick[g], ext_far[u]).astype(BF16)] * rep, axis=0)
             for n, (u, _, _, g) in enumerate(chains)]

    n_lane = lax.broadcasted_iota(jnp.int32, (rows, 4 * qb), 1)
    near_at = [pl.multiple_of(pad + (i - 3) * qb, qb) for _, i, _ in blocks]
    s_near = [jnp.where(n_lane + (i - 3) * qb >= 0,
                        _dot_nt(q_near[n], kv_ref[pl.ds(near_at[u], 4 * qb), gw * g:gw * g + LANE])
                        + bn_ref[rep * g:rep * (g + 1)].reshape(rows, 4 * qb), NEG) for n, (u, i, _, g) in enumerate(chains)]

    nch = (jnp.maximum(blocks[-1][1] - 3, 0) * qb + FAR_CHUNK - 1) // FAR_CHUNK

    def pass1(c, mx):
        st = pl.multiple_of(pad + c * FAR_CHUNK, math.gcd(pad, FAR_CHUNK))
        col = pl.multiple_of(c * FAR_CHUNK, FAR_CHUNK)
        out = []
        for n, (_, _, _, g) in enumerate(chains):
            s = _dot_nt(q_far[n], kv_ref[pl.ds(st, FAR_CHUNK), gw * g:gw * g + LANE])
            s_ref[n, :, pl.ds(col, FAR_CHUNK)] = s
            out.append(jnp.maximum(mx[n], _lane_fold(s, jnp.maximum)))
        return tuple(out)

    mx = lax.fori_loop(0, nch, pass1, tuple(jnp.full((rows, LANE), NEG, F32) for _ in range(nc)))
    ms = [jnp.max(jnp.maximum(mx[n], _lane_fold(s_near[n], jnp.maximum)), axis=-1, keepdims=True) for n in range(nc)]

    def pass2(c, state):
        st = pl.multiple_of(pad + c * FAR_CHUNK, math.gcd(pad, FAR_CHUNK))
        col = pl.multiple_of(c * FAR_CHUNK, FAR_CHUNK)
        out = []
        for n, (_, _, _, g) in enumerate(chains):
            p = jnp.exp(s_ref[n, :, pl.ds(col, FAR_CHUNK)] - ms[n])
            out.append(state[n] + _dot(p.astype(BF16), kv_ref[pl.ds(st, FAR_CHUNK), gw * g + LANE:gw * g + 2 * LANE]))
        return tuple(out)

    far = lax.fori_loop(0, nch, pass2, tuple(jnp.zeros((rows, LANE), F32) for _ in range(nc)))
    p_near = [jnp.exp(s_near[n] - ms[n]).astype(BF16) for n in range(nc)]
    pv = [far[n] + _dot(p_near[n], kv_ref[pl.ds(near_at[u], 4 * qb), gw * g + LANE:gw * g + 2 * LANE])
          for n, (u, _, _, g) in enumerate(chains)]
    o_s = [x / x[:, NSA_DK:NSA_DK + 1] for x in pv]

    gates = g_ref[...]
    low = lax.broadcasted_iota(jnp.int32, (qb, LANE), 1) < NSA_DK
    for n, (_, _, tok, g) in enumerate(chains):
        outs = [gates[tok, LANE * g + 3 * r + 1:LANE * g + 3 * r + 2] * o_s[n][qb * r:qb * (r + 1)] for r in range(rep)]
        for pr in range(rep // 2):
            c0 = 2 * LANE * g + LANE * pr
            pair = jnp.where(low, outs[2 * pr], pltpu.roll(outs[2 * pr + 1], NSA_DK, 1))
            o_ref[tok, c0:c0 + LANE] = (oc_ref[tok, c0:c0 + LANE] + pair).astype(o_ref.dtype)


def _nsa_attention(qn, kv, kc, vc, gates, rel_bias, pad):
    bsz, seq, _ = qn.shape
    qb, ng = SEL_BLOCK, NSA_GROUPS
    nq = seq // qb
    ncp = seq // CMP_STRIDE
    assert nq <= qb and nq % SEL_TILE == 0 and nq % ATT_TILE == 0 and pad >= WINDOW and seq % FAR_CHUNK == 0
    bias_cmp, bias_near, bias_win = _bias_tables(rel_bias)
    ovt = _overlap_t(seq)
    dmask = _cmp_mask_dist(seq)
    width = NSA_HEADS * NSA_DK
    tile = lambda n, w: pl.BlockSpec((None, n, w), lambda b, i: (b, i, 0))
    per_b = lambda n, w: pl.BlockSpec((None, n, w), lambda b, i: (b, 0, 0))
    neg, o_local = pl.pallas_call(
        functools.partial(_nsa_local_kernel, pad=pad), grid=(bsz, nq // SEL_TILE),
        in_specs=[tile(SEL_TILE * qb, ng * 4 * LANE), per_b(seq + pad, ng * 4 * LANE), per_b(ncp, ng * LANE),
                  per_b(ncp, ng * LANE), tile(SEL_TILE * qb, ng * LANE), _const_spec(bias_cmp.shape),
                  _const_spec(bias_win.shape), _const_spec(ovt.shape), _const_spec(dmask.shape)],
        out_specs=[pl.BlockSpec((None, SEL_TILE, qb, ng * qb), lambda b, i: (b, i, 0, 0)), tile(SEL_TILE * qb, width)],
        out_shape=[jax.ShapeDtypeStruct((bsz, nq, qb, ng * qb), F32), jax.ShapeDtypeStruct((bsz, seq, width), F32)],
        compiler_params=_params("parallel", "arbitrary"), name="nsa_local",
    )(qn, kv, kc, vc, gates, bias_cmp, bias_win, ovt, dmask)
    return pl.pallas_call(
        functools.partial(_nsa_attend_kernel, pad=pad), grid=(bsz, nq // ATT_TILE),
        in_specs=[tile(ATT_TILE * qb, ng * 4 * LANE), per_b(seq + pad, ng * 4 * LANE), tile(ATT_TILE * qb, ng * LANE),
                  pl.BlockSpec((None, ATT_TILE, qb, ng * qb), lambda b, i: (b, i, 0, 0)), tile(ATT_TILE * qb, width),
                  _const_spec(bias_near.shape)],
        out_specs=tile(ATT_TILE * qb, width),
        out_shape=jax.ShapeDtypeStruct((bsz, seq, width), BF16),
        scratch_shapes=[pltpu.VMEM((ATT_TILE * ng, NSA_REP * qb, seq), F32)],
        compiler_params=_params("parallel", "arbitrary"), name="nsa_attention",
    )(qn, kv, gates, neg, o_local, bias_near)


def _mla_attn_kernel(q_ref, k_ref, v_ref, o_ref, *, tq, ck):
    i = pl.program_id(2)
    heads = [slice(LANE * hh, LANE * (hh + 1)) for hh in range(2)]
    qs = [q_ref[:, sl] for sl in heads]

    def body(c, states):
        st = pl.multiple_of(c * ck, ck)
        v = v_ref[pl.ds(st, ck), :]
        return tuple(_softmax_step(_dot_nt(q, k_ref[pl.ds(st, ck), sl]), v, state)
                     for q, sl, state in zip(qs, heads, states))

    init = (jnp.full((tq, 1), NEG, F32), jnp.zeros((tq, 1), F32), jnp.zeros((tq, LANE), F32))
    states = lax.fori_loop(0, i * (tq // ck), body, (init, init))
    st = pl.multiple_of(i * tq, tq)
    causal = lax.broadcasted_iota(jnp.int32, (tq, tq), 0) >= lax.broadcasted_iota(jnp.int32, (tq, tq), 1)
    outs = []
    for q, sl, state in zip(qs, heads, states):
        s = _dot_nt(q, k_ref[pl.ds(st, tq), sl])
        _, l, acc = _softmax_step(jnp.where(causal, s, NEG), v_ref[pl.ds(st, tq), :], state)
        outs.append(acc / l)
    low = lax.broadcasted_iota(jnp.int32, (tq, LANE), 1) < MLA_V
    o_ref[...] = jnp.where(low, outs[0], outs[1]).astype(o_ref.dtype)


def _mla_attention(qm, km, vm, tq=1024, ck=1024):
    bsz, seq, _ = qm.shape
    tq = min(tq, seq)
    return pl.pallas_call(
        functools.partial(_mla_attn_kernel, tq=tq, ck=min(ck, tq)), grid=(bsz, MLA_HEADS // 2, seq // tq),
        in_specs=[pl.BlockSpec((None, tq, 2 * LANE), lambda b, h, i: (b, i, h)),
                  pl.BlockSpec((None, seq, 2 * LANE), lambda b, h, i: (b, 0, h)),
                  pl.BlockSpec((None, seq, LANE), lambda b, h, i: (b, 0, h))],
        out_specs=pl.BlockSpec((None, tq, LANE), lambda b, h, i: (b, i, h)),
        out_shape=jax.ShapeDtypeStruct((bsz, seq, MLA_HEADS * MLA_V), BF16),
        compiler_params=_params("parallel", "parallel", "arbitrary"), name="mla_attention",
    )(qm, km, vm)


def _even_out_kernel(x_ref, on_ref, om_ref, wn_ref, wm_ref, g_ref, wg_ref, wu_ref, wd_ref, o_ref, *, chunk):
    x1 = x_ref[...] + _dot(on_ref[...], wn_ref[...]) + _dot(om_ref[...], wm_ref[...])
    n = _rms(x1, g_ref[...], x1.shape[-1]).astype(BF16)
    ffn = None
    for f0 in range(0, wg_ref.shape[1], chunk):
        gate = _dot(n, wg_ref[:, f0:f0 + chunk])
        act = (gate * jax.nn.sigmoid(gate) * _dot(n, wu_ref[:, f0:f0 + chunk])).astype(BF16)
        part = _dot(act, wd_ref[f0:f0 + chunk, :])
        ffn = part if ffn is None else ffn + part
    o_ref[...] = x1 + ffn


def _even_out_ffn(x, o_nsa, o_mla, w_out, gain, w_gate, w_up, w_down, tm=ROW_TILE):
    t, d = x.shape
    dff = w_gate.shape[1]
    wn, wm = w_out[:o_nsa.shape[1]].astype(BF16), w_out[o_nsa.shape[1]:].astype(BF16)
    row = lambda width: pl.BlockSpec((tm, width), lambda i: (i, 0))
    return pl.pallas_call(
        functools.partial(_even_out_kernel, chunk=dff // 2), grid=(t // tm,),
        in_specs=[row(d), row(o_nsa.shape[1]), row(o_mla.shape[1]), _const_spec(wn.shape), _const_spec(wm.shape),
                  _const_spec((1, d)), _const_spec((d, dff)), _const_spec((d, dff)), _const_spec((dff, d))],
        out_specs=row(d), out_shape=jax.ShapeDtypeStruct((t, d), F32),
        compiler_params=_params("parallel"), name="even_out_ffn",
    )(x, o_nsa, o_mla, wn, wm, gain[None, :], w_gate.astype(BF16), w_up.astype(BF16), w_down.astype(BF16))


def _conv_kernel(x_ref, g_ref, win_ref, cw_ref, wout_ref, o_ref, vbuf_ref, *, tm):
    j = pl.program_id(1)
    x = x_ref[...]
    d = x.shape[-1]
    n = _rms(x, g_ref[...], d).astype(BF16)
    b_gate = _dot(n, win_ref[:, 0:d])
    v = _dot(n, win_ref[:, d:2 * d]) * _dot(n, win_ref[:, 2 * d:3 * d])

    @pl.when(j == 0)
    def _():
        vbuf_ref[0:8, :] = jnp.zeros((8, d), F32)

    vbuf_ref[8:8 + tm, :] = v
    cw = cw_ref[...]
    y = cw[2:3] * v + cw[1:2] * vbuf_ref[7:7 + tm, :] + cw[0:1] * vbuf_ref[6:6 + tm, :]
    vbuf_ref[0:8, :] = v[tm - 8:tm]
    o_ref[...] = x + _dot((b_gate * y).astype(BF16), wout_ref[...])


def _conv_mixer(x, gain, w_in, conv_w, w_out, tm=ROW_TILE):
    bsz, seq, d = x.shape
    cw = jnp.pad(conv_w.astype(F32), ((0, 8 - CONV_WIDTH), (0, 0)))
    tok = pl.BlockSpec((None, tm, d), lambda b, j: (b, j, 0))
    return pl.pallas_call(
        functools.partial(_conv_kernel, tm=tm), grid=(bsz, seq // tm),
        in_specs=[tok, _const_spec((1, d)), _const_spec((d, 3 * d)), _const_spec((8, d)), _const_spec((d, d))],
        out_specs=tok, out_shape=jax.ShapeDtypeStruct((bsz, seq, d), F32),
        scratch_shapes=[pltpu.VMEM((tm + 8, d), F32)],
        compiler_params=_params("parallel", "arbitrary"), name="conv_mixer",
    )(x, gain[None, :], w_in.astype(BF16), cw, w_out.astype(BF16))


def _router_kernel(x_ref, g_ref, wr_ref, br_ref, tri_ref, route_ref, cnt_ref, carry_ref, *, tm):
    t = pl.program_id(0)

    @pl.when(t == 0)
    def _():
        carry_ref[...] = jnp.zeros_like(carry_ref)

    n = _rms(x_ref[...], g_ref[...], x_ref.shape[-1])
    hi = n.astype(BF16)
    lo = (n - hi.astype(F32)).astype(BF16)
    whi, wlo = wr_ref[0], wr_ref[1]
    logits = _dot(hi, whi) + _dot(lo, whi) + _dot(hi, wlo) + br_ref[...]
    lane = lax.broadcasted_iota(jnp.int32, (tm, LANE), 1).astype(F32)
    big = float(LANE)
    m1 = jnp.max(logits, axis=-1, keepdims=True)
    e1 = jnp.min(jnp.where(logits == m1, lane, big), axis=-1, keepdims=True)
    rest = jnp.where(lane == e1, NEG, logits)
    m2 = jnp.max(rest, axis=-1, keepdims=True)
    e2 = jnp.min(jnp.where(rest == m2, lane, big), axis=-1, keepdims=True)
    z = jnp.exp(m2 - m1)
    w1 = 1.0 / (1.0 + z)
    w2 = z / (1.0 + z)
    oh1 = jnp.where(lane == e1, 1.0, 0.0)
    oh2 = jnp.where(lane == e2, 1.0, 0.0)
    both = oh1 + oh2
    before = _dot(tri_ref[...], both.astype(BF16)) + carry_ref[0:1, :]
    r1 = jnp.sum(oh1 * before, axis=-1, keepdims=True)
    r2 = jnp.sum(oh2 * before, axis=-1, keepdims=True)
    cols = [e1, e2, w1, w2, r1, r2]
    out = jnp.zeros((tm, LANE), F32)
    for c, val in enumerate(cols):
        out = jnp.where(lane == c, val, out)
    route_ref[...] = out
    carry_ref[0:1, :] = carry_ref[0:1, :] + jnp.sum(both, axis=0, keepdims=True)
    cnt_ref[...] = carry_ref[...]


def _moe_router(x, gain, w_router, b_router, tm=ROW_TILE):
    t, d = x.shape
    wr = jnp.pad(w_router.astype(F32), ((0, 0), (0, LANE - N_EXPERTS)))
    whi = wr.astype(BF16)
    wlo = (wr - whi.astype(F32)).astype(BF16)
    br = jnp.concatenate([b_router.astype(F32), jnp.full((LANE - N_EXPERTS,), NEG, F32)])[None, :]
    tri = jnp.asarray(np.tril(np.ones((tm, tm), np.float32), -1), BF16)
    route, cnt = pl.pallas_call(
        functools.partial(_router_kernel, tm=tm), grid=(t // tm,),
        in_specs=[pl.BlockSpec((tm, d), lambda i: (i, 0)), _const_spec((1, d)), _const_spec((2, d, LANE)),
                  _const_spec((1, LANE)), _const_spec((tm, tm))],
        out_specs=[pl.BlockSpec((tm, LANE), lambda i: (i, 0)), pl.BlockSpec((8, LANE), lambda i: (0, 0))],
        out_shape=[jax.ShapeDtypeStruct((t, LANE), F32), jax.ShapeDtypeStruct((8, LANE), F32)],
        scratch_shapes=[pltpu.VMEM((8, LANE), F32)],
        compiler_params=_params("arbitrary"), name="moe_router",
    )(x, gain[None, :], jnp.stack([whi, wlo]), br, tri)
    return route, cnt[0, :N_EXPERTS].astype(jnp.int32)


def _row_copy(src, i, dst, j, sem):
    return pltpu.make_async_copy(src.at[pl.ds(i, 1)], dst.at[pl.ds(j, 1)], sem)


def _scatter_kernel(fill_ref, dest_ref, x_ref, g_ref, xs_ref, xn_ref, zero_ref, sems, *, tm, rb):
    @pl.when(pl.program_id(0) == 0)
    def _():
        sem = sems.at[0]
        zero_ref[...] = jnp.zeros_like(zero_ref)
        sizes = [rb >> s for s in range(rb.bit_length() - 3)]
        for e in range(N_EXPERTS + 1):
            lo, n = fill_ref[e], fill_ref[N_EXPERTS + 1 + e] - fill_ref[e]
            whole = n // rb

            def copy(off, size):
                return pltpu.make_async_copy(zero_ref.at[pl.ds(0, size)],
                                             xs_ref.at[pl.ds(pl.multiple_of(off, 8), size)], sem)

            def blocks(k, c, lo=lo, copy=copy):
                copy(lo + k * rb, rb).start()
                copy(lo + k * rb, rb).wait()
                return c

            lax.fori_loop(0, whole, blocks, 0)
            off = lo + whole * rb
            for size in sizes[1:]:
                @pl.when((n & size) != 0)
                def _(off=off, size=size, copy=copy):
                    copy(off, size).start()
                    copy(off, size).wait()
                off = off + (n & size)

    t = pl.program_id(0)
    slot = t % 2
    rows, row_sem = xn_ref.at[slot], sems.at[slot]
    rows[...] = _rms(x_ref[...], g_ref[...], x_ref.shape[-1])

    def start(r, c):
        _row_copy(rows, r, xs_ref, dest_ref[0, 0, r], row_sem).start()
        _row_copy(rows, r, xs_ref, dest_ref[0, 0, tm + r], row_sem).start()
        return c

    lax.fori_loop(0, tm, start, 0, unroll=DMA_ISSUE_UNROLL)

    def drain(s):
        for _ in range(TOP_K):
            pltpu.make_async_copy(xn_ref.at[s], xs_ref.at[pl.ds(0, tm)], sems.at[s]).wait()

    @pl.when(t > 0)
    def _():
        drain(1 - slot)

    @pl.when(t == pl.num_programs(0) - 1)
    def _():
        drain(slot)


def _moe_scatter(x, gain, dest, fill, n_rows, tm, rb):
    t, d = x.shape
    nt = t // tm
    grid_spec = pltpu.PrefetchScalarGridSpec(
        num_scalar_prefetch=1, grid=(nt,),
        in_specs=[pl.BlockSpec((1, 1, 2 * tm), lambda i, f: (i, 0, 0), memory_space=pltpu.SMEM),
                  pl.BlockSpec((tm, d), lambda i, f: (i, 0)),
                  pl.BlockSpec((1, d), lambda i, f: (0, 0))],
        out_specs=pl.BlockSpec(memory_space=pl.ANY),
        scratch_shapes=[pltpu.VMEM((2, tm, d), F32), pltpu.VMEM((rb, d), F32), pltpu.SemaphoreType.DMA((2,))])
    return pl.pallas_call(
        functools.partial(_scatter_kernel, tm=tm, rb=rb), grid_spec=grid_spec,
        out_shape=jax.ShapeDtypeStruct((n_rows, d), F32),
        compiler_params=_params("arbitrary"), name="moe_scatter",
    )(fill, dest, x, gain[None, :])


def _expert_kernel(be_ref, nu_ref, x_ref, wg_ref, wu_ref, wd_ref, o_ref):
    used = pl.program_id(0) < nu_ref[0]

    @pl.when(used)
    def _():
        xb = x_ref[...].astype(BF16)
        gate = _dot(xb, wg_ref[...])
        act = (gate * jax.nn.sigmoid(gate) * _dot(xb, wu_ref[...])).astype(BF16)
        o_ref[...] = _dot(act, wd_ref[...])

    @pl.when(jnp.logical_not(used))
    def _():
        o_ref[...] = jnp.zeros_like(o_ref)


def _moe_experts(xs, blk_expert, n_used, w_gate, w_up, w_down, rb):
    n_rows, d = xs.shape
    dff = w_gate.shape[-1]
    rowblk = pl.BlockSpec((rb, d), lambda i, be, nu: (i, 0))
    wspec = lambda shape: pl.BlockSpec((None,) + shape, lambda i, be, nu: (be[i], 0, 0))
    grid_spec = pltpu.PrefetchScalarGridSpec(
        num_scalar_prefetch=2, grid=(n_rows // rb,),
        in_specs=[rowblk, wspec((d, dff)), wspec((d, dff)), wspec((dff, d))], out_specs=rowblk)
    return pl.pallas_call(
        _expert_kernel, grid_spec=grid_spec, out_shape=jax.ShapeDtypeStruct(xs.shape, F32),
        compiler_params=_params("arbitrary"), name="moe_experts",
    )(blk_expert, n_used, xs, w_gate.astype(BF16), w_up.astype(BF16), w_down.astype(BF16))


def _combine_kernel(dest_ref, next_ref, x_ref, route_ref, ys_ref, o_ref, got_ref, sems, *, tm):
    t = pl.program_id(0)
    slot = t % 2

    def gather(idx_ref, s):
        def start(r, c):
            for k in range(TOP_K):
                _row_copy(ys_ref, idx_ref[0, 0, k * tm + r], got_ref.at[s, k], r, sems.at[s]).start()
            return c

        lax.fori_loop(0, tm, start, 0, unroll=DMA_ISSUE_UNROLL)

    @pl.when(t == 0)
    def _():
        gather(dest_ref, 0)

    @pl.when(t + 1 < pl.num_programs(0))
    def _():
        gather(next_ref, 1 - slot)

    for k in range(TOP_K):
        pltpu.make_async_copy(ys_ref.at[pl.ds(0, tm)], got_ref.at[slot, k], sems.at[slot]).wait()
    route = route_ref[...]
    o_ref[...] = x_ref[...] + route[:, 2:3] * got_ref[slot, 0] + route[:, 3:4] * got_ref[slot, 1]


def _moe_combine(x, route, dest, ys, tm):
    t, d = x.shape
    nt = t // tm
    idx = lambda nxt: pl.BlockSpec((1, 1, TOP_K * tm), lambda i: (jnp.minimum(i + nxt, nt - 1), 0, 0),
                                   memory_space=pltpu.SMEM)
    return pl.pallas_call(
        functools.partial(_combine_kernel, tm=tm), grid=(nt,),
        in_specs=[idx(0), idx(1), pl.BlockSpec((tm, d), lambda i: (i, 0)), pl.BlockSpec((tm, LANE), lambda i: (i, 0)),
                  pl.BlockSpec(memory_space=pl.ANY)],
        out_specs=pl.BlockSpec((tm, d), lambda i: (i, 0)),
        out_shape=jax.ShapeDtypeStruct((t, d), F32),
        scratch_shapes=[pltpu.VMEM((2, TOP_K, tm, d), F32), pltpu.SemaphoreType.DMA((2,))],
        compiler_params=_params("arbitrary"), name="moe_combine",
    )(dest, dest, x, route, ys)


def _moe(x, gain, w_router, b_router, w_gate, w_up, w_down, rb=MOE_ROW_BLOCK, tm=MOE_DMA_TILE):
    t, d = x.shape
    route, counts = _moe_router(x, gain, w_router, b_router)
    n_blk = (t * TOP_K + rb - 1) // rb + N_EXPERTS
    padded = (counts + rb - 1) // rb * rb
    pad_end = jnp.cumsum(padded)
    pad_start = pad_end - padded
    dest = pad_start[route[:, 0:2].astype(jnp.int32)] + route[:, 4:6].astype(jnp.int32)
    dest_t = jnp.concatenate([dest[:, 0].reshape(t // tm, 1, tm), dest[:, 1].reshape(t // tm, 1, tm)], axis=2)
    fill = jnp.concatenate([(pad_start + counts) // 8 * 8, pad_end[-1:], pad_end, jnp.full((1,), n_blk * rb, jnp.int32)])
    n_used = pad_end[-1:] // rb
    blk = jnp.minimum(jnp.arange(n_blk, dtype=jnp.int32), n_used - 1) * rb
    blk_expert = jnp.sum(pad_end[None, :] <= blk[:, None], axis=1).astype(jnp.int32)
    xs = _moe_scatter(x, gain, dest_t, fill.astype(jnp.int32), n_blk * rb, tm, rb)
    ys = _moe_experts(xs, blk_expert, n_used.astype(jnp.int32), w_gate, w_up, w_down, rb)
    return _moe_combine(x, route, dest_t, ys, tm)


def kernel(x, rel_bias, ev_mix_norm, ev_w_in, nsa_q_norm, nsa_k_norm, nsa_cmp_pos, nsa_cmp_w1, nsa_cmp_w2, mla_cq_norm, mla_ckv_norm, mla_w_uq, mla_w_ukv, mla_q_norm, mla_k_norm, ev_w_out, ev_ffn_norm, ffn_w_gate, ffn_w_up, ffn_w_down, od_mix_norm, od_w_in, conv_w, od_w_out, od_ffn_norm, moe_w_router, moe_b_router, moe_w_gate, moe_w_up, moe_w_down):
    bsz, seq, d = x.shape
    depth = ev_mix_norm.shape[0] + od_mix_norm.shape[0]
    pad = KV_PAD
    for layer in range(depth):
        i = layer // 2
        if layer % 2 == 0:
            qn, kv, kc_raw, vc_raw, gates, qm, km, vm = _even_in_proj(
                x, ev_mix_norm[i], ev_w_in[i], nsa_q_norm[i], nsa_k_norm[i], mla_cq_norm[i], mla_ckv_norm[i],
                mla_w_uq[i], mla_w_ukv[i], mla_q_norm[i], mla_k_norm[i], tm=pad)
            kc = _compress(kc_raw, nsa_cmp_pos[i, 0], nsa_cmp_w1[i, 0], nsa_cmp_w2[i, 0], nsa_k_norm[i, 0], True)
            vc = _compress(vc_raw, nsa_cmp_pos[i, 1], nsa_cmp_w1[i, 1], nsa_cmp_w2[i, 1], nsa_k_norm[i, 0], False)
            o_nsa = _nsa_attention(qn, kv, kc, vc, gates, rel_bias, pad)
            o_mla = _mla_attention(qm, km, vm)
            x = _even_out_ffn(x.reshape(bsz * seq, d), o_nsa.reshape(bsz * seq, -1), o_mla.reshape(bsz * seq, -1),
                              ev_w_out[i], ev_ffn_norm[i], ffn_w_gate[i], ffn_w_up[i], ffn_w_down[i]).reshape(bsz, seq, d)
        else:
            x = _conv_mixer(x, od_mix_norm[i], od_w_in[i], conv_w[i], od_w_out[i])
            x = _moe(x.reshape(bsz * seq, d), od_ffn_norm[i], moe_w_router[i], moe_b_router[i],
                     moe_w_gate[i], moe_w_up[i], moe_w_down[i]).reshape(bsz, seq, d)
    return x
```

```python
import functools
import math

import jax
import jax.numpy as jnp
import numpy as np
from jax import lax
from jax.experimental import pallas as pl
from jax.experimental.pallas import tpu as pltpu

F32 = jnp.float32
BF16 = jnp.bfloat16

EPS = 1e-6
NEG = -1e30
FORCED_SCORE = 1e9
NSA_HEADS = 8
NSA_GROUPS = 2
NSA_REP = NSA_HEADS // NSA_GROUPS
NSA_DK = 64
CMP_BLOCK = 32
CMP_STRIDE = 16
CMP_HIDDEN = 256
SEL_BLOCK = 64
SEL_TOP_N = 16
WINDOW = 512
MLA_HEADS = 8
MLA_Q_RANK = 256
MLA_KV_RANK = 128
MLA_NOPE = 64
MLA_ROPE = 32
MLA_V = 64
ROPE_THETA = 10000.0
REL_BUCKETS = 32
REL_MAX_DIST = 128
CONV_WIDTH = 3
N_EXPERTS = 8
TOP_K = 2
EVEN_IN_SIZES = (512,) + (128,) * 6 + (24, 256, 128, 32)

LANE = 128
VMEM_LIMIT = 56 * 1024 * 1024
SEL_MASK = NEG
FAR_DIST = 128

ROW_TILE = 512
KV_PAD = ROW_TILE
MOE_ROW_BLOCK = 512
MOE_DMA_TILE = 512
DMA_ISSUE_UNROLL = 8

_Q0, _KV0, _KC0, _VC0, _GT0, _CQ0, _CKV0, _KR0, _EVEN_W = 0, 1024, 2048, 2176, 2304, 2560, 2816, 2944, 3072


def _dot(a, b):
    return jnp.dot(a, b, preferred_element_type=F32)


def _dot_nt(a, b):
    return lax.dot_general(a, b, (((1,), (1,)), ((), ())), preferred_element_type=F32)


def _rms(x, gain, n):
    ss = jnp.sum(x * x, axis=-1, keepdims=True) * (1.0 / n)
    return x * lax.rsqrt(ss + EPS) * gain


def _params(*sem):
    return pltpu.CompilerParams(dimension_semantics=sem, vmem_limit_bytes=VMEM_LIMIT)


def _const_spec(shape):
    nd = len(shape)
    return pl.BlockSpec(shape, lambda *_: (0,) * nd, pipeline_mode=pl.Buffered(1))


def _even_in_kernel(x_ref, gmix_ref, w_ref, gq_ref, gks_ref, gkw_ref, gcq_ref, gckv_ref,
                    wuq_ref, wuk_ref, wuv_ref, gmq_ref, gmk_ref, cos_ref, sa_ref, sb_ref,
                    qn_ref, kv_ref, kc_ref, vc_ref, gate_ref, qm_ref, km_ref, vm_ref, *, tm):
    j = pl.program_id(1)

    @pl.when(j == 0)
    def _():
        kv_ref[...] = jnp.zeros_like(kv_ref)
        qn_ref[...] = jnp.zeros_like(qn_ref)
        kc_ref[...] = jnp.zeros_like(kc_ref)
        vc_ref[...] = jnp.zeros_like(vc_ref)
        gate_ref[...] = jnp.zeros_like(gate_ref)
        qm_ref[...] = jnp.zeros_like(qm_ref)
        km_ref[...] = jnp.zeros_like(km_ref)
        vm_ref[...] = jnp.zeros_like(vm_ref)

    @pl.when(j > 0)
    def _():
        xn = _rms(x_ref[...], gmix_ref[...], x_ref.shape[-1]).astype(BF16)
        hq = _dot(xn, w_ref[:, _Q0:_Q0 + 1024])
        hkv = _dot(xn, w_ref[:, _KV0:_KV0 + 1024])
        hc = _dot(xn, w_ref[:, _KC0:_KC0 + 256])
        hg = _dot(xn, w_ref[:, _GT0:_GT0 + 256])
        hm = _dot(xn, w_ref[:, _CQ0:_EVEN_W])
        cq = _rms(hm[:, :MLA_Q_RANK], gcq_ref[...], MLA_Q_RANK).astype(BF16)
        ckv = _rms(hm[:, MLA_Q_RANK:MLA_Q_RANK + MLA_KV_RANK], gckv_ref[...], MLA_KV_RANK).astype(BF16)
        k_rope = hm[:, MLA_Q_RANK + MLA_KV_RANK:]
        qm = _dot(cq, wuq_ref[...])
        kn = _dot(ckv, wuk_ref[...])
        vm_ref[...] = _dot(ckv, wuv_ref[...]).astype(BF16)
        dqk = MLA_NOPE + MLA_ROPE
        nsa_heads = [slice(LANE * h, LANE * (h + 1)) for h in range(NSA_HEADS)]
        mla_heads = [slice(LANE * h, LANE * (h + 1)) for h in range(MLA_HEADS)]
        segs = [(hq[:, sl], gq_ref[...], NSA_DK, NSA_DK ** -0.5) for sl in nsa_heads]
        for g in range(NSA_GROUPS):
            o = 4 * LANE * g
            segs.append((hkv[:, o:o + LANE], gks_ref[...], NSA_DK, 1.0))
            segs.append((hkv[:, o + 2 * LANE:o + 3 * LANE], gkw_ref[...], NSA_DK, 1.0))
        n_nsa = len(segs)
        segs += [(qm[:, sl], gmq_ref[...], dqk, dqk ** -0.5) for sl in mla_heads]
        segs += [(kn[:, sl] + k_rope, gmk_ref[...], dqk, 1.0) for sl in mla_heads]
        sums = [jnp.sum(x * x, axis=-1, keepdims=True) for x, _, _, _ in segs]
        scales = [lax.rsqrt(ss * (1.0 / n) + EPS) for ss, (_, _, n, _) in zip(sums, segs)]
        normed = [x * sc * gain for sc, (x, gain, _, _) in zip(scales, segs)]
        for h, sl in enumerate(nsa_heads):
            qn_ref[:, sl] = (normed[h] * segs[h][3]).astype(BF16)
        pos = (j - 1) * tm + lax.broadcasted_iota(jnp.int32, (tm, LANE), 0)
        lane = lax.broadcasted_iota(jnp.int32, (tm, LANE), 1)
        onehot = jnp.where(lane - NSA_DK == pos // SEL_BLOCK, 1.0, 0.0)
        ones = jnp.where(lane >= NSA_DK, 1.0, 0.0)
        for g in range(NSA_GROUPS):
            o = 4 * LANE * g
            kv_ref[:, o:o + LANE] = (normed[NSA_HEADS + 2 * g] + onehot).astype(BF16)
            kv_ref[:, o + LANE:o + 2 * LANE] = (hkv[:, o + LANE:o + 2 * LANE] + ones).astype(BF16)
            kv_ref[:, o + 2 * LANE:o + 3 * LANE] = normed[NSA_HEADS + 2 * g + 1].astype(BF16)
            kv_ref[:, o + 3 * LANE:o + 4 * LANE] = (hkv[:, o + 3 * LANE:o + 4 * LANE] + ones).astype(BF16)
        kc_ref[...] = hc[:, :LANE].astype(BF16)
        vc_ref[...] = hc[:, LANE:].astype(BF16)
        gate_ref[...] = jax.nn.sigmoid(hg)
        cos, sa, sb = cos_ref[...], sa_ref[...], sb_ref[...]
        mla = normed[n_nsa:]
        fwd = [pltpu.roll(x, 16, 1) for x in mla]
        bwd = [pltpu.roll(x, LANE - 16, 1) for x in mla]
        roped = [x * cos + f * sa + b * sb for x, f, b in zip(mla, fwd, bwd)]
        for h, sl in enumerate(mla_heads):
            qm_ref[:, sl] = (roped[h] * segs[n_nsa + h][3]).astype(BF16)
            km_ref[:, sl] = roped[MLA_HEADS + h].astype(BF16)


def _even_in_weights(w_in):
    offs = np.concatenate([[0], np.cumsum(EVEN_IN_SIZES)])
    part = [w_in[:, offs[n]:offs[n + 1]] for n in range(len(EVEN_IN_SIZES))]
    q, k_c, v_c, k_s, v_s, k_w, v_w, gates, c_q, c_kv, k_rope = part
    d = w_in.shape[0]
    z = lambda n: jnp.zeros((d, n), w_in.dtype)
    cols = []
    for h in range(NSA_HEADS):
        cols += [q[:, 64 * h:64 * h + 64], z(64)]
    for g in range(NSA_GROUPS):
        s = slice(64 * g, 64 * g + 64)
        cols += [k_s[:, s], z(64), v_s[:, s], z(64), k_w[:, s], z(64), v_w[:, s], z(64)]
    cols += [k_c, v_c]
    for g in range(NSA_GROUPS):
        cols += [gates[:, 12 * g:12 * g + 12], z(LANE - 12)]
    cols += [c_q, c_kv, z(64), k_rope, z(32)]
    w = jnp.concatenate(cols, axis=1)
    assert w.shape[1] == _EVEN_W
    return w.astype(BF16)


def _pad_gain(g, width):
    return jnp.pad(g.astype(F32), (0, width - g.shape[0]))[None, :]


def _rope_tables(seq):
    half = MLA_ROPE // 2
    inv_freq = ROPE_THETA ** (-jnp.arange(half, dtype=F32) / half)
    ang = jnp.arange(seq).astype(F32)[:, None] * inv_freq[None, :]
    cos, sin = jnp.cos(ang), jnp.sin(ang)
    one = jnp.ones((seq, MLA_NOPE), F32)
    zn = jnp.zeros((seq, MLA_NOPE), F32)
    zt = jnp.zeros((seq, LANE - MLA_NOPE - MLA_ROPE), F32)
    zh = jnp.zeros((seq, half), F32)
    cos_t = jnp.concatenate([one, cos, cos, zt + 1.0], axis=1)
    sa = jnp.concatenate([zn, zh, sin, zt], axis=1)
    sb = jnp.concatenate([zn, -sin, zh, zt], axis=1)
    return cos_t, sa, sb


def _even_in_proj(x, gmix, w_in, q_norm, k_norm, cq_norm, ckv_norm, w_uq, w_ukv, mq_norm, mk_norm, tm=ROW_TILE):
    bsz, seq, d = x.shape
    nt = seq // tm
    w = _even_in_weights(w_in)
    dqk = MLA_NOPE + MLA_ROPE
    wuq = jnp.concatenate(
        [jnp.pad(w_uq[:, dqk * h:dqk * (h + 1)], ((0, 0), (0, LANE - dqk))) for h in range(MLA_HEADS)],
        axis=1).astype(BF16)
    kvw = MLA_NOPE + MLA_V
    wuk = jnp.concatenate(
        [jnp.pad(w_ukv[:, kvw * h:kvw * h + MLA_NOPE], ((0, 0), (0, LANE - MLA_NOPE))) for h in range(MLA_HEADS)],
        axis=1).astype(BF16)
    wuv = jnp.concatenate([w_ukv[:, kvw * h + MLA_NOPE:kvw * (h + 1)] for h in range(MLA_HEADS)], axis=1).astype(BF16)
    cos_t, sa, sb = _rope_tables(seq)
    tok = lambda width: pl.BlockSpec((None, tm, width), lambda b, j: (b, jnp.maximum(j - 1, 0), 0))
    postab = pl.BlockSpec((tm, LANE), lambda b, j: (jnp.maximum(j - 1, 0), 0))
    in_specs = [tok(d), _const_spec((1, d)), _const_spec((d, _EVEN_W)),
                _const_spec((1, LANE)), _const_spec((1, LANE)), _const_spec((1, LANE)),
                _const_spec((1, MLA_Q_RANK)), _const_spec((1, MLA_KV_RANK)),
                _const_spec(wuq.shape), _const_spec(wuk.shape), _const_spec(wuv.shape),
                _const_spec((1, LANE)), _const_spec((1, LANE)), postab, postab, postab]
    out_shape = [jax.ShapeDtypeStruct((bsz, seq, 1024), BF16),
                 jax.ShapeDtypeStruct((bsz, seq + tm, 1024), BF16),
                 jax.ShapeDtypeStruct((bsz, seq, LANE), BF16),
                 jax.ShapeDtypeStruct((bsz, seq, LANE), BF16),
                 jax.ShapeDtypeStruct((bsz, seq, 2 * LANE), F32),
                 jax.ShapeDtypeStruct((bsz, seq, 1024), BF16),
                 jax.ShapeDtypeStruct((bsz, seq, 1024), BF16),
                 jax.ShapeDtypeStruct((bsz, seq, 512), BF16)]
    out_specs = [tok(1024), pl.BlockSpec((None, tm, 1024), lambda b, j: (b, j, 0)), tok(LANE), tok(LANE),
                 tok(2 * LANE), tok(1024), tok(1024), tok(512)]
    return pl.pallas_call(
        functools.partial(_even_in_kernel, tm=tm), grid=(bsz, nt + 1), in_specs=in_specs, out_specs=out_specs,
        out_shape=out_shape, compiler_params=_params("parallel", "arbitrary"), name="even_in_proj",
    )(x, gmix[None, :], w, _pad_gain(q_norm, LANE), _pad_gain(k_norm[1], LANE), _pad_gain(k_norm[2], LANE),
      cq_norm[None, :], ckv_norm[None, :], wuq, wuk, wuv, _pad_gain(mq_norm, LANE), _pad_gain(mk_norm, LANE),
      cos_t, sa, sb)


def _cmp_kernel(x_ref, wa_ref, wb_ref, pa_ref, pb_ref, w2_ref, gain_ref, o_ref, *, normalize):
    x = x_ref[...]
    ua = _dot(x, wa_ref[...])
    ub = _dot(x, wb_ref[...])
    pt = _dot(pa_ref[...], wa_ref[...]) + _dot(pb_ref[...], wb_ref[...])
    n = ub.shape[0]
    pre = ua + pltpu.roll(ub, n - 1, 0) + pt[0:1]
    hid = jax.nn.gelu(pre).astype(BF16)
    out = _dot(hid, w2_ref[...])
    if normalize:
        gain = gain_ref[...]
        for g in range(NSA_GROUPS):
            sl = slice(LANE * g, LANE * (g + 1))
            o_ref[:, sl] = _rms(out[:, sl], gain, NSA_DK).astype(BF16)
    else:
        o_ref[...] = out.astype(BF16)


def _compress(kv, pos_emb, w1, w2, gain, normalize):
    bsz, seq, _ = kv.shape
    half = CMP_BLOCK // 2
    assert CMP_STRIDE == half
    nrow = seq // CMP_STRIDE
    x = kv.reshape(bsz, nrow, CMP_STRIDE * LANE)
    w1r = w1.reshape(CMP_BLOCK, NSA_DK, CMP_HIDDEN)
    eye = jnp.eye(NSA_GROUPS, dtype=w1.dtype)
    widen = lambda w: jnp.einsum("ldn,gh->lgdhn", w, eye).reshape(half * LANE, NSA_GROUPS * CMP_HIDDEN).astype(BF16)
    wa, wb = widen(w1r[:half]), widen(w1r[half:])
    prow = lambda p: jnp.pad(jnp.broadcast_to(p[:, None, :], (half, NSA_GROUPS, NSA_DK)).reshape(1, half * LANE),
                             ((0, 15), (0, 0))).astype(BF16)
    pa, pb = prow(pos_emb[:half]), prow(pos_emb[half:])
    second = jnp.zeros_like(w2) if normalize else w2
    w2w = jnp.einsum("nd,gh->gnhd", jnp.concatenate([w2, second], axis=1), eye)
    w2w = w2w.reshape(NSA_GROUPS * CMP_HIDDEN, NSA_GROUPS * LANE).astype(BF16)
    return pl.pallas_call(
        functools.partial(_cmp_kernel, normalize=normalize), grid=(bsz,),
        in_specs=[pl.BlockSpec((None, nrow, CMP_STRIDE * LANE), lambda b: (b, 0, 0)),
                  _const_spec(wa.shape), _const_spec(wb.shape), _const_spec(pa.shape), _const_spec(pb.shape),
                  _const_spec(w2w.shape), _const_spec((1, LANE))],
        out_specs=pl.BlockSpec((None, nrow, NSA_GROUPS * LANE), lambda b: (b, 0, 0)),
        out_shape=jax.ShapeDtypeStruct((bsz, nrow, NSA_GROUPS * LANE), BF16),
        compiler_params=_params("parallel"), name="nsa_compress",
    )(x, wa, wb, pa, pb, w2w, _pad_gain(gain, LANE))


def _bucket_table():
    dist = np.arange(FAR_DIST + 1)
    max_exact = REL_BUCKETS // 2
    nf = np.maximum(dist, max_exact).astype(np.float32)
    large = max_exact + (np.log(nf / max_exact) / math.log(REL_MAX_DIST / max_exact)
                         * (REL_BUCKETS - max_exact)).astype(np.int32)
    return np.where(dist < max_exact, dist, np.minimum(large, REL_BUCKETS - 1))


CMP_NEAR = 32


def _expand(tbl, idx):
    idx = np.asarray(idx)
    onehot = jnp.asarray(np.eye(tbl.shape[1], dtype=np.float32)[idx.reshape(-1)])
    out = lax.dot_general(tbl, onehot, (((1,), (1,)), ((), ())), precision=lax.Precision.HIGHEST)
    return out.reshape((tbl.shape[0],) + idx.shape)


def _toeplitz_tile(tbl, far, width, dist0, valid):
    period = width + SEL_BLOCK + 1
    j = np.arange(period)
    dist = dist0 - np.where(j < width, j, j - period)
    gen = jnp.where(jnp.asarray(valid(dist))[None], _expand(tbl, np.clip(dist, 0, FAR_DIST)) - far, NEG)
    flat = jnp.tile(gen, (1, SEL_BLOCK))[:, :SEL_BLOCK * (period - 1)]
    return flat.reshape(tbl.shape[0], SEL_BLOCK, period - 1)[:, :, :width]


def _bias_tables(rel_bias):
    tbl = _expand(rel_bias.astype(F32).T, _bucket_table())
    far = tbl[:, FAR_DIST:]
    bias_near = _toeplitz_tile(tbl, far, 4 * SEL_BLOCK, 3 * SEL_BLOCK, lambda d: d >= 0)
    bias_win = _toeplitz_tile(tbl, 0.0, WINDOW + SEL_BLOCK, WINDOW, lambda d: (d >= 0) & (d < WINDOW))
    d_c = (np.arange(SEL_BLOCK)[:, None] + (CMP_NEAR // 2) * CMP_STRIDE - (CMP_BLOCK - 1)
           - CMP_STRIDE * np.arange(CMP_NEAR)[None, :])
    assert d_c[:, 0].min() >= FAR_DIST and d_c[:, -1].max() < 0
    bias_cmp = jnp.where(jnp.asarray(d_c >= 0)[None], _expand(tbl, np.clip(d_c, 0, FAR_DIST)) - far[:, :, None], 0.0)
    return bias_cmp, bias_near, bias_win


def _cmp_mask_dist(seq):
    a = np.tile(np.arange(SEL_BLOCK), NSA_REP)[:, None]
    c = np.arange(seq // CMP_STRIDE)[None, :]
    return jnp.asarray(c * CMP_STRIDE + CMP_BLOCK - 1 - a, jnp.int32)


def _overlap_t(seq):
    ncp = seq // CMP_STRIDE
    cs = np.arange(ncp)[None, :] * CMP_STRIDE
    ss = np.arange(SEL_BLOCK)[:, None] * SEL_BLOCK
    ov = (cs < ss + SEL_BLOCK) & (cs + CMP_BLOCK - 1 >= ss) & (np.arange(ncp)[None, :] < (seq - CMP_BLOCK) // CMP_STRIDE + 1)
    return jnp.asarray(ov, BF16)


def _softmax_step(s, v, state):
    m, l, acc = state
    mn = jnp.maximum(m, jnp.max(s, axis=-1, keepdims=True))
    alpha = jnp.exp(m - mn)
    p = jnp.exp(s - mn)
    return mn, alpha * l + jnp.sum(p, axis=-1, keepdims=True), alpha * acc + _dot(p.astype(BF16), v)


FAR_CHUNK = 1024


def _lane_fold(x, op):
    out = x[:, 0:LANE]
    for c in range(1, x.shape[1] // LANE):
        out = op(out, x[:, LANE * c:LANE * (c + 1)])
    return out


SEL_TILE = 8


def _nsa_local_kernel(q_ref, kv_ref, kc_ref, vc_ref, g_ref, bc_ref, bw_ref, ovt_ref, dmask_ref, neg_ref, oc_ref, *, pad):
    t = pl.program_id(1)
    qb, rep, ng = SEL_BLOCK, NSA_REP, NSA_GROUPS
    rows = qb * rep
    gw = 4 * LANE
    ncp = kc_ref.shape[0]
    wk = WINDOW + qb
    m_i = lax.broadcasted_iota(jnp.int32, (CMP_NEAR, ncp), 0)
    c_i = lax.broadcasted_iota(jnp.int32, (CMP_NEAR, ncp), 1)
    w_lane = lax.broadcasted_iota(jnp.int32, (rows, wk), 1)
    dist = dmask_ref[...]
    ovt = ovt_ref[...]
    gates = g_ref[...]
    low = lax.broadcasted_iota(jnp.int32, (qb, LANE), 1) < NSA_DK
    jj = lax.broadcasted_iota(jnp.int32, (qb, ng * qb), 0)
    bases = []
    for g in range(ng):
        base = bc_ref[rep * g:rep * (g + 1)].reshape(rows, CMP_NEAR)
        b_hi = base.astype(BF16)
        bases.append((b_hi, (base - b_hi.astype(F32)).astype(BF16)))
    blocks = [(u, t * SEL_TILE + u, slice(qb * u, qb * (u + 1))) for u in range(SEL_TILE)]
    chains = [(u, i, tok, g) for u, i, tok in blocks for g in range(ng)]
    qq = [jnp.concatenate([q_ref[tok, gw * g + LANE * r:gw * g + LANE * (r + 1)] for r in range(rep)], axis=0)
          for _, _, tok, g in chains]
    shift = [jnp.where(c_i - m_i == (qb // CMP_STRIDE) * i - CMP_NEAR // 2, 1.0, 0.0).astype(BF16) for _, i, _ in blocks]
    s_c = [jnp.where(dist <= i * qb,
                     _dot_nt(qq[n], kc_ref[:, LANE * g:LANE * (g + 1)]) + _dot(bases[g][0], shift[u])
                     + _dot(bases[g][1], shift[u]), NEG) for n, (u, i, _, g) in enumerate(chains)]
    s_w = [jnp.where(w_lane + i * qb - WINDOW >= 0,
                     _dot_nt(qq[n], kv_ref[pl.ds(pl.multiple_of(pad - WINDOW + i * qb, qb), wk),
                                           gw * g + 2 * LANE:gw * g + 3 * LANE])
                     + bw_ref[rep * g:rep * (g + 1)].reshape(rows, wk), NEG) for n, (u, i, _, g) in enumerate(chains)]
    e_c = [jnp.exp(s - jnp.maximum(jnp.max(s, axis=-1, keepdims=True), -1e20)) for s in s_c]
    e_w = [jnp.exp(s - jnp.max(s, axis=-1, keepdims=True)) for s in s_w]
    inv = [1.0 / jnp.maximum(jnp.sum(e, axis=-1, keepdims=True), 1e-30) for e in e_c]
    o_c = [_dot(e_c[n].astype(BF16), vc_ref[:, LANE * g:LANE * (g + 1)]) * inv[n] for n, (_, _, _, g) in enumerate(chains)]
    pv_w = [_dot(e_w[n].astype(BF16), kv_ref[pl.ds(pl.multiple_of(pad - WINDOW + i * qb, qb), wk),
                                            gw * g + 3 * LANE:gw * g + 4 * LANE]) for n, (_, i, _, g) in enumerate(chains)]
    o_w = [pv / pv[:, NSA_DK:NSA_DK + 1] for pv in pv_w]
    for n, (_, _, tok, g) in enumerate(chains):
        part = []
        for r in range(rep):
            sl = slice(qb * r, qb * (r + 1))
            c0 = LANE * g + 3 * r
            part.append(gates[tok, c0:c0 + 1] * o_c[n][sl] + gates[tok, c0 + 2:c0 + 3] * o_w[n][sl])
        for pr in range(rep // 2):
            c0 = 2 * LANE * g + LANE * pr
            oc_ref[tok, c0:c0 + LANE] = jnp.where(low, part[2 * pr], pltpu.roll(part[2 * pr + 1], NSA_DK, 1))
    imps = []
    for u, i, _ in blocks:
        p = [e_c[ng * u + g] * inv[ng * u + g] for g in range(ng)]
        psum = jnp.concatenate([x[0:qb] + x[qb:2 * qb] + x[2 * qb:3 * qb] + x[3 * qb:4 * qb] for x in p], axis=0)
        hi = psum.astype(BF16)
        r1 = psum - hi.astype(F32)
        mid = r1.astype(BF16)
        lo = (r1 - mid.astype(F32)).astype(BF16)
        imp = _dot_nt(ovt, hi) + _dot_nt(ovt, mid) + _dot_nt(ovt, lo)
        imp = jnp.where((jj == i) | (jj == 0), FORCED_SCORE, imp)
        imps.append(jnp.where(jj > i, -1.0, imp))
    grp = [[imp[8 * a:8 * a + 8] for a in range(8)] for imp in imps]
    cnt = [[jnp.zeros((8, ng * qb), F32) for _ in range(8)] for _ in imps]
    sub = lax.broadcasted_iota(jnp.int32, (8, ng * qb), 0)
    for k in range(qb):
        for u in range(SEL_TILE):
            rk = imps[u][k:k + 1, :]
            for a in range(8):
                if 8 * a + 7 <= k:
                    cnt[u][a] = cnt[u][a] + jnp.where(rk > grp[u][a], 1.0, 0.0)
                elif 8 * a > k:
                    cnt[u][a] = cnt[u][a] + jnp.where(rk >= grp[u][a], 1.0, 0.0)
                else:
                    cnt[u][a] = cnt[u][a] + jnp.where(sub + 8 * a > k, jnp.where(rk >= grp[u][a], 1.0, 0.0),
                                                      jnp.where(rk > grp[u][a], 1.0, 0.0))
    for u in range(SEL_TILE):
        neg_ref[u] = jnp.where(jnp.concatenate(cnt[u], axis=0) < SEL_TOP_N, 0.0, SEL_MASK)


ATT_TILE = 2


def _nsa_attend_kernel(q_ref, kv_ref, g_ref, neg_ref, oc_ref, bn_ref, o_ref, s_ref, *, pad):
    t = pl.program_id(1)
    qb, rep, ng = SEL_BLOCK, NSA_REP, NSA_GROUPS
    rows = qb * rep
    gw = 4 * LANE
    blocks = [(u, t * ATT_TILE + u, slice(qb * u, qb * (u + 1))) for u in range(ATT_TILE)]
    chains = [(u, i, tok, g) for u, i, tok in blocks for g in range(ng)]
    nc = len(chains)
    qs = [jnp.concatenate([q_ref[tok, gw * g + LANE * r:gw * g + LANE * (r + 1)] for r in range(rep)], axis=0)
          for _, _, tok, g in chains]
    jj = lax.broadcasted_iota(jnp.int32, (qb, ng * qb), 0)
    q_lane = lax.broadcasted_iota(jnp.int32, (qb, ng * qb), 1)
    zero = jnp.zeros((qb, ng * qb), BF16)
    ext_near = [jnp.concatenate([zero, neg_ref[u].astype(BF16)], axis=0) for u, _, _ in blocks]
    ext_far = [jnp.concatenate([zero, jnp.where(jj >= i - 3, SEL_MASK, neg_ref[u]).astype(BF16)], axis=0)
               for u, i, _ in blocks]
    pick = [jnp.where(q_lane == jj + qb * g, 1.0, 0.0).astype(BF16) for g in range(ng)]
    q_near = [qs[n] + jnp.concatenate([_dot_nt(pick[g], ext_near[u]).astype(BF16)] * rep, axis=0)
              for n, (u, _, _, g) in enumerate(chains)]
    q_far = [qs[n] + jnp.concatenate([_dot_nt(pick[g], ext_far[u]).astype(BF16)] * rep, axis=0)
             for n, (u, _, _, g) in enumerate(chains)]

    n_lane = lax.broadcasted_iota(jnp.int32, (rows, 4 * qb), 1)
    near_at = [pl.multiple_of(pad + (i - 3) * qb, qb) for _, i, _ in blocks]
    s_near = [jnp.where(n_lane + (i - 3) * qb >= 0,
                        _dot_nt(q_near[n], kv_ref[pl.ds(near_at[u], 4 * qb), gw * g:gw * g + LANE])
                        + bn_ref[rep * g:rep * (g + 1)].reshape(rows, 4 * qb), NEG) for n, (u, i, _, g) in enumerate(chains)]

    nch = (jnp.maximum(blocks[-1][1] - 3, 0) * qb + FAR_CHUNK - 1) // FAR_CHUNK

    def pass1(c, mx):
        st = pl.multiple_of(pad + c * FAR_CHUNK, math.gcd(pad, FAR_CHUNK))
        col = pl.multiple_of(c * FAR_CHUNK, FAR_CHUNK)
        out = []
        for n, (_, _, _, g) in enumerate(chains):
            s = _dot_nt(q_far[n], kv_ref[pl.ds(st, FAR_CHUNK), gw * g:gw * g + LANE])
            s_ref[n, :, pl.ds(col, FAR_CHUNK)] = s
            out.append(jnp.maximum(mx[n], _lane_fold(s, jnp.maximum)))
        return tuple(out)

    mx = lax.fori_loop(0, nch, pass1, tuple(jnp.full((rows, LANE), NEG, F32) for _ in range(nc)))
    ms = [jnp.max(jnp.maximum(mx[n], _lane_fold(s_near[n], jnp.maximum)), axis=-1, keepdims=True) for n in range(nc)]

    def pass2(c, state):
        st = pl.multiple_of(pad + c * FAR_CHUNK, math.gcd(pad, FAR_CHUNK))
        col = pl.multiple_of(c * FAR_CHUNK, FAR_CHUNK)
        out = []
        for n, (_, _, _, g) in enumerate(chains):
            p = jnp.exp(s_ref[n, :, pl.ds(col, FAR_CHUNK)] - ms[n])
            out.append(state[n] + _dot(p.astype(BF16), kv_ref[pl.ds(st, FAR_CHUNK), gw * g + LANE:gw * g + 2 * LANE]))
        return tuple(out)

    far = lax.fori_loop(0, nch, pass2, tuple(jnp.zeros((rows, LANE), F32) for _ in range(nc)))
    p_near = [jnp.exp(s_near[n] - ms[n]).astype(BF16) for n in range(nc)]
    pv = [far[n] + _dot(p_near[n], kv_ref[pl.ds(near_at[u], 4 * qb), gw * g + LANE:gw * g + 2 * LANE])
          for n, (u, _, _, g) in enumerate(chains)]
    o_s = [x / x[:, NSA_DK:NSA_DK + 1] for x in pv]

    gates = g_ref[...]
    low = lax.broadcasted_iota(jnp.int32, (qb, LANE), 1) < NSA_DK
    for n, (_, _, tok, g) in enumerate(chains):
        outs = [gates[tok, LANE * g + 3 * r + 1:LANE * g + 3 * r + 2] * o_s[n][qb * r:qb * (r + 1)] for r in range(rep)]
        for pr in range(rep // 2):
            c0 = 2 * LANE * g + LANE * pr
            pair = jnp.where(low, outs[2 * pr], pltpu.roll(outs[2 * pr + 1], NSA_DK, 1))
            o_ref[tok, c0:c0 + LANE] = (oc_ref[tok, c0:c0 + LANE] + pair).astype(o_ref.dtype)


def _nsa_attention(qn, kv, kc, vc, gates, rel_bias, pad):
    bsz, seq, _ = qn.shape
    qb, ng = SEL_BLOCK, NSA_GROUPS
    nq = seq // qb
    ncp = seq // CMP_STRIDE
    assert nq <= qb and nq % SEL_TILE == 0 and nq % ATT_TILE == 0 and pad >= WINDOW and seq % FAR_CHUNK == 0
    bias_cmp, bias_near, bias_win = _bias_tables(rel_bias)
    ovt = _overlap_t(seq)
    dmask = _cmp_mask_dist(seq)
    width = NSA_HEADS * NSA_DK
    tile = lambda n, w: pl.BlockSpec((None, n, w), lambda b, i: (b, i, 0))
    per_b = lambda n, w: pl.BlockSpec((None, n, w), lambda b, i: (b, 0, 0))
    neg, o_local = pl.pallas_call(
        functools.partial(_nsa_local_kernel, pad=pad), grid=(bsz, nq // SEL_TILE),
        in_specs=[tile(SEL_TILE * qb, ng * 4 * LANE), per_b(seq + pad, ng * 4 * LANE), per_b(ncp, ng * LANE),
                  per_b(ncp, ng * LANE), tile(SEL_TILE * qb, ng * LANE), _const_spec(bias_cmp.shape),
                  _const_spec(bias_win.shape), _const_spec(ovt.shape), _const_spec(dmask.shape)],
        out_specs=[pl.BlockSpec((None, SEL_TILE, qb, ng * qb), lambda b, i: (b, i, 0, 0)), tile(SEL_TILE * qb, width)],
        out_shape=[jax.ShapeDtypeStruct((bsz, nq, qb, ng * qb), F32), jax.ShapeDtypeStruct((bsz, seq, width), F32)],
        compiler_params=_params("parallel", "arbitrary"), name="nsa_local",
    )(qn, kv, kc, vc, gates, bias_cmp, bias_win, ovt, dmask)
    return pl.pallas_call(
        functools.partial(_nsa_attend_kernel, pad=pad), grid=(bsz, nq // ATT_TILE),
        in_specs=[tile(ATT_TILE * qb, ng * 4 * LANE), per_b(seq + pad, ng * 4 * LANE), tile(ATT_TILE * qb, ng * LANE),
                  pl.BlockSpec((None, ATT_TILE, qb, ng * qb), lambda b, i: (b, i, 0, 0)), tile(ATT_TILE * qb, width),
                  _const_spec(bias_near.shape)],
        out_specs=tile(ATT_TILE * qb, width),
        out_shape=jax.ShapeDtypeStruct((bsz, seq, width), BF16),
        scratch_shapes=[pltpu.VMEM((ATT_TILE * ng, NSA_REP * qb, seq), F32)],
        compiler_params=_params("parallel", "arbitrary"), name="nsa_attention",
    )(qn, kv, gates, neg, o_local, bias_near)


def _mla_attn_kernel(q_ref, k_ref, v_ref, o_ref, *, tq, ck):
    i = pl.program_id(2)
    heads = [slice(LANE * hh, LANE * (hh + 1)) for hh in range(2)]
    qs = [q_ref[:, sl] for sl in heads]

    def body(c, states):
        st = pl.multiple_of(c * ck, ck)
        v = v_ref[pl.ds(st, ck), :]
        return tuple(_softmax_step(_dot_nt(q, k_ref[pl.ds(st, ck), sl]), v, state)
                     for q, sl, state in zip(qs, heads, states))

    init = (jnp.full((tq, 1), NEG, F32), jnp.zeros((tq, 1), F32), jnp.zeros((tq, LANE), F32))
    states = lax.fori_loop(0, i * (tq // ck), body, (init, init))
    st = pl.multiple_of(i * tq, tq)
    causal = lax.broadcasted_iota(jnp.int32, (tq, tq), 0) >= lax.broadcasted_iota(jnp.int32, (tq, tq), 1)
    outs = []
    for q, sl, state in zip(qs, heads, states):
        s = _dot_nt(q, k_ref[pl.ds(st, tq), sl])
        _, l, acc = _softmax_step(jnp.where(causal, s, NEG), v_ref[pl.ds(st, tq), :], state)
        outs.append(acc / l)
    low = lax.broadcasted_iota(jnp.int32, (tq, LANE), 1) < MLA_V
    o_ref[...] = jnp.where(low, outs[0], outs[1]).astype(o_ref.dtype)


def _mla_attention(qm, km, vm, tq=1024, ck=1024):
    bsz, seq, _ = qm.shape
    tq = min(tq, seq)
    return pl.pallas_call(
        functools.partial(_mla_attn_kernel, tq=tq, ck=min(ck, tq)), grid=(bsz, MLA_HEADS // 2, seq // tq),
        in_specs=[pl.BlockSpec((None, tq, 2 * LANE), lambda b, h, i: (b, i, h)),
                  pl.BlockSpec((None, seq, 2 * LANE), lambda b, h, i: (b, 0, h)),
                  pl.BlockSpec((None, seq, LANE), lambda b, h, i: (b, 0, h))],
        out_specs=pl.BlockSpec((None, tq, LANE), lambda b, h, i: (b, i, h)),
        out_shape=jax.ShapeDtypeStruct((bsz, seq, MLA_HEADS * MLA_V), BF16),
        compiler_params=_params("parallel", "parallel", "arbitrary"), name="mla_attention",
    )(qm, km, vm)


def _even_out_kernel(x_ref, on_ref, om_ref, wn_ref, wm_ref, g_ref, wg_ref, wu_ref, wd_ref, o_ref, *, chunk):
    x1 = x_ref[...] + _dot(on_ref[...], wn_ref[...]) + _dot(om_ref[...], wm_ref[...])
    n = _rms(x1, g_ref[...], x1.shape[-1]).astype(BF16)
    ffn = None
    for f0 in range(0, wg_ref.shape[1], chunk):
        gate = _dot(n, wg_ref[:, f0:f0 + chunk])
        act = (gate * jax.nn.sigmoid(gate) * _dot(n, wu_ref[:, f0:f0 + chunk])).astype(BF16)
        part = _dot(act, wd_ref[f0:f0 + chunk, :])
        ffn = part if ffn is None else ffn + part
    o_ref[...] = x1 + ffn


def _even_out_ffn(x, o_nsa, o_mla, w_out, gain, w_gate, w_up, w_down, tm=ROW_TILE):
    t, d = x.shape
    dff = w_gate.shape[1]
    wn, wm = w_out[:o_nsa.shape[1]].astype(BF16), w_out[o_nsa.shape[1]:].astype(BF16)
    row = lambda width: pl.BlockSpec((tm, width), lambda i: (i, 0))
    return pl.pallas_call(
        functools.partial(_even_out_kernel, chunk=dff // 2), grid=(t // tm,),
        in_specs=[row(d), row(o_nsa.shape[1]), row(o_mla.shape[1]), _const_spec(wn.shape), _const_spec(wm.shape),
                  _const_spec((1, d)), _const_spec((d, dff)), _const_spec((d, dff)), _const_spec((dff, d))],
        out_specs=row(d), out_shape=jax.ShapeDtypeStruct((t, d), F32),
        compiler_params=_params("parallel"), name="even_out_ffn",
    )(x, o_nsa, o_mla, wn, wm, gain[None, :], w_gate.astype(BF16), w_up.astype(BF16), w_down.astype(BF16))


def _conv_kernel(x_ref, g_ref, win_ref, cw_ref, wout_ref, o_ref, vbuf_ref, *, tm):
    j = pl.program_id(1)
    x = x_ref[...]
    d = x.shape[-1]
    n = _rms(x, g_ref[...], d).astype(BF16)
    b_gate = _dot(n, win_ref[:, 0:d])
    v = _dot(n, win_ref[:, d:2 * d]) * _dot(n, win_ref[:, 2 * d:3 * d])

    @pl.when(j == 0)
    def _():
        vbuf_ref[0:8, :] = jnp.zeros((8, d), F32)

    vbuf_ref[8:8 + tm, :] = v
    cw = cw_ref[...]
    y = cw[2:3] * v + cw[1:2] * vbuf_ref[7:7 + tm, :] + cw[0:1] * vbuf_ref[6:6 + tm, :]
    vbuf_ref[0:8, :] = v[tm - 8:tm]
    o_ref[...] = x + _dot((b_gate * y).astype(BF16), wout_ref[...])


def _conv_mixer(x, gain, w_in, conv_w, w_out, tm=ROW_TILE):
    bsz, seq, d = x.shape
    cw = jnp.pad(conv_w.astype(F32), ((0, 8 - CONV_WIDTH), (0, 0)))
    tok = pl.BlockSpec((None, tm, d), lambda b, j: (b, j, 0))
    return pl.pallas_call(
        functools.partial(_conv_kernel, tm=tm), grid=(bsz, seq // tm),
        in_specs=[tok, _const_spec((1, d)), _const_spec((d, 3 * d)), _const_spec((8, d)), _const_spec((d, d))],
        out_specs=tok, out_shape=jax.ShapeDtypeStruct((bsz, seq, d), F32),
        scratch_shapes=[pltpu.VMEM((tm + 8, d), F32)],
        compiler_params=_params("parallel", "arbitrary"), name="conv_mixer",
    )(x, gain[None, :], w_in.astype(BF16), cw, w_out.astype(BF16))


def _router_kernel(x_ref, g_ref, wr_ref, br_ref, tri_ref, route_ref, cnt_ref, carry_ref, *, tm):
    t = pl.program_id(0)

    @pl.when(t == 0)
    def _():
        carry_ref[...] = jnp.zeros_like(carry_ref)

    n = _rms(x_ref[...], g_ref[...], x_ref.shape[-1])
    hi = n.astype(BF16)
    lo = (n - hi.astype(F32)).astype(BF16)
    whi, wlo = wr_ref[0], wr_ref[1]
    logits = _dot(hi, whi) + _dot(lo, whi) + _dot(hi, wlo) + br_ref[...]
    lane = lax.broadcasted_iota(jnp.int32, (tm, LANE), 1).astype(F32)
    big = float(LANE)
    m1 = jnp.max(logits, axis=-1, keepdims=True)
    e1 = jnp.min(jnp.where(logits == m1, lane, big), axis=-1, keepdims=True)
    rest = jnp.where(lane == e1, NEG, logits)
    m2 = jnp.max(rest, axis=-1, keepdims=True)
    e2 = jnp.min(jnp.where(rest == m2, lane, big), axis=-1, keepdims=True)
    z = jnp.exp(m2 - m1)
    w1 = 1.0 / (1.0 + z)
    w2 = z / (1.0 + z)
    oh1 = jnp.where(lane == e1, 1.0, 0.0)
    oh2 = jnp.where(lane == e2, 1.0, 0.0)
    both = oh1 + oh2
    before = _dot(tri_ref[...], both.astype(BF16)) + carry_ref[0:1, :]
    r1 = jnp.sum(oh1 * before, axis=-1, keepdims=True)
    r2 = jnp.sum(oh2 * before, axis=-1, keepdims=True)
    cols = [e1, e2, w1, w2, r1, r2]
    out = jnp.zeros((tm, LANE), F32)
    for c, val in enumerate(cols):
        out = jnp.where(lane == c, val, out)
    route_ref[...] = out
    carry_ref[0:1, :] = carry_ref[0:1, :] + jnp.sum(both, axis=0, keepdims=True)
    cnt_ref[...] = carry_ref[...]


def _moe_router(x, gain, w_router, b_router, tm=ROW_TILE):
    t, d = x.shape
    wr = jnp.pad(w_router.astype(F32), ((0, 0), (0, LANE - N_EXPERTS)))
    whi = wr.astype(BF16)
    wlo = (wr - whi.astype(F32)).astype(BF16)
    br = jnp.concatenate([b_router.astype(F32), jnp.full((LANE - N_EXPERTS,), NEG, F32)])[None, :]
    tri = jnp.asarray(np.tril(np.ones((tm, tm), np.float32), -1), BF16)
    route, cnt = pl.pallas_call(
        functools.partial(_router_kernel, tm=tm), grid=(t // tm,),
        in_specs=[pl.BlockSpec((tm, d), lambda i: (i, 0)), _const_spec((1, d)), _const_spec((2, d, LANE)),
                  _const_spec((1, LANE)), _const_spec((tm, tm))],
        out_specs=[pl.BlockSpec((tm, LANE), lambda i: (i, 0)), pl.BlockSpec((8, LANE), lambda i: (0, 0))],
        out_shape=[jax.ShapeDtypeStruct((t, LANE), F32), jax.ShapeDtypeStruct((8, LANE), F32)],
        scratch_shapes=[pltpu.VMEM((8, LANE), F32)],
        compiler_params=_params("arbitrary"), name="moe_router",
    )(x, gain[None, :], jnp.stack([whi, wlo]), br, tri)
    return route, cnt[0, :N_EXPERTS].astype(jnp.int32)


def _row_copy(src, i, dst, j, sem):
    return pltpu.make_async_copy(src.at[pl.ds(i, 1)], dst.at[pl.ds(j, 1)], sem)


def _scatter_kernel(fill_ref, dest_ref, x_ref, g_ref, xs_ref, xn_ref, zero_ref, sems, *, tm, rb):
    @pl.when(pl.program_id(0) == 0)
    def _():
        sem = sems.at[0]
        zero_ref[...] = jnp.zeros_like(zero_ref)
        sizes = [rb >> s for s in range(rb.bit_length() - 3)]
        for e in range(N_EXPERTS + 1):
            lo, n = fill_ref[e], fill_ref[N_EXPERTS + 1 + e] - fill_ref[e]
            whole = n // rb

            def copy(off, size):
                return pltpu.make_async_copy(zero_ref.at[pl.ds(0, size)],
                                             xs_ref.at[pl.ds(pl.multiple_of(off, 8), size)], sem)

            def blocks(k, c, lo=lo, copy=copy):
                copy(lo + k * rb, rb).start()
                copy(lo + k * rb, rb).wait()
                return c

            lax.fori_loop(0, whole, blocks, 0)
            off = lo + whole * rb
            for size in sizes[1:]:
                @pl.when((n & size) != 0)
                def _(off=off, size=size, copy=copy):
                    copy(off, size).start()
                    copy(off, size).wait()
                off = off + (n & size)

    t = pl.program_id(0)
    slot = t % 2
    rows, row_sem = xn_ref.at[slot], sems.at[slot]
    rows[...] = _rms(x_ref[...], g_ref[...], x_ref.shape[-1])

    def start(r, c):
        _row_copy(rows, r, xs_ref, dest_ref[0, 0, r], row_sem).start()
        _row_copy(rows, r, xs_ref, dest_ref[0, 0, tm + r], row_sem).start()
        return c

    lax.fori_loop(0, tm, start, 0, unroll=DMA_ISSUE_UNROLL)

    def drain(s):
        for _ in range(TOP_K):
            pltpu.make_async_copy(xn_ref.at[s], xs_ref.at[pl.ds(0, tm)], sems.at[s]).wait()

    @pl.when(t > 0)
    def _():
        drain(1 - slot)

    @pl.when(t == pl.num_programs(0) - 1)
    def _():
        drain(slot)


def _moe_scatter(x, gain, dest, fill, n_rows, tm, rb):
    t, d = x.shape
    nt = t // tm
    grid_spec = pltpu.PrefetchScalarGridSpec(
        num_scalar_prefetch=1, grid=(nt,),
        in_specs=[pl.BlockSpec((1, 1, 2 * tm), lambda i, f: (i, 0, 0), memory_space=pltpu.SMEM),
                  pl.BlockSpec((tm, d), lambda i, f: (i, 0)),
                  pl.BlockSpec((1, d), lambda i, f: (0, 0))],
        out_specs=pl.BlockSpec(memory_space=pl.ANY),
        scratch_shapes=[pltpu.VMEM((2, tm, d), F32), pltpu.VMEM((rb, d), F32), pltpu.SemaphoreType.DMA((2,))])
    return pl.pallas_call(
        functools.partial(_scatter_kernel, tm=tm, rb=rb), grid_spec=grid_spec,
        out_shape=jax.ShapeDtypeStruct((n_rows, d), F32),
        compiler_params=_params("arbitrary"), name="moe_scatter",
    )(fill, dest, x, gain[None, :])


def _expert_kernel(be_ref, nu_ref, x_ref, wg_ref, wu_ref, wd_ref, o_ref):
    used = pl.program_id(0) < nu_ref[0]

    @pl.when(used)
    def _():
        xb = x_ref[...].astype(BF16)
        gate = _dot(xb, wg_ref[...])
        act = (gate * jax.nn.sigmoid(gate) * _dot(xb, wu_ref[...])).astype(BF16)
        o_ref[...] = _dot(act, wd_ref[...])

    @pl.when(jnp.logical_not(used))
    def _():
        o_ref[...] = jnp.zeros_like(o_ref)


def _moe_experts(xs, blk_expert, n_used, w_gate, w_up, w_down, rb):
    n_rows, d = xs.shape
    dff = w_gate.shape[-1]
    rowblk = pl.BlockSpec((rb, d), lambda i, be, nu: (i, 0))
    wspec = lambda shape: pl.BlockSpec((None,) + shape, lambda i, be, nu: (be[i], 0, 0))
    grid_spec = pltpu.PrefetchScalarGridSpec(
        num_scalar_prefetch=2, grid=(n_rows // rb,),
        in_specs=[rowblk, wspec((d, dff)), wspec((d, dff)), wspec((dff, d))], out_specs=rowblk)
    return pl.pallas_call(
        _expert_kernel, grid_spec=grid_spec, out_shape=jax.ShapeDtypeStruct(xs.shape, F32),
        compiler_params=_params("arbitrary"), name="moe_experts",
    )(blk_expert, n_used, xs, w_gate.astype(BF16), w_up.astype(BF16), w_down.astype(BF16))


def _combine_kernel(dest_ref, next_ref, x_ref, route_ref, ys_ref, o_ref, got_ref, sems, *, tm):
    t = pl.program_id(0)
    slot = t % 2

    def gather(idx_ref, s):
        def start(r, c):
            for k in range(TOP_K):
                _row_copy(ys_ref, idx_ref[0, 0, k * tm + r], got_ref.at[s, k], r, sems.at[s]).start()
            return c

        lax.fori_loop(0, tm, start, 0, unroll=DMA_ISSUE_UNROLL)

    @pl.when(t == 0)
    def _():
        gather(dest_ref, 0)

    @pl.when(t + 1 < pl.num_programs(0))
    def _():
        gather(next_ref, 1 - slot)

    for k in range(TOP_K):
        pltpu.make_async_copy(ys_ref.at[pl.ds(0, tm)], got_ref.at[slot, k], sems.at[slot]).wait()
    route = route_ref[...]
    o_ref[...] = x_ref[...] + route[:, 2:3] * got_ref[slot, 0] + route[:, 3:4] * got_ref[slot, 1]


def _moe_combine(x, route, dest, ys, tm):
    t, d = x.shape
    nt = t // tm
    idx = lambda nxt: pl.BlockSpec((1, 1, TOP_K * tm), lambda i: (jnp.minimum(i + nxt, nt - 1), 0, 0),
                                   memory_space=pltpu.SMEM)
    return pl.pallas_call(
        functools.partial(_combine_kernel, tm=tm), grid=(nt,),
        in_specs=[idx(0), idx(1), pl.BlockSpec((tm, d), lambda i: (i, 0)), pl.BlockSpec((tm, LANE), lambda i: (i, 0)),
                  pl.BlockSpec(memory_space=pl.ANY)],
        out_specs=pl.BlockSpec((tm, d), lambda i: (i, 0)),
        out_shape=jax.ShapeDtypeStruct((t, d), F32),
        scratch_shapes=[pltpu.VMEM((2, TOP_K, tm, d), F32), pltpu.SemaphoreType.DMA((2,))],
        compiler_params=_params("arbitrary"), name="moe_combine",
    )(dest, dest, x, route, ys)


def _moe(x, gain, w_router, b_router, w_gate, w_up, w_down, rb=MOE_ROW_BLOCK, tm=MOE_DMA_TILE):
    t, d = x.shape
    route, counts = _moe_router(x, gain, w_router, b_router)
    n_blk = (t * TOP_K + rb - 1) // rb + N_EXPERTS
    padded = (counts + rb - 1) // rb * rb
    pad_end = jnp.cumsum(padded)
    pad_start = pad_end - padded
    dest = pad_start[route[:, 0:2].astype(jnp.int32)] + route[:, 4:6].astype(jnp.int32)
    dest_t = jnp.concatenate([dest[:, 0].reshape(t // tm, 1, tm), dest[:, 1].reshape(t // tm, 1, tm)], axis=2)
    fill = jnp.concatenate([(pad_start + counts) // 8 * 8, pad_end[-1:], pad_end, jnp.full((1,), n_blk * rb, jnp.int32)])
    n_used = pad_end[-1:] // rb
    blk = jnp.minimum(jnp.arange(n_blk, dtype=jnp.int32), n_used - 1) * rb
    blk_expert = jnp.sum(pad_end[None, :] <= blk[:, None], axis=1).astype(jnp.int32)
    xs = _moe_scatter(x, gain, dest_t, fill.astype(jnp.int32), n_blk * rb, tm, rb)
    ys = _moe_experts(xs, blk_expert, n_used.astype(jnp.int32), w_gate, w_up, w_down, rb)
    return _moe_combine(x, route, dest_t, ys, tm)


def kernel(x, rel_bias, ev_mix_norm, ev_w_in, nsa_q_norm, nsa_k_norm, nsa_cmp_pos, nsa_cmp_w1, nsa_cmp_w2, mla_cq_norm, mla_ckv_norm, mla_w_uq, mla_w_ukv, mla_q_norm, mla_k_norm, ev_w_out, ev_ffn_norm, ffn_w_gate, ffn_w_up, ffn_w_down, od_mix_norm, od_w_in, conv_w, od_w_out, od_ffn_norm, moe_w_router, moe_b_router, moe_w_gate, moe_w_up, moe_w_down):
    bsz, seq, d = x.shape
    depth = ev_mix_norm.shape[0] + od_mix_norm.shape[0]
    pad = KV_PAD
    for layer in range(depth):
        i = layer // 2
        if layer % 2 == 0:
            qn, kv, kc_raw, vc_raw, gates, qm, km, vm = _even_in_proj(
                x, ev_mix_norm[i], ev_w_in[i], nsa_q_norm[i], nsa_k_norm[i], mla_cq_norm[i], mla_ckv_norm[i],
                mla_w_uq[i], mla_w_ukv[i], mla_q_norm[i], mla_k_norm[i], tm=pad)
            kc = _compress(kc_raw, nsa_cmp_pos[i, 0], nsa_cmp_w1[i, 0], nsa_cmp_w2[i, 0], nsa_k_norm[i, 0], True)
            vc = _compress(vc_raw, nsa_cmp_pos[i, 1], nsa_cmp_w1[i, 1], nsa_cmp_w2[i, 1], nsa_k_norm[i, 0], False)
            o_nsa = _nsa_attention(qn, kv, kc, vc, gates, rel_bias, pad)
            o_mla = _mla_attention(qm, km, vm)
            x = _even_out_ffn(x.reshape(bsz * seq, d), o_nsa.reshape(bsz * seq, -1), o_mla.reshape(bsz * seq, -1),
                              ev_w_out[i], ev_ffn_norm[i], ffn_w_gate[i], ffn_w_up[i], ffn_w_down[i]).reshape(bsz, seq, d)
        else:
            x = _conv_mixer(x, od_mix_norm[i], od_w_in[i], conv_w[i], od_w_out[i])
            x = _moe(x.reshape(bsz * seq, d), od_ffn_norm[i], moe_w_router[i], moe_b_router[i],
                     moe_w_gate[i], moe_w_up[i], moe_w_down[i]).reshape(bsz, seq, d)
    return x
```

```python
import functools
import math

import jax
import jax.numpy as jnp
import numpy as np
from jax import lax
from jax.experimental import pallas as pl
from jax.experimental.pallas import tpu as pltpu

F32 = jnp.float32
BF16 = jnp.bfloat16

EPS = 1e-6
NEG = -1e30
FORCED_SCORE = 1e9
NSA_HEADS = 8
NSA_GROUPS = 2
NSA_REP = NSA_HEADS // NSA_GROUPS
NSA_DK = 64
CMP_BLOCK = 32
CMP_STRIDE = 16
CMP_HIDDEN = 256
SEL_BLOCK = 64
SEL_TOP_N = 16
WINDOW = 512
MLA_HEADS = 8
MLA_Q_RANK = 256
MLA_KV_RANK = 128
MLA_NOPE = 64
MLA_ROPE = 32
MLA_V = 64
ROPE_THETA = 10000.0
REL_BUCKETS = 32
REL_MAX_DIST = 128
CONV_WIDTH = 3
N_EXPERTS = 8
TOP_K = 2
EVEN_IN_SIZES = (512,) + (128,) * 6 + (24, 256, 128, 32)

LANE = 128
VMEM_LIMIT = 56 * 1024 * 1024
SEL_MASK = NEG
FAR_DIST = 128

ROW_TILE = 512
KV_PAD = ROW_TILE
MOE_ROW_BLOCK = 512
MOE_DMA_TILE = 512
DMA_ISSUE_UNROLL = 8

_Q0, _KV0, _KC0, _VC0, _GT0, _CQ0, _CKV0, _KR0, _EVEN_W = 0, 1024, 2048, 2176, 2304, 2560, 2816, 2944, 3072


def _dot(a, b):
    return jnp.dot(a, b, preferred_element_type=F32)


def _dot_nt(a, b):
    return lax.dot_general(a, b, (((1,), (1,)), ((), ())), preferred_element_type=F32)


def _rms(x, gain, n):
    ss = jnp.sum(x * x, axis=-1, keepdims=True) * (1.0 / n)
    return x * lax.rsqrt(ss + EPS) * gain


def _params(*sem):
    return pltpu.CompilerParams(dimension_semantics=sem, vmem_limit_bytes=VMEM_LIMIT)


def _const_spec(shape):
    nd = len(shape)
    return pl.BlockSpec(shape, lambda *_: (0,) * nd, pipeline_mode=pl.Buffered(1))


def _even_in_kernel(x_ref, gmix_ref, w_ref, gq_ref, gks_ref, gkw_ref, gcq_ref, gckv_ref,
                    wuq_ref, wuk_ref, wuv_ref, gmq_ref, gmk_ref, cos_ref, sa_ref, sb_ref,
                    qn_ref, kv_ref, kc_ref, vc_ref, gate_ref, qm_ref, km_ref, vm_ref, *, tm):
    j = pl.program_id(1)

    @pl.when(j == 0)
    def _():
        kv_ref[...] = jnp.zeros_like(kv_ref)
        qn_ref[...] = jnp.zeros_like(qn_ref)
        kc_ref[...] = jnp.zeros_like(kc_ref)
        vc_ref[...] = jnp.zeros_like(vc_ref)
        gate_ref[...] = jnp.zeros_like(gate_ref)
        qm_ref[...] = jnp.zeros_like(qm_ref)
        km_ref[...] = jnp.zeros_like(km_ref)
        vm_ref[...] = jnp.zeros_like(vm_ref)

    @pl.when(j > 0)
    def _():
        xn = _rms(x_ref[...], gmix_ref[...], x_ref.shape[-1]).astype(BF16)
        hq = _dot(xn, w_ref[:, _Q0:_Q0 + 1024])
        hkv = _dot(xn, w_ref[:, _KV0:_KV0 + 1024])
        hc = _dot(xn, w_ref[:, _KC0:_KC0 + 256])
        hg = _dot(xn, w_ref[:, _GT0:_GT0 + 256])
        hm = _dot(xn, w_ref[:, _CQ0:_EVEN_W])
        cq = _rms(hm[:, :MLA_Q_RANK], gcq_ref[...], MLA_Q_RANK).astype(BF16)
        ckv = _rms(hm[:, MLA_Q_RANK:MLA_Q_RANK + MLA_KV_RANK], gckv_ref[...], MLA_KV_RANK).astype(BF16)
        k_rope = hm[:, MLA_Q_RANK + MLA_KV_RANK:]
        qm = _dot(cq, wuq_ref[...])
        kn = _dot(ckv, wuk_ref[...])
        vm_ref[...] = _dot(ckv, wuv_ref[...]).astype(BF16)
        dqk = MLA_NOPE + MLA_ROPE
        nsa_heads = [slice(LANE * h, LANE * (h + 1)) for h in range(NSA_HEADS)]
        mla_heads = [slice(LANE * h, LANE * (h + 1)) for h in range(MLA_HEADS)]
        segs = [(hq[:, sl], gq_ref[...], NSA_DK, NSA_DK ** -0.5) for sl in nsa_heads]
        for g in range(NSA_GROUPS):
            o = 4 * LANE * g
            segs.append((hkv[:, o:o + LANE], gks_ref[...], NSA_DK, 1.0))
            segs.append((hkv[:, o + 2 * LANE:o + 3 * LANE], gkw_ref[...], NSA_DK, 1.0))
        n_nsa = len(segs)
        segs += [(qm[:, sl], gmq_ref[...], dqk, dqk ** -0.5) for sl in mla_heads]
        segs += [(kn[:, sl] + k_rope, gmk_ref[...], dqk, 1.0) for sl in mla_heads]
        sums = [jnp.sum(x * x, axis=-1, keepdims=True) for x, _, _, _ in segs]
        scales = [lax.rsqrt(ss * (1.0 / n) + EPS) for ss, (_, _, n, _) in zip(sums, segs)]
        normed = [x * sc * gain for sc, (x, gain, _, _) in zip(scales, segs)]
        for h, sl in enumerate(nsa_heads):
            qn_ref[:, sl] = (normed[h] * segs[h][3]).astype(BF16)
        pos = (j - 1) * tm + lax.broadcasted_iota(jnp.int32, (tm, LANE), 0)
        lane = lax.broadcasted_iota(jnp.int32, (tm, LANE), 1)
        onehot = jnp.where(lane - NSA_DK == pos // SEL_BLOCK, 1.0, 0.0)
        ones = jnp.where(lane >= NSA_DK, 1.0, 0.0)
        for g in range(NSA_GROUPS):
            o = 4 * LANE * g
            kv_ref[:, o:o + LANE] = (normed[NSA_HEADS + 2 * g] + onehot).astype(BF16)
            kv_ref[:, o + LANE:o + 2 * LANE] = (hkv[:, o + LANE:o + 2 * LANE] + ones).astype(BF16)
            kv_ref[:, o + 2 * LANE:o + 3 * LANE] = normed[NSA_HEADS + 2 * g + 1].astype(BF16)
            kv_ref[:, o + 3 * LANE:o + 4 * LANE] = (hkv[:, o + 3 * LANE:o + 4 * LANE] + ones).astype(BF16)
        kc_ref[...] = hc[:, :LANE].astype(BF16)
        vc_ref[...] = hc[:, LANE:].astype(BF16)
        gate_ref[...] = jax.nn.sigmoid(hg)
        cos, sa, sb = cos_ref[...], sa_ref[...], sb_ref[...]
        mla = normed[n_nsa:]
        fwd = [pltpu.roll(x, 16, 1) for x in mla]
        bwd = [pltpu.roll(x, LANE - 16, 1) for x in mla]
        roped = [x * cos + f * sa + b * sb for x, f, b in zip(mla, fwd, bwd)]
        for h, sl in enumerate(mla_heads):
            qm_ref[:, sl] = (roped[h] * segs[n_nsa + h][3]).astype(BF16)
            km_ref[:, sl] = roped[MLA_HEADS + h].astype(BF16)


def _even_in_weights(w_in):
    offs = np.concatenate([[0], np.cumsum(EVEN_IN_SIZES)])
    part = [w_in[:, offs[n]:offs[n + 1]] for n in range(len(EVEN_IN_SIZES))]
    q, k_c, v_c, k_s, v_s, k_w, v_w, gates, c_q, c_kv, k_rope = part
    d = w_in.shape[0]
    z = lambda n: jnp.zeros((d, n), w_in.dtype)
    cols = []
    for h in range(NSA_HEADS):
        cols += [q[:, 64 * h:64 * h + 64], z(64)]
    for g in range(NSA_GROUPS):
        s = slice(64 * g, 64 * g + 64)
        cols += [k_s[:, s], z(64), v_s[:, s], z(64), k_w[:, s], z(64), v_w[:, s], z(64)]
    cols += [k_c, v_c]
    for g in range(NSA_GROUPS):
        cols += [gates[:, 12 * g:12 * g + 12], z(LANE - 12)]
    cols += [c_q, c_kv, z(64), k_rope, z(32)]
    w = jnp.concatenate(cols, axis=1)
    assert w.shape[1] == _EVEN_W
    return w.astype(BF16)


def _pad_gain(g, width):
    return jnp.pad(g.astype(F32), (0, width - g.shape[0]))[None, :]


def _rope_tables(seq):
    half = MLA_ROPE // 2
    inv_freq = ROPE_THETA ** (-jnp.arange(half, dtype=F32) / half)
    ang = jnp.arange(seq).astype(F32)[:, None] * inv_freq[None, :]
    cos, sin = jnp.cos(ang), jnp.sin(ang)
    one = jnp.ones((seq, MLA_NOPE), F32)
    zn = jnp.zeros((seq, MLA_NOPE), F32)
    zt = jnp.zeros((seq, LANE - MLA_NOPE - MLA_ROPE), F32)
    zh = jnp.zeros((seq, half), F32)
    cos_t = jnp.concatenate([one, cos, cos, zt + 1.0], axis=1)
    sa = jnp.concatenate([zn, zh, sin, zt], axis=1)
    sb = jnp.concatenate([zn, -sin, zh, zt], axis=1)
    return cos_t, sa, sb


def _even_in_proj(x, gmix, w_in, q_norm, k_norm, cq_norm, ckv_norm, w_uq, w_ukv, mq_norm, mk_norm, tm=ROW_TILE):
    bsz, seq, d = x.shape
    nt = seq // tm
    w = _even_in_weights(w_in)
    dqk = MLA_NOPE + MLA_ROPE
    wuq = jnp.concatenate(
        [jnp.pad(w_uq[:, dqk * h:dqk * (h + 1)], ((0, 0), (0, LANE - dqk))) for h in range(MLA_HEADS)],
        axis=1).astype(BF16)
    kvw = MLA_NOPE + MLA_V
    wuk = jnp.concatenate(
        [jnp.pad(w_ukv[:, kvw * h:kvw * h + MLA_NOPE], ((0, 0), (0, LANE - MLA_NOPE))) for h in range(MLA_HEADS)],
        axis=1).astype(BF16)
    wuv = jnp.concatenate([w_ukv[:, kvw * h + MLA_NOPE:kvw * (h + 1)] for h in range(MLA_HEADS)], axis=1).astype(BF16)
    cos_t, sa, sb = _rope_tables(seq)
    tok = lambda width: pl.BlockSpec((None, tm, width), lambda b, j: (b, jnp.maximum(j - 1, 0), 0))
    postab = pl.BlockSpec((tm, LANE), lambda b, j: (jnp.maximum(j - 1, 0), 0))
    in_specs = [tok(d), _const_spec((1, d)), _const_spec((d, _EVEN_W)),
                _const_spec((1, LANE)), _const_spec((1, LANE)), _const_spec((1, LANE)),
                _const_spec((1, MLA_Q_RANK)), _const_spec((1, MLA_KV_RANK)),
                _const_spec(wuq.shape), _const_spec(wuk.shape), _const_spec(wuv.shape),
                _const_spec((1, LANE)), _const_spec((1, LANE)), postab, postab, postab]
    out_shape = [jax.ShapeDtypeStruct((bsz, seq, 1024), BF16),
                 jax.ShapeDtypeStruct((bsz, seq + tm, 1024), BF16),
                 jax.ShapeDtypeStruct((bsz, seq, LANE), BF16),
                 jax.ShapeDtypeStruct((bsz, seq, LANE), BF16),
                 jax.ShapeDtypeStruct((bsz, seq, 2 * LANE), F32),
                 jax.ShapeDtypeStruct((bsz, seq, 1024), BF16),
                 jax.ShapeDtypeStruct((bsz, seq, 1024), BF16),
                 jax.ShapeDtypeStruct((bsz, seq, 512), BF16)]
    out_specs = [tok(1024), pl.BlockSpec((None, tm, 1024), lambda b, j: (b, j, 0)), tok(LANE), tok(LANE),
                 tok(2 * LANE), tok(1024), tok(1024), tok(512)]
    return pl.pallas_call(
        functools.partial(_even_in_kernel, tm=tm), grid=(bsz, nt + 1), in_specs=in_specs, out_specs=out_specs,
        out_shape=out_shape, compiler_params=_params("parallel", "arbitrary"), name="even_in_proj",
    )(x, gmix[None, :], w, _pad_gain(q_norm, LANE), _pad_gain(k_norm[1], LANE), _pad_gain(k_norm[2], LANE),
      cq_norm[None, :], ckv_norm[None, :], wuq, wuk, wuv, _pad_gain(mq_norm, LANE), _pad_gain(mk_norm, LANE),
      cos_t, sa, sb)


def _cmp_kernel(x_ref, wa_ref, wb_ref, pa_ref, pb_ref, w2_ref, gain_ref, o_ref, *, normalize):
    x = x_ref[...]
    ua = _dot(x, wa_ref[...])
    ub = _dot(x, wb_ref[...])
    pt = _dot(pa_ref[...], wa_ref[...]) + _dot(pb_ref[...], wb_ref[...])
    n = ub.shape[0]
    pre = ua + pltpu.roll(ub, n - 1, 0) + pt[0:1]
    hid = jax.nn.gelu(pre).astype(BF16)
    out = _dot(hid, w2_ref[...])
    if normalize:
        gain = gain_ref[...]
        for g in range(NSA_GROUPS):
            sl = slice(LANE * g, LANE * (g + 1))
            o_ref[:, sl] = _rms(out[:, sl], gain, NSA_DK).astype(BF16)
    else:
        o_ref[...] = out.astype(BF16)


def _compress(kv, pos_emb, w1, w2, gain, normalize):
    bsz, seq, _ = kv.shape
    half = CMP_BLOCK // 2
    assert CMP_STRIDE == half
    nrow = seq // CMP_STRIDE
    x = kv.reshape(bsz, nrow, CMP_STRIDE * LANE)
    w1r = w1.reshape(CMP_BLOCK, NSA_DK, CMP_HIDDEN)
    eye = jnp.eye(NSA_GROUPS, dtype=w1.dtype)
    widen = lambda w: jnp.einsum("ldn,gh->lgdhn", w, eye).reshape(half * LANE, NSA_GROUPS * CMP_HIDDEN).astype(BF16)
    wa, wb = widen(w1r[:half]), widen(w1r[half:])
    prow = lambda p: jnp.pad(jnp.broadcast_to(p[:, None, :], (half, NSA_GROUPS, NSA_DK)).reshape(1, half * LANE),
                             ((0, 15), (0, 0))).astype(BF16)
    pa, pb = prow(pos_emb[:half]), prow(pos_emb[half:])
    second = jnp.zeros_like(w2) if normalize else w2
    w2w = jnp.einsum("nd,gh->gnhd", jnp.concatenate([w2, second], axis=1), eye)
    w2w = w2w.reshape(NSA_GROUPS * CMP_HIDDEN, NSA_GROUPS * LANE).astype(BF16)
    return pl.pallas_call(
        functools.partial(_cmp_kernel, normalize=normalize), grid=(bsz,),
        in_specs=[pl.BlockSpec((None, nrow, CMP_STRIDE * LANE), lambda b: (b, 0, 0)),
                  _const_spec(wa.shape), _const_spec(wb.shape), _const_spec(pa.shape), _const_spec(pb.shape),
                  _const_spec(w2w.shape), _const_spec((1, LANE))],
        out_specs=pl.BlockSpec((None, nrow, NSA_GROUPS * LANE), lambda b: (b, 0, 0)),
        out_shape=jax.ShapeDtypeStruct((bsz, nrow, NSA_GROUPS * LANE), BF16),
        compiler_params=_params("parallel"), name="nsa_compress",
    )(x, wa, wb, pa, pb, w2w, _pad_gain(gain, LANE))


def _bucket_table():
    dist = np.arange(FAR_DIST + 1)
    max_exact = REL_BUCKETS // 2
    nf = np.maximum(dist, max_exact).astype(np.float32)
    large = max_exact + (np.log(nf / max_exact) / math.log(REL_MAX_DIST / max_exact)
                         * (REL_BUCKETS - max_exact)).astype(np.int32)
    return np.where(dist < max_exact, dist, np.minimum(large, REL_BUCKETS - 1))


CMP_NEAR = 32


def _expand(tbl, idx):
    idx = np.asarray(idx)
    onehot = jnp.asarray(np.eye(tbl.shape[1], dtype=np.float32)[idx.reshape(-1)])
    out = lax.dot_general(tbl, onehot, (((1,), (1,)), ((), ())), precision=lax.Precision.HIGHEST)
    return out.reshape((tbl.shape[0],) + idx.shape)


def _toeplitz_tile(tbl, far, width, dist0, valid):
    period = width + SEL_BLOCK + 1
    j = np.arange(period)
    dist = dist0 - np.where(j < width, j, j - period)
    gen = jnp.where(jnp.asarray(valid(dist))[None], _expand(tbl, np.clip(dist, 0, FAR_DIST)) - far, NEG)
    flat = jnp.tile(gen, (1, SEL_BLOCK))[:, :SEL_BLOCK * (period - 1)]
    return flat.reshape(tbl.shape[0], SEL_BLOCK, period - 1)[:, :, :width]


def _bias_tables(rel_bias):
    tbl = _expand(rel_bias.astype(F32).T, _bucket_table())
    far = tbl[:, FAR_DIST:]
    bias_near = _toeplitz_tile(tbl, far, 4 * SEL_BLOCK, 3 * SEL_BLOCK, lambda d: d >= 0)
    bias_win = _toeplitz_tile(tbl, 0.0, WINDOW + SEL_BLOCK, WINDOW, lambda d: (d >= 0) & (d < WINDOW))
    d_c = (np.arange(SEL_BLOCK)[:, None] + (CMP_NEAR // 2) * CMP_STRIDE - (CMP_BLOCK - 1)
           - CMP_STRIDE * np.arange(CMP_NEAR)[None, :])
    assert d_c[:, 0].min() >= FAR_DIST and d_c[:, -1].max() < 0
    bias_cmp = jnp.where(jnp.asarray(d_c >= 0)[None], _expand(tbl, np.clip(d_c, 0, FAR_DIST)) - far[:, :, None], 0.0)
    return bias_cmp, bias_near, bias_win


def _cmp_mask_dist(seq):
    a = np.tile(np.arange(SEL_BLOCK), NSA_REP)[:, None]
    c = np.arange(seq // CMP_STRIDE)[None, :]
    return jnp.asarray(c * CMP_STRIDE + CMP_BLOCK - 1 - a, jnp.int32)


def _overlap_t(seq):
    ncp = seq // CMP_STRIDE
    cs = np.arange(ncp)[None, :] * CMP_STRIDE
    ss = np.arange(SEL_BLOCK)[:, None] * SEL_BLOCK
    ov = (cs < ss + SEL_BLOCK) & (cs + CMP_BLOCK - 1 >= ss) & (np.arange(ncp)[None, :] < (seq - CMP_BLOCK) // CMP_STRIDE + 1)
    return jnp.asarray(ov, BF16)


def _softmax_step(s, v, state):
    m, l, acc = state
    mn = jnp.maximum(m, jnp.max(s, axis=-1, keepdims=True))
    alpha = jnp.exp(m - mn)
    p = jnp.exp(s - mn)
    return mn, alpha * l + jnp.sum(p, axis=-1, keepdims=True), alpha * acc + _dot(p.astype(BF16), v)


FAR_CHUNK = 1024


def _lane_fold(x, op):
    out = x[:, 0:LANE]
    for c in range(1, x.shape[1] // LANE):
        out = op(out, x[:, LANE * c:LANE * (c + 1)])
    return out


SEL_TILE = 8


def _nsa_local_kernel(q_ref, kv_ref, kc_ref, vc_ref, g_ref, bc_ref, bw_ref, ovt_ref, dmask_ref, neg_ref, oc_ref, *, pad):
    t = pl.program_id(1)
    qb, rep, ng = SEL_BLOCK, NSA_REP, NSA_GROUPS
    rows = qb * rep
    gw = 4 * LANE
    ncp = kc_ref.shape[0]
    wk = WINDOW + qb
    m_i = lax.broadcasted_iota(jnp.int32, (CMP_NEAR, ncp), 0)
    c_i = lax.broadcasted_iota(jnp.int32, (CMP_NEAR, ncp), 1)
    w_lane = lax.broadcasted_iota(jnp.int32, (rows, wk), 1)
    dist = dmask_ref[...]
    ovt = ovt_ref[...]
    gates = g_ref[...]
    low = lax.broadcasted_iota(jnp.int32, (qb, LANE), 1) < NSA_DK
    jj = lax.broadcasted_iota(jnp.int32, (qb, ng * qb), 0)
    bases = []
    for g in range(ng):
        base = bc_ref[rep * g:rep * (g + 1)].reshape(rows, CMP_NEAR)
        b_hi = base.astype(BF16)
        bases.append((b_hi, (base - b_hi.astype(F32)).astype(BF16)))
    blocks = [(u, t * SEL_TILE + u, slice(qb * u, qb * (u + 1))) for u in range(SEL_TILE)]
    chains = [(u, i, tok, g) for u, i, tok in blocks for g in range(ng)]
    qq = [jnp.concatenate([q_ref[tok, gw * g + LANE * r:gw * g + LANE * (r + 1)] for r in range(rep)], axis=0)
          for _, _, tok, g in chains]
    shift = [jnp.where(c_i - m_i == (qb // CMP_STRIDE) * i - CMP_NEAR // 2, 1.0, 0.0).astype(BF16) for _, i, _ in blocks]
    s_c = [jnp.where(dist <= i * qb,
                     _dot_nt(qq[n], kc_ref[:, LANE * g:LANE * (g + 1)]) + _dot(bases[g][0], shift[u])
                     + _dot(bases[g][1], shift[u]), NEG) for n, (u, i, _, g) in enumerate(chains)]
    s_w = [jnp.where(w_lane + i * qb - WINDOW >= 0,
                     _dot_nt(qq[n], kv_ref[pl.ds(pl.multiple_of(pad - WINDOW + i * qb, qb), wk),
                                           gw * g + 2 * LANE:gw * g + 3 * LANE])
                     + bw_ref[rep * g:rep * (g + 1)].reshape(rows, wk), NEG) for n, (u, i, _, g) in enumerate(chains)]
    e_c = [jnp.exp(s - jnp.maximum(jnp.max(s, axis=-1, keepdims=True), -1e20)) for s in s_c]
    e_w = [jnp.exp(s - jnp.max(s, axis=-1, keepdims=True)) for s in s_w]
    inv = [1.0 / jnp.maximum(jnp.sum(e, axis=-1, keepdims=True), 1e-30) for e in e_c]
    o_c = [_dot(e_c[n].astype(BF16), vc_ref[:, LANE * g:LANE * (g + 1)]) * inv[n] for n, (_, _, _, g) in enumerate(chains)]
    pv_w = [_dot(e_w[n].astype(BF16), kv_ref[pl.ds(pl.multiple_of(pad - WINDOW + i * qb, qb), wk),
                                            gw * g + 3 * LANE:gw * g + 4 * LANE]) for n, (_, i, _, g) in enumerate(chains)]
    o_w = [pv / pv[:, NSA_DK:NSA_DK + 1] for pv in pv_w]
    for n, (_, _, tok, g) in enumerate(chains):
        part = []
        for r in range(rep):
            sl = slice(qb * r, qb * (r + 1))
            c0 = LANE * g + 3 * r
            part.append(gates[tok, c0:c0 + 1] * o_c[n][sl] + gates[tok, c0 + 2:c0 + 3] * o_w[n][sl])
        for pr in range(rep // 2):
            c0 = 2 * LANE * g + LANE * pr
            oc_ref[tok, c0:c0 + LANE] = jnp.where(low, part[2 * pr], pltpu.roll(part[2 * pr + 1], NSA_DK, 1))
    imps = []
    for u, i, _ in blocks:
        p = [e_c[ng * u + g] * inv[ng * u + g] for g in range(ng)]
        psum = jnp.concatenate([x[0:qb] + x[qb:2 * qb] + x[2 * qb:3 * qb] + x[3 * qb:4 * qb] for x in p], axis=0)
        hi = psum.astype(BF16)
        r1 = psum - hi.astype(F32)
        mid = r1.astype(BF16)
        lo = (r1 - mid.astype(F32)).astype(BF16)
        imp = _dot_nt(ovt, hi) + _dot_nt(ovt, mid) + _dot_nt(ovt, lo)
        imp = jnp.where((jj == i) | (jj == 0), FORCED_SCORE, imp)
        imps.append(jnp.where(jj > i, -1.0, imp))
    grp = [[imp[8 * a:8 * a + 8] for a in range(8)] for imp in imps]
    cnt = [[jnp.zeros((8, ng * qb), F32) for _ in range(8)] for _ in imps]
    sub = lax.broadcasted_iota(jnp.int32, (8, ng * qb), 0)
    for k in range(qb):
        for u in range(SEL_TILE):
            rk = imps[u][k:k + 1, :]
            for a in range(8):
                if 8 * a + 7 <= k:
                    cnt[u][a] = cnt[u][a] + jnp.where(rk > grp[u][a], 1.0, 0.0)
                elif 8 * a > k:
                    cnt[u][a] = cnt[u][a] + jnp.where(rk >= grp[u][a], 1.0, 0.0)
                else:
                    cnt[u][a] = cnt[u][a] + jnp.where(sub + 8 * a > k, jnp.where(rk >= grp[u][a], 1.0, 0.0),
                                                      jnp.where(rk > grp[u][a], 1.0, 0.0))
    for u in range(SEL_TILE):
        neg_ref[u] = jnp.where(jnp.concatenate(cnt[u], axis=0) < SEL_TOP_N, 0.0, SEL_MASK)


ATT_TILE = 2


def _nsa_attend_kernel(q_ref, kv_ref, g_ref, neg_ref, oc_ref, bn_ref, o_ref, s_ref, *, pad):
    t = pl.program_id(1)
    qb, rep, ng = SEL_BLOCK, NSA_REP, NSA_GROUPS
    rows = qb * rep
    gw = 4 * LANE
    blocks = [(u, t * ATT_TILE + u, slice(qb * u, qb * (u + 1))) for u in range(ATT_TILE)]
    chains = [(u, i, tok, g) for u, i, tok in blocks for g in range(ng)]
    nc = len(chains)
    qs = [jnp.concatenate([q_ref[tok, gw * g + LANE * r:gw * g + LANE * (r + 1)] for r in range(rep)], axis=0)
          for _, _, tok, g in chains]
    jj = lax.broadcasted_iota(jnp.int32, (qb, ng * qb), 0)
    q_lane = lax.broadcasted_iota(jnp.int32, (qb, ng * qb), 1)
    zero = jnp.zeros((qb, ng * qb), BF16)
    ext_near = [jnp.concatenate([zero, neg_ref[u].astype(BF16)], axis=0) for u, _, _ in blocks]
    ext_far = [jnp.concatenate([zero, jnp.where(jj >= i - 3, SEL_MASK, neg_ref[u]).astype(BF16)], axis=0)
               for u, i, _ in blocks]
    pick = [jnp.where(q_lane == jj + qb * g, 1.0, 0.0).astype(BF16) for g in range(ng)]
    q_near = [qs[n] + jnp.concatenate([_dot_nt(pick[g], ext_near[u]).astype(BF16)] * rep, axis=0)
              for n, (u, _, _, g) in enumerate(chains)]
    q_far = [qs[n] + jnp.concatenate([_dot_nt(pick[g], ext_far[u]).astype(BF16)] * rep, axis=0)
             for n, (u, _, _, g) in enumerate(chains)]

    n_lane = lax.broadcasted_iota(jnp.int32, (rows, 4 * qb), 1)
    near_at = [pl.multiple_of(pad + (i - 3) * qb, qb) for _, i, _ in blocks]
    s_near = [jnp.where(n_lane + (i - 3) * qb >= 0,
                        _dot_nt(q_near[n], kv_ref[pl.ds(near_at[u], 4 * qb), gw * g:gw * g + LANE])
                        + bn_ref[rep * g:rep * (g + 1)].reshape(rows, 4 * qb), NEG) for n, (u, i, _, g) in enumerate(chains)]

    nch = (jnp.maximum(blocks[-1][1] - 3, 0) * qb + FAR_CHUNK - 1) // FAR_CHUNK

    def pass1(c, mx):
        st = pl.multiple_of(pad + c * FAR_CHUNK, math.gcd(pad, FAR_CHUNK))
        col = pl.multiple_of(c * FAR_CHUNK, FAR_CHUNK)
        out = []
        for n, (_, _, _, g) in enumerate(chains):
            s = _dot_nt(q_far[n], kv_ref[pl.ds(st, FAR_CHUNK), gw * g:gw * g + LANE])
            s_ref[n, :, pl.ds(col, FAR_CHUNK)] = s
            out.append(jnp.maximum(mx[n], _lane_fold(s, jnp.maximum)))
        return tuple(out)

    mx = lax.fori_loop(0, nch, pass1, tuple(jnp.full((rows, LANE), NEG, F32) for _ in range(nc)))
    ms = [jnp.max(jnp.maximum(mx[n], _lane_fold(s_near[n], jnp.maximum)), axis=-1, keepdims=True) for n in range(nc)]

    def pass2(c, state):
        st = pl.multiple_of(pad + c * FAR_CHUNK, math.gcd(pad, FAR_CHUNK))
        col = pl.multiple_of(c * FAR_CHUNK, FAR_CHUNK)
        out = []
        for n, (_, _, _, g) in enumerate(chains):
            p = jnp.exp(s_ref[n, :, pl.ds(col, FAR_CHUNK)] - ms[n])
            out.append(state[n] + _dot(p.astype(BF16), kv_ref[pl.ds(st, FAR_CHUNK), gw * g + LANE:gw * g + 2 * LANE]))
        return tuple(out)

    far = lax.fori_loop(0, nch, pass2, tuple(jnp.zeros((rows, LANE), F32) for _ in range(nc)))
    p_near = [jnp.exp(s_near[n] - ms[n]).astype(BF16) for n in range(nc)]
    pv = [far[n] + _dot(p_near[n], kv_ref[pl.ds(near_at[u], 4 * qb), gw * g + LANE:gw * g + 2 * LANE])
          for n, (u, _, _, g) in enumerate(chains)]
    o_s = [x / x[:, NSA_DK:NSA_DK + 1] for x in pv]

    gates = g_ref[...]
    low = lax.broadcasted_iota(jnp.int32, (qb, LANE), 1) < NSA_DK
    for n, (_, _, tok, g) in enumerate(chains):
        outs = [gates[tok, LANE * g + 3 * r + 1:LANE * g + 3 * r + 2] * o_s[n][qb * r:qb * (r + 1)] for r in range(rep)]
        for pr in range(rep // 2):
            c0 = 2 * LANE * g + LANE * pr
            pair = jnp.where(low, outs[2 * pr], pltpu.roll(outs[2 * pr + 1], NSA_DK, 1))
            o_ref[tok, c0:c0 + LANE] = (oc_ref[tok, c0:c0 + LANE] + pair).astype(o_ref.dtype)


def _nsa_attention(qn, kv, kc, vc, gates, rel_bias, pad):
    bsz, seq, _ = qn.shape
    qb, ng = SEL_BLOCK, NSA_GROUPS
    nq = seq // qb
    ncp = seq // CMP_STRIDE
    assert nq <= qb and nq % SEL_TILE == 0 and nq % ATT_TILE == 0 and pad >= WINDOW and seq % FAR_CHUNK == 0
    bias_cmp, bias_near, bias_win = _bias_tables(rel_bias)
    ovt = _overlap_t(seq)
    dmask = _cmp_mask_dist(seq)
    width = NSA_HEADS * NSA_DK
    tile = lambda n, w: pl.BlockSpec((None, n, w), lambda b, i: (b, i, 0))
    per_b = lambda n, w: pl.BlockSpec((None, n, w), lambda b, i: (b, 0, 0))
    neg, o_local = pl.pallas_call(
        functools.partial(_nsa_local_kernel, pad=pad), grid=(bsz, nq // SEL_TILE),
        in_specs=[tile(SEL_TILE * qb, ng * 4 * LANE), per_b(seq + pad, ng * 4 * LANE), per_b(ncp, ng * LANE),
                  per_b(ncp, ng * LANE), tile(SEL_TILE * qb, ng * LANE), _const_spec(bias_cmp.shape),
                  _const_spec(bias_win.shape), _const_spec(ovt.shape), _const_spec(dmask.shape)],
        out_specs=[pl.BlockSpec((None, SEL_TILE, qb, ng * qb), lambda b, i: (b, i, 0, 0)), tile(SEL_TILE * qb, width)],
        out_shape=[jax.ShapeDtypeStruct((bsz, nq, qb, ng * qb), F32), jax.ShapeDtypeStruct((bsz, seq, width), F32)],
        compiler_params=_params("parallel", "arbitrary"), name="nsa_local",
    )(qn, kv, kc, vc, gates, bias_cmp, bias_win, ovt, dmask)
    return pl.pallas_call(
        functools.partial(_nsa_attend_kernel, pad=pad), grid=(bsz, nq // ATT_TILE),
        in_specs=[tile(ATT_TILE * qb, ng * 4 * LANE), per_b(seq + pad, ng * 4 * LANE), tile(ATT_TILE * qb, ng * LANE),
                  pl.BlockSpec((None, ATT_TILE, qb, ng * qb), lambda b, i: (b, i, 0, 0)), tile(ATT_TILE * qb, width),
                  _const_spec(bias_near.shape)],
        out_specs=tile(ATT_TILE * qb, width),
        out_shape=jax.ShapeDtypeStruct((bsz, seq, width), BF16),
        scratch_shapes=[pltpu.VMEM((ATT_TILE * ng, NSA_REP * qb, seq), F32)],
        compiler_params=_params("parallel", "arbitrary"), name="nsa_attention",
    )(qn, kv, gates, neg, o_local, bias_near)


def _mla_attn_kernel(q_ref, k_ref, v_ref, o_ref, *, tq, ck):
    i = pl.program_id(2)
    heads = [slice(LANE * hh, LANE * (hh + 1)) for hh in range(2)]
    qs = [q_ref[:, sl] for sl in heads]

    def body(c, states):
        st = pl.multiple_of(c * ck, ck)
        v = v_ref[pl.ds(st, ck), :]
        return tuple(_softmax_step(_dot_nt(q, k_ref[pl.ds(st, ck), sl]), v, state)
                     for q, sl, state in zip(qs, heads, states))

    init = (jnp.full((tq, 1), NEG, F32), jnp.zeros((tq, 1), F32), jnp.zeros((tq, LANE), F32))
    states = lax.fori_loop(0, i * (tq // ck), body, (init, init))
    st = pl.multiple_of(i * tq, tq)
    causal = lax.broadcasted_iota(jnp.int32, (tq, tq), 0) >= lax.broadcasted_iota(jnp.int32, (tq, tq), 1)
    outs = []
    for q, sl, state in zip(qs, heads, states):
        s = _dot_nt(q, k_ref[pl.ds(st, tq), sl])
        _, l, acc = _softmax_step(jnp.where(causal, s, NEG), v_ref[pl.ds(st, tq), :], state)
        outs.append(acc / l)
    low = lax.broadcasted_iota(jnp.int32, (tq, LANE), 1) < MLA_V
    o_ref[...] = jnp.where(low, outs[0], outs[1]).astype(o_ref.dtype)


def _mla_attention(qm, km, vm, tq=1024, ck=1024):
    bsz, seq, _ = qm.shape
    tq = min(tq, seq)
    return pl.pallas_call(
        functools.partial(_mla_attn_kernel, tq=tq, ck=min(ck, tq)), grid=(bsz, MLA_HEADS // 2, seq // tq),
        in_specs=[pl.BlockSpec((None, tq, 2 * LANE), lambda b, h, i: (b, i, h)),
                  pl.BlockSpec((None, seq, 2 * LANE), lambda b, h, i: (b, 0, h)),
                  pl.BlockSpec((None, seq, LANE), lambda b, h, i: (b, 0, h))],
        out_specs=pl.BlockSpec((None, tq, LANE), lambda b, h, i: (b, i, h)),
        out_shape=jax.ShapeDtypeStruct((bsz, seq, MLA_HEADS * MLA_V), BF16),
        compiler_params=_params("parallel", "parallel", "arbitrary"), name="mla_attention",
    )(qm, km, vm)


def _even_out_kernel(x_ref, on_ref, om_ref, wn_ref, wm_ref, g_ref, wg_ref, wu_ref, wd_ref, o_ref, *, chunk):
    x1 = x_ref[...] + _dot(on_ref[...], wn_ref[...]) + _dot(om_ref[...], wm_ref[...])
    n = _rms(x1, g_ref[...], x1.shape[-1]).astype(BF16)
    ffn = None
    for f0 in range(0, wg_ref.shape[1], chunk):
        gate = _dot(n, wg_ref[:, f0:f0 + chunk])
        act = (gate * jax.nn.sigmoid(gate) * _dot(n, wu_ref[:, f0:f0 + chunk])).astype(BF16)
        part = _dot(act, wd_ref[f0:f0 + chunk, :])
        ffn = part if ffn is None else ffn + part
    o_ref[...] = x1 + ffn


def _even_out_ffn(x, o_nsa, o_mla, w_out, gain, w_gate, w_up, w_down, tm=ROW_TILE):
    t, d = x.shape
    dff = w_gate.shape[1]
    wn, wm = w_out[:o_nsa.shape[1]].astype(BF16), w_out[o_nsa.shape[1]:].astype(BF16)
    row = lambda width: pl.BlockSpec((tm, width), lambda i: (i, 0))
    return pl.pallas_call(
        functools.partial(_even_out_kernel, chunk=dff // 2), grid=(t // tm,),
        in_specs=[row(d), row(o_nsa.shape[1]), row(o_mla.shape[1]), _const_spec(wn.shape), _const_spec(wm.shape),
                  _const_spec((1, d)), _const_spec((d, dff)), _const_spec((d, dff)), _const_spec((dff, d))],
        out_specs=row(d), out_shape=jax.ShapeDtypeStruct((t, d), F32),
        compiler_params=_params("parallel"), name="even_out_ffn",
    )(x, o_nsa, o_mla, wn, wm, gain[None, :], w_gate.astype(BF16), w_up.astype(BF16), w_down.astype(BF16))


def _conv_kernel(x_ref, g_ref, win_ref, cw_ref, wout_ref, g2_ref, wr_ref, br_ref, tri_ref,
                 o_ref, route_ref, cnt_ref, vbuf_ref, carry_ref, *, tm):
    j = pl.program_id(1)
    x = x_ref[...]
    d = x.shape[-1]
    n = _rms(x, g_ref[...], d).astype(BF16)
    b_gate = _dot(n, win_ref[:, 0:d])
    v = _dot(n, win_ref[:, d:2 * d]) * _dot(n, win_ref[:, 2 * d:3 * d])

    @pl.when(j == 0)
    def _():
        vbuf_ref[0:8, :] = jnp.zeros((8, d), F32)

    vbuf_ref[8:8 + tm, :] = v
    cw = cw_ref[...]
    y = cw[2:3] * v + cw[1:2] * vbuf_ref[7:7 + tm, :] + cw[0:1] * vbuf_ref[6:6 + tm, :]
    vbuf_ref[0:8, :] = v[tm - 8:tm]
    out = x + _dot((b_gate * y).astype(BF16), wout_ref[...])
    o_ref[...] = out
    _route_tile(out, (pl.program_id(0) == 0) & (j == 0), g2_ref, wr_ref, br_ref, tri_ref, route_ref, cnt_ref,
                carry_ref, tm)


def _conv_mixer(x, gain, w_in, conv_w, w_out, ffn_gain, w_router, b_router, tm=ROW_TILE):
    bsz, seq, d = x.shape
    nt = seq // tm
    cw = jnp.pad(conv_w.astype(F32), ((0, 8 - CONV_WIDTH), (0, 0)))
    wr, br, tri = _router_operands(w_router, b_router, tm)
    tok = pl.BlockSpec((None, tm, d), lambda b, j: (b, j, 0))
    out, route, cnt = pl.pallas_call(
        functools.partial(_conv_kernel, tm=tm), grid=(bsz, nt),
        in_specs=[tok, _const_spec((1, d)), _const_spec((d, 3 * d)), _const_spec((8, d)), _const_spec((d, d)),
                  _const_spec((1, d)), _const_spec(wr.shape), _const_spec((1, LANE)), _const_spec((tm, tm))],
        out_specs=[tok, pl.BlockSpec((tm, LANE), lambda b, j: (b * nt + j, 0)),
                   pl.BlockSpec((8, LANE), lambda b, j: (0, 0))],
        out_shape=[jax.ShapeDtypeStruct((bsz, seq, d), F32), jax.ShapeDtypeStruct((bsz * seq, LANE), F32),
                   jax.ShapeDtypeStruct((8, LANE), F32)],
        scratch_shapes=[pltpu.VMEM((tm + 8, d), F32), pltpu.VMEM((8, LANE), F32)],
        compiler_params=_params("arbitrary", "arbitrary"), name="conv_mixer",
    )(x, gain[None, :], w_in.astype(BF16), cw, w_out.astype(BF16), ffn_gain[None, :], wr, br, tri)
    return out, route, cnt[0, :N_EXPERTS].astype(jnp.int32)


def _route_tile(x, first, g_ref, wr_ref, br_ref, tri_ref, route_ref, cnt_ref, carry_ref, tm):
    @pl.when(first)
    def _():
        carry_ref[...] = jnp.zeros_like(carry_ref)

    n = _rms(x, g_ref[...], x.shape[-1])
    hi = n.astype(BF16)
    lo = (n - hi.astype(F32)).astype(BF16)
    whi, wlo = wr_ref[0], wr_ref[1]
    logits = _dot(hi, whi) + _dot(lo, whi) + _dot(hi, wlo) + br_ref[...]
    lane = lax.broadcasted_iota(jnp.int32, (tm, LANE), 1).astype(F32)
    big = float(LANE)
    m1 = jnp.max(logits, axis=-1, keepdims=True)
    e1 = jnp.min(jnp.where(logits == m1, lane, big), axis=-1, keepdims=True)
    rest = jnp.where(lane == e1, NEG, logits)
    m2 = jnp.max(rest, axis=-1, keepdims=True)
    e2 = jnp.min(jnp.where(rest == m2, lane, big), axis=-1, keepdims=True)
    z = jnp.exp(m2 - m1)
    w1 = 1.0 / (1.0 + z)
    w2 = z / (1.0 + z)
    oh1 = jnp.where(lane == e1, 1.0, 0.0)
    oh2 = jnp.where(lane == e2, 1.0, 0.0)
    both = oh1 + oh2
    before = _dot(tri_ref[...], both.astype(BF16)) + carry_ref[0:1, :]
    r1 = jnp.sum(oh1 * before, axis=-1, keepdims=True)
    r2 = jnp.sum(oh2 * before, axis=-1, keepdims=True)
    cols = [e1, e2, w1, w2, r1, r2]
    out = jnp.zeros((tm, LANE), F32)
    for c, val in enumerate(cols):
        out = jnp.where(lane == c, val, out)
    route_ref[...] = out
    carry_ref[0:1, :] = carry_ref[0:1, :] + jnp.sum(both, axis=0, keepdims=True)
    cnt_ref[...] = carry_ref[...]


def _router_operands(w_router, b_router, tm):
    wr = jnp.pad(w_router.astype(F32), ((0, 0), (0, LANE - N_EXPERTS)))
    whi = wr.astype(BF16)
    wlo = (wr - whi.astype(F32)).astype(BF16)
    br = jnp.concatenate([b_router.astype(F32), jnp.full((LANE - N_EXPERTS,), NEG, F32)])[None, :]
    tri = jnp.asarray(np.tril(np.ones((tm, tm), np.float32), -1), BF16)
    return jnp.stack([whi, wlo]), br, tri


def _row_copy(src, i, dst, j, sem):
    return pltpu.make_async_copy(src.at[pl.ds(i, 1)], dst.at[pl.ds(j, 1)], sem)


def _scatter_kernel(fill_ref, dest_ref, x_ref, g_ref, xs_ref, xn_ref, zero_ref, sems, *, tm, rb):
    @pl.when(pl.program_id(0) == 0)
    def _():
        sem = sems.at[0]
        zero_ref[...] = jnp.zeros_like(zero_ref)
        sizes = [rb >> s for s in range(rb.bit_length() - 3)]
        for e in range(N_EXPERTS + 1):
            lo, n = fill_ref[e], fill_ref[N_EXPERTS + 1 + e] - fill_ref[e]
            whole = n // rb

            def copy(off, size):
                return pltpu.make_async_copy(zero_ref.at[pl.ds(0, size)],
                                             xs_ref.at[pl.ds(pl.multiple_of(off, 8), size)], sem)

            def blocks(k, c, lo=lo, copy=copy):
                copy(lo + k * rb, rb).start()
                copy(lo + k * rb, rb).wait()
                return c

            lax.fori_loop(0, whole, blocks, 0)
            off = lo + whole * rb
            for size in sizes[1:]:
                @pl.when((n & size) != 0)
                def _(off=off, size=size, copy=copy):
                    copy(off, size).start()
                    copy(off, size).wait()
                off = off + (n & size)

    t = pl.program_id(0)
    slot = t % 2
    rows, row_sem = xn_ref.at[slot], sems.at[slot]
    rows[...] = _rms(x_ref[...], g_ref[...], x_ref.shape[-1])

    def start(r, c):
        _row_copy(rows, r, xs_ref, dest_ref[0, 0, r], row_sem).start()
        _row_copy(rows, r, xs_ref, dest_ref[0, 0, tm + r], row_sem).start()
        return c

    lax.fori_loop(0, tm, start, 0, unroll=DMA_ISSUE_UNROLL)

    def drain(s):
        for _ in range(TOP_K):
            pltpu.make_async_copy(xn_ref.at[s], xs_ref.at[pl.ds(0, tm)], sems.at[s]).wait()

    @pl.when(t > 0)
    def _():
        drain(1 - slot)

    @pl.when(t == pl.num_programs(0) - 1)
    def _():
        drain(slot)


def _moe_scatter(x, gain, dest, fill, n_rows, tm, rb):
    t, d = x.shape
    nt = t // tm
    grid_spec = pltpu.PrefetchScalarGridSpec(
        num_scalar_prefetch=1, grid=(nt,),
        in_specs=[pl.BlockSpec((1, 1, 2 * tm), lambda i, f: (i, 0, 0), memory_space=pltpu.SMEM),
                  pl.BlockSpec((tm, d), lambda i, f: (i, 0)),
                  pl.BlockSpec((1, d), lambda i, f: (0, 0))],
        out_specs=pl.BlockSpec(memory_space=pl.ANY),
        scratch_shapes=[pltpu.VMEM((2, tm, d), F32), pltpu.VMEM((rb, d), F32), pltpu.SemaphoreType.DMA((2,))])
    return pl.pallas_call(
        functools.partial(_scatter_kernel, tm=tm, rb=rb), grid_spec=grid_spec,
        out_shape=jax.ShapeDtypeStruct((n_rows, d), F32),
        compiler_params=_params("arbitrary"), name="moe_scatter",
    )(fill, dest, x, gain[None, :])


def _expert_kernel(be_ref, nu_ref, x_ref, wg_ref, wu_ref, wd_ref, o_ref):
    used = pl.program_id(0) < nu_ref[0]

    @pl.when(used)
    def _():
        xb = x_ref[...].astype(BF16)
        gate = _dot(xb, wg_ref[...])
        act = (gate * jax.nn.sigmoid(gate) * _dot(xb, wu_ref[...])).astype(BF16)
        o_ref[...] = _dot(act, wd_ref[...])

    @pl.when(jnp.logical_not(used))
    def _():
        o_ref[...] = jnp.zeros_like(o_ref)


def _moe_experts(xs, blk_expert, n_used, w_gate, w_up, w_down, rb):
    n_rows, d = xs.shape
    dff = w_gate.shape[-1]
    rowblk = pl.BlockSpec((rb, d), lambda i, be, nu: (i, 0))
    wspec = lambda shape: pl.BlockSpec((None,) + shape, lambda i, be, nu: (be[i], 0, 0))
    grid_spec = pltpu.PrefetchScalarGridSpec(
        num_scalar_prefetch=2, grid=(n_rows // rb,),
        in_specs=[rowblk, wspec((d, dff)), wspec((d, dff)), wspec((dff, d))], out_specs=rowblk)
    return pl.pallas_call(
        _expert_kernel, grid_spec=grid_spec, out_shape=jax.ShapeDtypeStruct(xs.shape, F32),
        compiler_params=_params("arbitrary"), name="moe_experts",
    )(blk_expert, n_used, xs, w_gate.astype(BF16), w_up.astype(BF16), w_down.astype(BF16))


def _combine_kernel(dest_ref, next_ref, x_ref, route_ref, ys_ref, o_ref, got_ref, sems, *, tm):
    t = pl.program_id(0)
    slot = t % 2

    def gather(idx_ref, s):
        def start(r, c):
            for k in range(TOP_K):
                _row_copy(ys_ref, idx_ref[0, 0, k * tm + r], got_ref.at[s, k], r, sems.at[s]).start()
            return c

        lax.fori_loop(0, tm, start, 0, unroll=DMA_ISSUE_UNROLL)

    @pl.when(t == 0)
    def _():
        gather(dest_ref, 0)

    @pl.when(t + 1 < pl.num_programs(0))
    def _():
        gather(next_ref, 1 - slot)

    for k in range(TOP_K):
        pltpu.make_async_copy(ys_ref.at[pl.ds(0, tm)], got_ref.at[slot, k], sems.at[slot]).wait()
    route = route_ref[...]
    o_ref[...] = x_ref[...] + route[:, 2:3] * got_ref[slot, 0] + route[:, 3:4] * got_ref[slot, 1]


def _moe_combine(x, route, dest, ys, tm):
    t, d = x.shape
    nt = t // tm
    idx = lambda nxt: pl.BlockSpec((1, 1, TOP_K * tm), lambda i: (jnp.minimum(i + nxt, nt - 1), 0, 0),
                                   memory_space=pltpu.SMEM)
    return pl.pallas_call(
        functools.partial(_combine_kernel, tm=tm), grid=(nt,),
        in_specs=[idx(0), idx(1), pl.BlockSpec((tm, d), lambda i: (i, 0)), pl.BlockSpec((tm, LANE), lambda i: (i, 0)),
                  pl.BlockSpec(memory_space=pl.ANY)],
        out_specs=pl.BlockSpec((tm, d), lambda i: (i, 0)),
        out_shape=jax.ShapeDtypeStruct((t, d), F32),
        scratch_shapes=[pltpu.VMEM((2, TOP_K, tm, d), F32), pltpu.SemaphoreType.DMA((2,))],
        compiler_params=_params("arbitrary"), name="moe_combine",
    )(dest, dest, x, route, ys)


def _moe(x, gain, route, counts, w_gate, w_up, w_down, rb=MOE_ROW_BLOCK, tm=MOE_DMA_TILE):
    t, d = x.shape
    n_blk = (t * TOP_K + rb - 1) // rb + N_EXPERTS
    padded = (counts + rb - 1) // rb * rb
    pad_end = jnp.cumsum(padded)
    pad_start = pad_end - padded
    dest = pad_start[route[:, 0:2].astype(jnp.int32)] + route[:, 4:6].astype(jnp.int32)
    dest_t = jnp.concatenate([dest[:, 0].reshape(t // tm, 1, tm), dest[:, 1].reshape(t // tm, 1, tm)], axis=2)
    fill = jnp.concatenate([(pad_start + counts) // 8 * 8, pad_end[-1:], pad_end, jnp.full((1,), n_blk * rb, jnp.int32)])
    n_used = pad_end[-1:] // rb
    blk = jnp.minimum(jnp.arange(n_blk, dtype=jnp.int32), n_used - 1) * rb
    blk_expert = jnp.sum(pad_end[None, :] <= blk[:, None], axis=1).astype(jnp.int32)
    xs = _moe_scatter(x, gain, dest_t, fill.astype(jnp.int32), n_blk * rb, tm, rb)
    ys = _moe_experts(xs, blk_expert, n_used.astype(jnp.int32), w_gate, w_up, w_down, rb)
    return _moe_combine(x, route, dest_t, ys, tm)


def kernel(x, rel_bias, ev_mix_norm, ev_w_in, nsa_q_norm, nsa_k_norm, nsa_cmp_pos, nsa_cmp_w1, nsa_cmp_w2, mla_cq_norm, mla_ckv_norm, mla_w_uq, mla_w_ukv, mla_q_norm, mla_k_norm, ev_w_out, ev_ffn_norm, ffn_w_gate, ffn_w_up, ffn_w_down, od_mix_norm, od_w_in, conv_w, od_w_out, od_ffn_norm, moe_w_router, moe_b_router, moe_w_gate, moe_w_up, moe_w_down):
    bsz, seq, d = x.shape
    depth = ev_mix_norm.shape[0] + od_mix_norm.shape[0]
    pad = KV_PAD
    for layer in range(depth):
        i = layer // 2
        if layer % 2 == 0:
            qn, kv, kc_raw, vc_raw, gates, qm, km, vm = _even_in_proj(
                x, ev_mix_norm[i], ev_w_in[i], nsa_q_norm[i], nsa_k_norm[i], mla_cq_norm[i], mla_ckv_norm[i],
                mla_w_uq[i], mla_w_ukv[i], mla_q_norm[i], mla_k_norm[i], tm=pad)
            kc = _compress(kc_raw, nsa_cmp_pos[i, 0], nsa_cmp_w1[i, 0], nsa_cmp_w2[i, 0], nsa_k_norm[i, 0], True)
            vc = _compress(vc_raw, nsa_cmp_pos[i, 1], nsa_cmp_w1[i, 1], nsa_cmp_w2[i, 1], nsa_k_norm[i, 0], False)
            o_nsa = _nsa_attention(qn, kv, kc, vc, gates, rel_bias, pad)
            o_mla = _mla_attention(qm, km, vm)
            x = _even_out_ffn(x.reshape(bsz * seq, d), o_nsa.reshape(bsz * seq, -1), o_mla.reshape(bsz * seq, -1),
                              ev_w_out[i], ev_ffn_norm[i], ffn_w_gate[i], ffn_w_up[i], ffn_w_down[i]).reshape(bsz, seq, d)
        else:
            x, route, counts = _conv_mixer(x, od_mix_norm[i], od_w_in[i], conv_w[i], od_w_out[i], od_ffn_norm[i],
                                           moe_w_router[i], moe_b_router[i])
            x = _moe(x.reshape(bsz * seq, d), od_ffn_norm[i], route, counts,
                     moe_w_gate[i], moe_w_up[i], moe_w_down[i]).reshape(bsz, seq, d)
    return x
```
